```python
import math
import numpy as np
import jax
import jax.numpy as jnp
from jax import lax

D_MODEL = 2048
BATCH = 2
SEQ = 8192
DEPTH = 4

N_MIXERS = 4
Q_BLOCK = 128
ROPE_THETA = 10000.0
NORM_EPS = 1e-6

DA_QK_DIM = 64
DA_V_DIM = 2 * DA_QK_DIM
DA_HEADS = D_MODEL // DA_V_DIM

NSA_HEADS = 16
NSA_GROUPS = 4
NSA_HEAD_DIM = D_MODEL // NSA_HEADS
NSA_CMP_BLOCK = 32
NSA_CMP_STRIDE = 16
NSA_SLC_BLOCK = 64
NSA_SLC_TOPK = 16
NSA_WINDOW = 512

MLA_HEADS = 16
MLA_Q_RANK = 512
MLA_KV_RANK = 512
MLA_NOPE_DIM = 128
MLA_ROPE_DIM = 64
MLA_V_DIM = 128

SB_HEADS = 16
SB_HEAD_DIM = D_MODEL // SB_HEADS

FF_DIM = 5632
N_EXPERTS = 8
TOP_K = 2
EXPERT_FF_DIM = 5632
MOE_ROW_BLOCK = 512

kernel_name = 'hybrid_diff_nsa_mla_stickbreak_moe_trunk'


def _rms_norm(x, w):
    xf = x.astype(jnp.float32)
    y = xf * lax.rsqrt(jnp.mean(xf * xf, axis=-1, keepdims=True) + NORM_EPS)
    return (y * w.astype(jnp.float32)).astype(x.dtype)


def _rope(x, pos):
    half = x.shape[-1] // 2
    inv_freq = ROPE_THETA ** (-jnp.arange(half, dtype=jnp.float32) / half)
    ang = pos.astype(jnp.float32)[:, None] * inv_freq[None, :]
    cos, sin = jnp.cos(ang), jnp.sin(ang)
    xf = x.astype(jnp.float32)
    x1, x2 = xf[..., :half], xf[..., half:]
    return jnp.concatenate([x1 * cos - x2 * sin, x1 * sin + x2 * cos], axis=-1).astype(x.dtype)


def _masked_softmax(s, mask):
    s = jnp.where(mask, s.astype(jnp.float32), -jnp.inf)
    m = jnp.max(s, axis=-1, keepdims=True)
    e = jnp.exp(s - jnp.where(jnp.isfinite(m), m, 0.0))
    den = jnp.sum(e, axis=-1, keepdims=True)
    return e / jnp.where(den > 0.0, den, 1.0)


def _merge_heads(out):
    nb, b, h, q, d = out.shape
    return out.transpose(1, 0, 3, 2, 4).reshape(b, nb * q, h * d)


def diff_attention(h, w_in, lam, subln_w, w_out, lam_init):
    B, S, _ = h.shape
    H, dk = DA_HEADS, DA_QK_DIM
    pos = jnp.arange(S)
    proj = (h @ w_in).reshape(B, S, H, 4 * dk + DA_V_DIM).transpose(0, 2, 1, 3)
    q1 = _rope(proj[..., :dk], pos)
    q2 = _rope(proj[..., dk:2 * dk], pos)
    k1 = _rope(proj[..., 2 * dk:3 * dk], pos)
    k2 = _rope(proj[..., 3 * dk:4 * dk], pos)
    v = proj[..., 4 * dk:]
    lf = lam.astype(jnp.float32)
    lam_full = jnp.exp(jnp.sum(lf[0] * lf[1])) - jnp.exp(jnp.sum(lf[2] * lf[3])) + lam_init
    scale = dk ** -0.5

    def block(b):
        start = b * Q_BLOCK
        qpos = start + jnp.arange(Q_BLOCK)
        mask = pos[None, :] <= qpos[:, None]
        qa = lax.dynamic_slice_in_dim(q1, start, Q_BLOCK, axis=2)
        qb = lax.dynamic_slice_in_dim(q2, start, Q_BLOCK, axis=2)
        p1 = _masked_softmax(jnp.einsum('bhqd,bhkd->bhqk', qa, k1) * scale, mask)
        p2 = _masked_softmax(jnp.einsum('bhqd,bhkd->bhqk', qb, k2) * scale, mask)
        o = jnp.einsum('bhqk,bhkd->bhqd', (p1 - lam_full * p2).astype(v.dtype), v)
        return _rms_norm(o, subln_w) * (1.0 - lam_init)

    return _merge_heads(lax.map(block, jnp.arange(S // Q_BLOCK))) @ w_out


def nsa_attention(h, w_in, cmp_pe, cmp_w, w_out):
    B, S, _ = h.shape
    H, G, dh = NSA_HEADS, NSA_GROUPS, NSA_HEAD_DIM
    R = H // G
    kvw = G * dh
    pos = jnp.arange(S)
    proj = h @ w_in
    q = _rope(proj[..., :H * dh].reshape(B, S, G, R, dh).transpose(0, 2, 3, 1, 4), pos)

    def kv(i):
        lo = H * dh + i * kvw
        return proj[..., lo:lo + kvw].reshape(B, S, G, dh).transpose(0, 2, 1, 3)

    k_cmp_tok, v_cmp_tok, k_slc, v_slc, k_win, v_win = [kv(i) for i in range(6)]
    gates = jax.nn.sigmoid(proj[..., H * dh + 6 * kvw:].astype(jnp.float32)).astype(h.dtype)
    gates = gates.reshape(B, S, 3, G, R).transpose(2, 0, 3, 4, 1)

    n_chunk = S // NSA_CMP_STRIDE
    r = NSA_CMP_BLOCK // NSA_CMP_STRIDE
    n_cmp = n_chunk - r + 1
    cmp_end = jnp.arange(n_cmp) * NSA_CMP_STRIDE + NSA_CMP_BLOCK - 1

    def compress(x, pe, w):
        xc = x.reshape(B, G, n_chunk, NSA_CMP_STRIDE, dh)
        blocks = jnp.concatenate([xc[:, :, o:o + n_cmp] for o in range(r)], axis=3)
        return jnp.einsum('bgnld,lde->bgne', blocks + pe, w)

    k_cmp = _rope(compress(k_cmp_tok, cmp_pe[0], cmp_w[0]), cmp_end)
    v_cmp = compress(v_cmp_tok, cmp_pe[1], cmp_w[1])

    n_slc = S // NSA_SLC_BLOCK
    n_sel = min(NSA_SLC_TOPK, n_slc)
    cs = np.arange(n_cmp)[:, None] * NSA_CMP_STRIDE
    js = np.arange(n_slc)[None, :] * NSA_SLC_BLOCK
    overlap = jnp.asarray(((cs < js + NSA_SLC_BLOCK) & (cs + NSA_CMP_BLOCK > js)).astype(np.float32))
    k_slc_blocks = k_slc.reshape(B, G, n_slc, NSA_SLC_BLOCK, dh)
    v_slc_blocks = v_slc.reshape(B, G, n_slc, NSA_SLC_BLOCK, dh)
    gather = jax.vmap(jax.vmap(lambda blocks, ids: blocks[ids]))
    slc_id = jnp.arange(n_slc)

    k_win_pad = jnp.pad(k_win, ((0, 0), (0, 0), (NSA_WINDOW, 0), (0, 0)))
    v_win_pad = jnp.pad(v_win, ((0, 0), (0, 0), (NSA_WINDOW, 0), (0, 0)))
    scale = dh ** -0.5

    def block(b):
        start = b * Q_BLOCK
        qpos = start + jnp.arange(Q_BLOCK)
        qb = lax.dynamic_slice_in_dim(q, start, Q_BLOCK, axis=3)
        m_c = cmp_end[None, :] <= qpos[:, None]
        p_c = _masked_softmax(jnp.einsum('bgrqd,bgnd->bgrqn', qb, k_cmp) * scale, m_c)
        o_c = jnp.einsum('bgrqn,bgnd->bgrqd', p_c.astype(v_cmp.dtype), v_cmp)
        imp = jnp.einsum('bgqn,nj->bgqj', jnp.sum(p_c, axis=2), overlap)
        cur = qpos[:, None] // NSA_SLC_BLOCK
        forced = (slc_id[None, :] == 0) | (slc_id[None, :] == cur) | (slc_id[None, :] == cur - 1)
        causal = slc_id[None, :] <= cur
        imp = jnp.where(forced, jnp.inf, jnp.where(causal, imp, -jnp.inf))
        _, idx = lax.top_k(imp, n_sel)
        kg = gather(k_slc_blocks, idx).reshape(B, G, Q_BLOCK, n_sel * NSA_SLC_BLOCK, dh)
        vg = gather(v_slc_blocks, idx).reshape(B, G, Q_BLOCK, n_sel * NSA_SLC_BLOCK, dh)
        kpos = (idx[..., None] * NSA_SLC_BLOCK + jnp.arange(NSA_SLC_BLOCK)).reshape(B, G, Q_BLOCK, -1)
        m_s = (kpos <= qpos[:, None])[:, :, None]
        p_s = _masked_softmax(jnp.einsum('bgrqd,bgqkd->bgrqk', qb, kg) * scale, m_s)
        o_s = jnp.einsum('bgrqk,bgqkd->bgrqd', p_s.astype(vg.dtype), vg)
        kw = lax.dynamic_slice_in_dim(k_win_pad, start, NSA_WINDOW + Q_BLOCK, axis=2)
        vw = lax.dynamic_slice_in_dim(v_win_pad, start, NSA_WINDOW + Q_BLOCK, axis=2)
        wpos = start - NSA_WINDOW + jnp.arange(NSA_WINDOW + Q_BLOCK)
        m_w = ((wpos[None, :] <= qpos[:, None]) & (wpos[None, :] > qpos[:, None] - NSA_WINDOW)
               & (wpos[None, :] >= 0))
        p_w = _masked_softmax(jnp.einsum('bgrqd,bgkd->bgrqk', qb, kw) * scale, m_w)
        o_w = jnp.einsum('bgrqk,bgkd->bgrqd', p_w.astype(vw.dtype), vw)
        g = lax.dynamic_slice_in_dim(gates, start, Q_BLOCK, axis=4)[..., None]
        return g[0] * o_c + g[1] * o_s + g[2] * o_w

    out = lax.map(block, jnp.arange(S // Q_BLOCK))
    out = out.transpose(1, 0, 4, 2, 3, 5).reshape(B, S, H * dh)
    return out @ w_out


def mla_attention(h, w_in, q_norm_w, w_uq, kv_norm_w, w_ukv, w_out):
    B, S, _ = h.shape
    H = MLA_HEADS
    pos = jnp.arange(S)
    proj = h @ w_in
    c_q = _rms_norm(proj[..., :MLA_Q_RANK], q_norm_w)
    c_kv = _rms_norm(proj[..., MLA_Q_RANK:MLA_Q_RANK + MLA_KV_RANK], kv_norm_w)
    k_rope = _rope(proj[..., MLA_Q_RANK + MLA_KV_RANK:], pos)
    q = (c_q @ w_uq).reshape(B, S, H, MLA_NOPE_DIM + MLA_ROPE_DIM).transpose(0, 2, 1, 3)
    q_nope = q[..., :MLA_NOPE_DIM]
    q_rope = _rope(q[..., MLA_NOPE_DIM:], pos)
    kv = jnp.einsum('bsc,chd->bhsd', c_kv, w_ukv)
    k_nope, v = kv[..., :MLA_NOPE_DIM], kv[..., MLA_NOPE_DIM:]
    scale = (MLA_NOPE_DIM + MLA_ROPE_DIM) ** -0.5

    def block(b):
        start = b * Q_BLOCK
        qpos = start + jnp.arange(Q_BLOCK)
        mask = pos[None, :] <= qpos[:, None]
        qn = lax.dynamic_slice_in_dim(q_nope, start, Q_BLOCK, axis=2)
        qr = lax.dynamic_slice_in_dim(q_rope, start, Q_BLOCK, axis=2)
        s = jnp.einsum('bhqd,bhkd->bhqk', qn, k_nope) + jnp.einsum('bhqd,bkd->bhqk', qr, k_rope)
        p = _masked_softmax(s * scale, mask)
        return jnp.einsum('bhqk,bhkd->bhqd', p.astype(v.dtype), v)

    return _merge_heads(lax.map(block, jnp.arange(S // Q_BLOCK))) @ w_out


def stick_breaking_attention(h, w_in, w_out):
    B, S, _ = h.shape
    H, dh = SB_HEADS, SB_HEAD_DIM
    pos = jnp.arange(S)
    qkv = (h @ w_in).reshape(B, S, 3, H, dh).transpose(2, 0, 3, 1, 4)
    q, k, v = qkv[0], qkv[1], qkv[2]
    scale = dh ** -0.5

    def block(b):
        start = b * Q_BLOCK
        qpos = start + jnp.arange(Q_BLOCK)
        mask = pos[None, :] < qpos[:, None]
        qb = lax.dynamic_slice_in_dim(q, start, Q_BLOCK, axis=2)
        z = jnp.einsum('bhqd,bhkd->bhqk', qb, k).astype(jnp.float32) * scale
        log_beta = jax.nn.log_sigmoid(z)
        log_rest = jnp.where(mask, jax.nn.log_sigmoid(-z), 0.0)
        tail = lax.cumsum(log_rest, axis=3, reverse=True) - log_rest
        a = jnp.where(mask, jnp.exp(log_beta + tail), 0.0)
        return jnp.einsum('bhqk,bhkd->bhqd', a.astype(v.dtype), v)

    return _merge_heads(lax.map(block, jnp.arange(S // Q_BLOCK))) @ w_out


def swiglu(h, w_gu, w_down):
    g, u = jnp.split(h @ w_gu, 2, axis=-1)
    return (jax.nn.silu(g) * u) @ w_down


def moe_swiglu(h, w_router, b_router, w_gu, w_down):
    B, S, D = h.shape
    N = B * S
    xt = h.reshape(N, D)
    logits = (xt @ w_router).astype(jnp.float32) + b_router.astype(jnp.float32)
    top_val, top_idx = lax.top_k(logits, TOP_K)
    gate = jax.nn.softmax(top_val, axis=-1)
    nk = N * TOP_K
    e_flat = top_idx.reshape(nk)
    tok = jnp.repeat(jnp.arange(N), TOP_K)
    g_flat = gate.reshape(nk)
    order = jnp.argsort(e_flat)
    e_s, tok_s, g_s = e_flat[order], tok[order], g_flat[order]
    counts = jnp.bincount(e_flat, length=N_EXPERTS)
    padded = (counts + MOE_ROW_BLOCK - 1) // MOE_ROW_BLOCK * MOE_ROW_BLOCK
    start = jnp.cumsum(counts) - counts
    pend = jnp.cumsum(padded)
    pstart = pend - padded
    dest = pstart[e_s] + (jnp.arange(nk) - start[e_s])
    n_rows = (-(-nk // MOE_ROW_BLOCK)) * MOE_ROW_BLOCK + N_EXPERTS * MOE_ROW_BLOCK
    n_blocks = n_rows // MOE_ROW_BLOCK
    xbuf = jnp.zeros((n_rows, D), h.dtype).at[dest].set(xt[tok_s])
    blk_expert = jnp.minimum(
        jnp.searchsorted(pend, jnp.arange(n_blocks) * MOE_ROW_BLOCK, side='right'), N_EXPERTS - 1)

    def expert_block(args):
        xb, e = args
        g, u = jnp.split(xb @ w_gu[e], 2, axis=-1)
        return (jax.nn.silu(g) * u) @ w_down[e]

    ybuf = lax.map(expert_block, (xbuf.reshape(n_blocks, MOE_ROW_BLOCK, D), blk_expert)).reshape(n_rows, D)
    y = jnp.zeros((N, D), h.dtype).at[tok_s].add(ybuf[dest] * g_s[:, None].astype(h.dtype))
    return y.reshape(B, S, D)


def setup_inputs(seed: int = 0) -> dict:
    key = jax.random.key(seed)
    keys = jax.random.split(key, 26)
    D = D_MODEL
    n_da = (DEPTH + 3) // 4
    n_nsa = (DEPTH + 2) // 4
    n_mla = (DEPTH + 1) // 4
    n_sb = DEPTH // 4
    n_dense = (DEPTH + 1) // 2
    n_moe = DEPTH // 2
    nsa_in = NSA_HEADS * NSA_HEAD_DIM + 6 * NSA_GROUPS * NSA_HEAD_DIM + 3 * NSA_HEADS
    mla_in = MLA_Q_RANK + MLA_KV_RANK + MLA_ROPE_DIM

    def nrm(i, shape, scale):
        return jax.random.normal(keys[i], shape, jnp.float32) * scale

    def gain(i, shape):
        return 1.0 + nrm(i, shape, 0.02)

    return {
        'x': nrm(0, (BATCH, SEQ, D), 1.0),
        'attn_norm_w': gain(1, (DEPTH, D)),
        'ffn_norm_w': gain(2, (DEPTH, D)),
        'final_norm_w': gain(3, (D,)),
        'da_w_in': nrm(4, (n_da, D, DA_HEADS * (4 * DA_QK_DIM + DA_V_DIM)), D ** -0.5),
        'da_lambda': nrm(5, (n_da, 4, DA_QK_DIM), 0.1),
        'da_subln_w': gain(6, (n_da, DA_V_DIM)),
        'da_w_out': nrm(7, (n_da, DA_HEADS * DA_V_DIM, D), (DA_HEADS * DA_V_DIM) ** -0.5),
        'nsa_w_in': nrm(8, (n_nsa, D, nsa_in), D ** -0.5),
        'nsa_cmp_pe': nrm(9, (n_nsa, 2, NSA_CMP_BLOCK, NSA_HEAD_DIM), 0.02),
        'nsa_cmp_w': nrm(10, (n_nsa, 2, NSA_CMP_BLOCK, NSA_HEAD_DIM, NSA_HEAD_DIM),
                         (NSA_CMP_BLOCK * NSA_HEAD_DIM) ** -0.5),
        'nsa_w_out': nrm(11, (n_nsa, NSA_HEADS * NSA_HEAD_DIM, D), (NSA_HEADS * NSA_HEAD_DIM) ** -0.5),
        'mla_w_in': nrm(12, (n_mla, D, mla_in), D ** -0.5),
        'mla_q_norm_w': gain(13, (n_mla, MLA_Q_RANK)),
        'mla_w_uq': nrm(14, (n_mla, MLA_Q_RANK, MLA_HEADS * (MLA_NOPE_DIM + MLA_ROPE_DIM)), MLA_Q_RANK ** -0.5),
        'mla_kv_norm_w': gain(15, (n_mla, MLA_KV_RANK)),
        'mla_w_ukv': nrm(16, (n_mla, MLA_KV_RANK, MLA_HEADS, MLA_NOPE_DIM + MLA_V_DIM), MLA_KV_RANK ** -0.5),
        'mla_w_out': nrm(17, (n_mla, MLA_HEADS * MLA_V_DIM, D), (MLA_HEADS * MLA_V_DIM) ** -0.5),
        'sb_w_in': nrm(18, (n_sb, D, 3 * SB_HEADS * SB_HEAD_DIM), D ** -0.5),
        'sb_w_out': nrm(19, (n_sb, SB_HEADS * SB_HEAD_DIM, D), (SB_HEADS * SB_HEAD_DIM) ** -0.5),
        'ffn_w_gu': nrm(20, (n_dense, D, 2 * FF_DIM), D ** -0.5),
        'ffn_w_down': nrm(21, (n_dense, FF_DIM, D), FF_DIM ** -0.5),
        'moe_w_router': nrm(22, (n_moe, D, N_EXPERTS), D ** -0.5),
        'moe_b_router': nrm(23, (n_moe, N_EXPERTS), 0.01),
        'moe_w_gu': nrm(24, (n_moe, N_EXPERTS, D, 2 * EXPERT_FF_DIM), D ** -0.5),
        'moe_w_down': nrm(25, (n_moe, N_EXPERTS, EXPERT_FF_DIM, D), EXPERT_FF_DIM ** -0.5),
    }


def reference(x, attn_norm_w, ffn_norm_w, final_norm_w,
              da_w_in, da_lambda, da_subln_w, da_w_out,
              nsa_w_in, nsa_cmp_pe, nsa_cmp_w, nsa_w_out,
              mla_w_in, mla_q_norm_w, mla_w_uq, mla_kv_norm_w, mla_w_ukv, mla_w_out,
              sb_w_in, sb_w_out,
              ffn_w_gu, ffn_w_down,
              moe_w_router, moe_b_router, moe_w_gu, moe_w_down):
    h = x
    for i in range(DEPTH):
        kind = i % N_MIXERS
        j = i // N_MIXERS
        a = _rms_norm(h, attn_norm_w[i])
        if kind == 0:
            lam_init = 0.8 - 0.6 * math.exp(-0.3 * i)
            mix = diff_attention(a, da_w_in[j], da_lambda[j], da_subln_w[j], da_w_out[j], lam_init)
        elif kind == 1:
            mix = nsa_attention(a, nsa_w_in[j], nsa_cmp_pe[j], nsa_cmp_w[j], nsa_w_out[j])
        elif kind == 2:
            mix = mla_attention(a, mla_w_in[j], mla_q_norm_w[j], mla_w_uq[j], mla_kv_norm_w[j],
                                mla_w_ukv[j], mla_w_out[j])
        else:
            mix = stick_breaking_attention(a, sb_w_in[j], sb_w_out[j])
        h = h + mix
        f = _rms_norm(h, ffn_norm_w[i])
        if i % 2 == 0:
            h = h + swiglu(f, ffn_w_gu[i // 2], ffn_w_down[i // 2])
        else:
            h = h + moe_swiglu(f, moe_w_router[i // 2], moe_b_router[i // 2],
                               moe_w_gu[i // 2], moe_w_down[i // 2])
    return _rms_norm(h, final_norm_w)
```

```python
import functools
import math

import numpy as np
import jax
import jax.numpy as jnp
from jax import lax
from jax.experimental import pallas as pl
from jax.experimental.pallas import tpu as pltpu

ROPE_THETA = 10000.0
NORM_EPS = 1e-6

DA_QK_DIM = 64
DA_V_DIM = 2 * DA_QK_DIM

NSA_HEADS = 16
NSA_GROUPS = 4
NSA_CMP_BLOCK = 32
NSA_CMP_STRIDE = 16
NSA_SLC_BLOCK = 64
NSA_SLC_TOPK = 16
NSA_WINDOW = 512

MLA_HEADS = 16
MLA_Q_RANK = 512
MLA_KV_RANK = 512
MLA_NOPE_DIM = 128
MLA_ROPE_DIM = 64
MLA_V_DIM = 128

SB_HEADS = 16

N_EXPERTS = 8
TOP_K = 2
MOE_ROW_BLOCK = 512

LANES = 128
HALF_LANES = LANES // 2
V7X_VMEM_BYTES = 64 * 1024 * 1024
VMEM_LIMIT = V7X_VMEM_BYTES * 7 // 8
NEG = -1e30

F32 = jnp.float32
BF16 = jnp.bfloat16


def _cparams(sem):
    return pltpu.CompilerParams(dimension_semantics=sem, vmem_limit_bytes=VMEM_LIMIT)


def _dot(a, b):
    return jnp.dot(a, b, preferred_element_type=F32)


def _dot_nt(a, b):
    return lax.dot_general(a, b, (((1,), (1,)), ((), ())), preferred_element_type=F32)


def _split_bf16(x):
    hi = x.astype(BF16)
    lo = (x - hi.astype(F32)).astype(BF16)
    return hi, lo


def _linear_kernel(*refs, has_norm, prologue, n_w, has_res, rope, act):
    it = iter(refs)
    x_ref = next(it)
    nw_ref = next(it) if has_norm else None
    w_refs = [next(it) for _ in range(n_w)]
    cos_ref = next(it) if rope else None
    sin_ref = next(it) if rope else None
    res_ref = next(it) if has_res else None
    o_ref = next(it)
    xs_ref = next(it) if prologue else None
    j = pl.program_id(1)

    if prologue:
        @pl.when(j == 0)
        def _():
            xf = x_ref[...].astype(F32)
            if has_norm:
                ms = jnp.mean(xf * xf, axis=-1, keepdims=True)
                xf = xf * lax.rsqrt(ms + NORM_EPS) * nw_ref[...]
            xs_ref[...] = xf.astype(BF16)
        xb = xs_ref[...]
    else:
        xb = x_ref[...]

    y = _dot(xb, w_refs[0][...])
    if act == "swiglu":
        u = _dot(xb, w_refs[1][...])
        y = y * jax.nn.sigmoid(y) * u
    elif act == "sigmoid":
        y = jax.nn.sigmoid(y)
    if has_res:
        y = y + res_ref[...]

    if rope:
        lo, hi = rope

        @pl.when((j >= lo) & (j < hi))
        def _():
            cos = cos_ref[...]
            sin = sin_ref[...]
            for c in range(y.shape[1] // LANES):
                yc = y[:, c * LANES:(c + 1) * LANES]
                yc = yc * cos + pltpu.roll(yc, HALF_LANES, 1) * sin
                o_ref[:, c * LANES:(c + 1) * LANES] = yc.astype(o_ref.dtype)

        @pl.when((j < lo) | (j >= hi))
        def _():
            o_ref[...] = y.astype(o_ref.dtype)
    else:
        o_ref[...] = y.astype(o_ref.dtype)


def _linear(x, w, *, n_out, k=None, x_col_block=0, w_tile_offsets=(0,), norm_w=None, residual=None,
            rope=None, act=None, out_dtype=None, tm=512, tn=512):
    m = x.shape[0]
    k = x.shape[1] if k is None else k
    tm = min(tm, m)
    tn = min(tn, n_out)
    assert m % tm == 0 and n_out % tn == 0 and w.shape[0] == k
    out_dtype = BF16 if out_dtype is None else out_dtype
    has_norm = norm_w is not None
    prologue = has_norm or x.dtype != BF16
    grid = (m // tm, n_out // tn)

    in_specs = [pl.BlockSpec((tm, k), lambda i, j: (i, x_col_block))]
    args = [x]
    if has_norm:
        in_specs.append(pl.BlockSpec((1, k), lambda i, j: (0, 0)))
        args.append(norm_w.reshape(1, k).astype(F32))
    for off in w_tile_offsets:
        in_specs.append(pl.BlockSpec((k, tn), lambda i, j, off=off: (0, j + off)))
        args.append(w)
    rope_range = None
    if rope is not None:
        cos, sin, lo, hi, seq = rope
        assert seq % tm == 0
        nrep = seq // tm
        for tab in (cos, sin):
            in_specs.append(pl.BlockSpec((tm, LANES), lambda i, j: (i % nrep, 0)))
            args.append(tab)
        rope_range = (lo, hi)
    if residual is not None:
        in_specs.append(pl.BlockSpec((tm, tn), lambda i, j: (i, j)))
        args.append(residual)
    scratch = [pltpu.VMEM((tm, k), BF16)] if prologue else []

    kern = functools.partial(_linear_kernel, has_norm=has_norm, prologue=prologue,
                             n_w=len(w_tile_offsets), has_res=residual is not None,
                             rope=rope_range, act=act)
    return pl.pallas_call(
        kern,
        grid=grid,
        in_specs=in_specs,
        out_specs=pl.BlockSpec((tm, tn), lambda i, j: (i, j)),
        out_shape=jax.ShapeDtypeStruct((m, n_out), out_dtype),
        scratch_shapes=scratch,
        compiler_params=_cparams(("parallel", "arbitrary")),
    )(*args)


def _rmsnorm_kernel(x_ref, w_ref, o_ref):
    xf = x_ref[...]
    ms = jnp.mean(xf * xf, axis=-1, keepdims=True)
    o_ref[...] = xf * lax.rsqrt(ms + NORM_EPS) * w_ref[...]


def _rmsnorm(x, w, tm=512):
    m, d = x.shape
    tm = min(tm, m)
    return pl.pallas_call(
        _rmsnorm_kernel,
        grid=(m // tm,),
        in_specs=[pl.BlockSpec((tm, d), lambda i: (i, 0)), pl.BlockSpec((1, d), lambda i: (0, 0))],
        out_specs=pl.BlockSpec((tm, d), lambda i: (i, 0)),
        out_shape=jax.ShapeDtypeStruct((m, d), F32),
        compiler_params=_cparams(("parallel",)),
    )(x, w.reshape(1, d))


def _rope_tables(pos, half):
    inv_freq = ROPE_THETA ** (-np.arange(half, dtype=np.float64) / half)
    ang = np.asarray(pos, np.float64)[:, None] * inv_freq[None, :]
    reps = LANES // half
    cos = np.tile(np.cos(ang), (1, reps))
    sin = np.tile(np.sin(ang), (1, reps))
    sign = np.where(np.arange(LANES) < HALF_LANES, -1.0, 1.0)
    return jnp.asarray(cos, F32), jnp.asarray(sin * sign, F32)


def _tri_tables(nq, reverse=False):
    qi, ki = [], []
    for q in range(nq):
        for kk in (range(q, -1, -1) if reverse else range(q + 1)):
            qi.append(q)
            ki.append(kk)
    return jnp.asarray(qi, jnp.int32), jnp.asarray(ki, jnp.int32)


def _softmax_update(s, mask, v, m_ref, l_ref, acc_ref):
    if mask is not None:
        s = jnp.where(mask, s, NEG)
    m_prev = m_ref[...]
    m_new = jnp.maximum(m_prev, jnp.max(s, axis=-1, keepdims=True))
    alpha = jnp.exp(m_prev - m_new)
    p = jnp.exp(s - m_new)
    if mask is not None:
        p = jnp.where(mask, p, 0.0)
    l_ref[...] = alpha * l_ref[...] + jnp.sum(p, axis=-1, keepdims=True)
    acc_ref[...] = alpha * acc_ref[...] + _dot(p.astype(BF16), v)
    m_ref[...] = m_new


def _causal_mask(qi, ki, tq, tk, strict=False):
    qpos = qi * tq + lax.broadcasted_iota(jnp.int32, (tq, tk), 0)
    kpos = ki * tk + lax.broadcasted_iota(jnp.int32, (tq, tk), 1)
    return (kpos < qpos) if strict else (kpos <= qpos)


def _da_kernel(qt_ref, kt_ref, q_ref, k_ref, v_ref, lam_ref, sub_ref, o_ref,
               m1, l1, a1, m2, l2, a2, *, scale, lam_init, t):
    step = pl.program_id(2)
    qi = qt_ref[step]
    ki = kt_ref[step]

    @pl.when(ki == 0)
    def _():
        for m_ref, l_ref, a_ref in ((m1, l1, a1), (m2, l2, a2)):
            m_ref[...] = jnp.full_like(m_ref, NEG)
            l_ref[...] = jnp.zeros_like(l_ref)
            a_ref[...] = jnp.zeros_like(a_ref)

    q = q_ref[...]
    lane = lax.broadcasted_iota(jnp.int32, (1, LANES), 1)
    first_map = (lane % HALF_LANES) < (HALF_LANES // 2)
    zero = jnp.zeros_like(q)
    q_maps = (jnp.where(first_map, q, zero), jnp.where(first_map, zero, q))
    k = k_ref[...]
    v = v_ref[...]

    def update(mask):
        for qm, m_ref, l_ref, a_ref in ((q_maps[0], m1, l1, a1), (q_maps[1], m2, l2, a2)):
            s = _dot_nt(qm, k) * scale
            _softmax_update(s, mask, v, m_ref, l_ref, a_ref)

    @pl.when(ki < qi)
    def _():
        update(None)

    @pl.when(ki == qi)
    def _():
        update(_causal_mask(qi, ki, t, t))
        lam = lam_ref[...]
        lam_full = (jnp.exp(jnp.sum(lam[0:1] * lam[1:2], axis=-1, keepdims=True))
                    - jnp.exp(jnp.sum(lam[2:3] * lam[3:4], axis=-1, keepdims=True)) + lam_init)
        o = a1[...] / l1[...] - lam_full * (a2[...] / l2[...])
        ms = jnp.mean(o * o, axis=-1, keepdims=True)
        o = o * lax.rsqrt(ms + NORM_EPS) * sub_ref[...]
        o_ref[...] = (o * (1.0 - lam_init)).astype(o_ref.dtype)


def _da_column_order(heads):
    dk = DA_QK_DIM
    hd = dk // 2
    per = 4 * dk + DA_V_DIM
    q_idx, k_idx, v_idx = [], [], []
    for h in range(heads):
        base = h * per
        for out, off in ((q_idx, 0), (k_idx, 2 * dk)):
            a = base + off + np.arange(dk)
            b = base + off + dk + np.arange(dk)
            out += [a[:hd], b[:hd], a[hd:], b[hd:]]
        v_idx.append(base + 4 * dk + np.arange(DA_V_DIM))
    return np.concatenate(q_idx + k_idx + v_idx)


def _diff_attention(hres, norm_w, w_in, lam, subln_w, w_out, lam_init, batch, seq):
    m, d = hres.shape
    heads = w_in.shape[1] // (4 * DA_QK_DIM + DA_V_DIM)
    hw = heads * LANES
    w_perm = w_in[:, _da_column_order(heads)].astype(BF16)
    cos, sin = _rope_tables(np.arange(seq), DA_QK_DIM // 2)
    tn = 512
    proj = _linear(hres, w_perm, n_out=3 * hw, norm_w=norm_w, tn=tn,
                   rope=(cos, sin, 0, 2 * hw // tn, seq))

    t = min(512, seq)
    nq = seq // t
    qt, kt = _tri_tables(nq)
    kern = functools.partial(_da_kernel, scale=DA_QK_DIM ** -0.5, lam_init=lam_init, t=t)
    attn = pl.pallas_call(
        kern,
        grid_spec=pltpu.PrefetchScalarGridSpec(
            num_scalar_prefetch=2,
            grid=(batch, heads, qt.shape[0]),
            in_specs=[
                pl.BlockSpec((t, LANES), lambda b, h, s, qt, kt: (b * nq + qt[s], h)),
                pl.BlockSpec((t, LANES), lambda b, h, s, qt, kt: (b * nq + kt[s], heads + h)),
                pl.BlockSpec((t, LANES), lambda b, h, s, qt, kt: (b * nq + kt[s], 2 * heads + h)),
                pl.BlockSpec(lam.shape, lambda b, h, s, qt, kt: (0, 0)),
                pl.BlockSpec((1, DA_V_DIM), lambda b, h, s, qt, kt: (0, 0)),
            ],
            out_specs=pl.BlockSpec((t, LANES), lambda b, h, s, qt, kt: (b * nq + qt[s], h)),
            scratch_shapes=[pltpu.VMEM((t, 1), F32), pltpu.VMEM((t, 1), F32), pltpu.VMEM((t, LANES), F32),
                            pltpu.VMEM((t, 1), F32), pltpu.VMEM((t, 1), F32), pltpu.VMEM((t, LANES), F32)],
        ),
        out_shape=jax.ShapeDtypeStruct((m, hw), BF16),
        compiler_params=_cparams(("parallel", "parallel", "arbitrary")),
    )(qt, kt, proj, proj, proj, lam.astype(F32), subln_w.reshape(1, DA_V_DIM).astype(F32))
    return _linear(attn, w_out.astype(BF16), n_out=d, residual=hres, out_dtype=F32)


def _mla_kernel(qt_ref, kt_ref, qn_ref, qr_ref, kn_ref, kr_ref, v_ref, o_ref, m_sc, l_sc, acc, *, scale, t):
    step = pl.program_id(2)
    qi = qt_ref[step]
    ki = kt_ref[step]

    @pl.when(ki == 0)
    def _():
        m_sc[...] = jnp.full_like(m_sc, NEG)
        l_sc[...] = jnp.zeros_like(l_sc)
        acc[...] = jnp.zeros_like(acc)

    q = jnp.concatenate([qn_ref[...], qr_ref[...]], axis=1)
    k = jnp.concatenate([kn_ref[...], kr_ref[...].astype(BF16)], axis=1)
    v = v_ref[...]

    def update(mask):
        _softmax_update(_dot_nt(q, k) * scale, mask, v, m_sc, l_sc, acc)

    @pl.when(ki < qi)
    def _():
        update(None)

    @pl.when(ki == qi)
    def _():
        update(_causal_mask(qi, ki, t, t))
        o_ref[...] = (acc[...] / l_sc[...]).astype(o_ref.dtype)


def _mla_q_columns(heads):
    per = MLA_NOPE_DIM + MLA_ROPE_DIM
    hd = MLA_ROPE_DIM // 2
    idx, keep = [], []
    for h in range(heads):
        idx.append(h * per + np.arange(MLA_NOPE_DIM))
        keep.append(np.ones(MLA_NOPE_DIM))
    for h in range(heads):
        s = h % 2
        g_idx = np.zeros(LANES, np.int64)
        g_keep = np.zeros(LANES)
        r0 = h * per + MLA_NOPE_DIM
        g_idx[s * hd:(s + 1) * hd] = r0 + np.arange(hd)
        g_idx[HALF_LANES + s * hd:HALF_LANES + (s + 1) * hd] = r0 + hd + np.arange(hd)
        g_keep[s * hd:(s + 1) * hd] = 1.0
        g_keep[HALF_LANES + s * hd:HALF_LANES + (s + 1) * hd] = 1.0
        idx.append(g_idx)
        keep.append(g_keep)
    return np.concatenate(idx), np.concatenate(keep)


def _mla_attention(hres, norm_w, w_in, q_norm_w, w_uq, kv_norm_w, w_ukv, w_out, batch, seq):
    m, d = hres.shape
    heads = MLA_HEADS
    hw = heads * LANES
    hd = MLA_ROPE_DIM // 2
    assert MLA_Q_RANK == MLA_KV_RANK and MLA_NOPE_DIM == LANES and MLA_V_DIM == LANES
    rank = MLA_Q_RANK
    r0 = 2 * rank
    kr_cols = np.concatenate([r0 + np.arange(hd), r0 + np.arange(hd),
                              r0 + hd + np.arange(hd), r0 + hd + np.arange(hd)])
    w_in_x = jnp.concatenate([w_in[:, :2 * rank], w_in[:, kr_cols]], axis=1).astype(BF16)
    cos, sin = _rope_tables(np.arange(seq), hd)
    n_in = 2 * rank + LANES
    proj = _linear(hres, w_in_x, n_out=n_in, norm_w=norm_w, tn=LANES, out_dtype=F32,
                   rope=(cos, sin, 2 * rank // LANES, n_in // LANES, seq))

    q_idx, q_keep = _mla_q_columns(heads)
    w_uq_x = (w_uq[:, q_idx] * jnp.asarray(q_keep, F32)[None, :]).astype(BF16)
    tn = 512
    qcat = _linear(proj, w_uq_x, n_out=2 * hw, k=rank, x_col_block=0, norm_w=q_norm_w, tn=tn,
                   rope=(cos, sin, hw // tn, 2 * hw // tn, seq))
    w_ukv_x = jnp.concatenate([w_ukv[:, :, :MLA_NOPE_DIM].reshape(rank, hw),
                               w_ukv[:, :, MLA_NOPE_DIM:].reshape(rank, hw)], axis=1).astype(BF16)
    kv = _linear(proj, w_ukv_x, n_out=2 * hw, k=rank, x_col_block=1, norm_w=kv_norm_w, tn=tn)

    t = min(512, seq)
    nq = seq // t
    qt, kt = _tri_tables(nq)
    kr_block = 2 * rank // LANES
    kern = functools.partial(_mla_kernel, scale=(MLA_NOPE_DIM + MLA_ROPE_DIM) ** -0.5, t=t)
    attn = pl.pallas_call(
        kern,
        grid_spec=pltpu.PrefetchScalarGridSpec(
            num_scalar_prefetch=2,
            grid=(batch, heads, qt.shape[0]),
            in_specs=[
                pl.BlockSpec((t, LANES), lambda b, h, s, qt, kt: (b * nq + qt[s], h)),
                pl.BlockSpec((t, LANES), lambda b, h, s, qt, kt: (b * nq + qt[s], heads + h)),
                pl.BlockSpec((t, LANES), lambda b, h, s, qt, kt: (b * nq + kt[s], h)),
                pl.BlockSpec((t, LANES), lambda b, h, s, qt, kt: (b * nq + kt[s], kr_block)),
                pl.BlockSpec((t, LANES), lambda b, h, s, qt, kt: (b * nq + kt[s], heads + h)),
            ],
            out_specs=pl.BlockSpec((t, LANES), lambda b, h, s, qt, kt: (b * nq + qt[s], h)),
            scratch_shapes=[pltpu.VMEM((t, 1), F32), pltpu.VMEM((t, 1), F32), pltpu.VMEM((t, LANES), F32)],
        ),
        out_shape=jax.ShapeDtypeStruct((m, hw), BF16),
        compiler_params=_cparams(("parallel", "parallel", "arbitrary")),
    )(qt, kt, qcat, qcat, kv, proj, kv)
    return _linear(attn, w_out.astype(BF16), n_out=d, residual=hres, out_dtype=F32)


def _sb_kernel(qt_ref, kt_ref, q_ref, k_ref, v_ref, tri_ref, o_ref, carry, acc, *, scale, t):
    step = pl.program_id(2)
    qi = qt_ref[step]
    ki = kt_ref[step]

    @pl.when(ki == qi)
    def _():
        carry[...] = jnp.zeros_like(carry)
        acc[...] = jnp.zeros_like(acc)

    q = q_ref[...]
    tri = tri_ref[...]

    def update(masked):
        for c in range(t // LANES - 1, -1, -1):
            kc = k_ref[c * LANES:(c + 1) * LANES, :]
            vc = v_ref[c * LANES:(c + 1) * LANES, :]
            z = _dot_nt(q, kc) * scale
            log_beta = jnp.minimum(z, 0.0) - jnp.log1p(jnp.exp(-jnp.abs(z)))
            log_rest = log_beta - z
            if masked:
                qpos = lax.broadcasted_iota(jnp.int32, z.shape, 0)
                kpos = c * LANES + lax.broadcasted_iota(jnp.int32, z.shape, 1)
                mask = kpos < qpos
                log_rest = jnp.where(mask, log_rest, 0.0)
            hi, lo = _split_bf16(log_rest)
            tail = _dot(hi, tri) + _dot(lo, tri) + carry[...]
            a = jnp.exp(log_beta + tail)
            if masked:
                a = jnp.where(mask, a, 0.0)
            acc[...] += _dot(a.astype(BF16), vc)
            carry[...] += jnp.sum(log_rest, axis=-1, keepdims=True)

    @pl.when(ki == qi)
    def _():
        update(True)

    @pl.when(ki < qi)
    def _():
        update(False)

    @pl.when(ki == 0)
    def _():
        o_ref[...] = acc[...].astype(o_ref.dtype)


def _sb_attention(hres, norm_w, w_in, w_out, batch, seq):
    m, d = hres.shape
    heads = SB_HEADS
    hw = heads * LANES
    assert w_in.shape[1] == 3 * hw
    qkv = _linear(hres, w_in.astype(BF16), n_out=3 * hw, norm_w=norm_w)
    t = min(512, seq)
    nq = seq // t
    qt, kt = _tri_tables(nq, reverse=True)
    tri = jnp.asarray(np.arange(LANES)[:, None] > np.arange(LANES)[None, :], BF16)
    kern = functools.partial(_sb_kernel, scale=LANES ** -0.5, t=t)
    attn = pl.pallas_call(
        kern,
        grid_spec=pltpu.PrefetchScalarGridSpec(
            num_scalar_prefetch=2,
            grid=(batch, heads, qt.shape[0]),
            in_specs=[
                pl.BlockSpec((t, LANES), lambda b, h, s, qt, kt: (b * nq + qt[s], h)),
                pl.BlockSpec((t, LANES), lambda b, h, s, qt, kt: (b * nq + kt[s], heads + h)),
                pl.BlockSpec((t, LANES), lambda b, h, s, qt, kt: (b * nq + kt[s], 2 * heads + h)),
                pl.BlockSpec((LANES, LANES), lambda b, h, s, qt, kt: (0, 0)),
            ],
            out_specs=pl.BlockSpec((t, LANES), lambda b, h, s, qt, kt: (b * nq + qt[s], h)),
            scratch_shapes=[pltpu.VMEM((t, 1), F32), pltpu.VMEM((t, LANES), F32)],
        ),
        out_shape=jax.ShapeDtypeStruct((m, hw), BF16),
        compiler_params=_cparams(("parallel", "parallel", "arbitrary")),
    )(qt, kt, qkv, qkv, qkv, tri)
    return _linear(attn, w_out.astype(BF16), n_out=d, residual=hres, out_dtype=F32)


def _nsa_compress_kernel(x_ref, w_ref, pe_ref, cos_ref, sin_ref, o_ref, *, rope):
    x = x_ref[0]
    half = w_ref.shape[0] // 2
    y_lo = _dot(x, w_ref[:half, :])
    y_hi = _dot(x, w_ref[half:, :])
    n = y_hi.shape[0]
    y = y_lo + pltpu.roll(y_hi, n - 1, 0) + _dot(pe_ref[...], w_ref[...])[0:1]
    if rope:
        y = y * cos_ref[...] + pltpu.roll(y, HALF_LANES, 1) * sin_ref[...]
    o_ref[0] = y.astype(o_ref.dtype)


def _nsa_compress(x_chunks, w, pe, cos, sin, rope):
    bg, n_chunk, kdim = x_chunks.shape
    dh = w.shape[1]
    return pl.pallas_call(
        functools.partial(_nsa_compress_kernel, rope=rope),
        grid=(bg,),
        in_specs=[pl.BlockSpec((1, n_chunk, kdim), lambda i: (i, 0, 0)),
                  pl.BlockSpec(w.shape, lambda i: (0, 0)),
                  pl.BlockSpec(pe.shape, lambda i: (0, 0)),
                  pl.BlockSpec(cos.shape, lambda i: (0, 0)),
                  pl.BlockSpec(sin.shape, lambda i: (0, 0))],
        out_specs=pl.BlockSpec((1, n_chunk, dh), lambda i: (i, 0, 0)),
        out_shape=jax.ShapeDtypeStruct((bg, n_chunk, dh), BF16),
        compiler_params=_cparams(("parallel",)),
    )(x_chunks, w, pe, cos, sin)


def _head_gate(gates, branch, head):
    lane = lax.broadcasted_iota(jnp.int32, gates.shape, 1)
    return jnp.sum(jnp.where(lane == branch * NSA_HEADS + head, gates, 0.0), axis=-1, keepdims=True)


def _nsa_cmp_kernel(q_ref, kc_ref, vc_ref, ovt_ref, g_ref, o_ref, sel_ref, *, scale, tq, rep, n_sel):
    qi = pl.program_id(2)
    grp = pl.program_id(1)
    qs = jnp.concatenate([q_ref[:, r * LANES:(r + 1) * LANES] for r in range(rep)], axis=0)
    kc = kc_ref[0]
    vc = vc_ref[0]
    n_cmp = kc.shape[0]
    rows = rep * tq

    s = _dot_nt(qs, kc) * scale
    qpos = qi * tq + lax.broadcasted_iota(jnp.int32, (rep, tq, n_cmp), 1).reshape(rows, n_cmp)
    cend = lax.broadcasted_iota(jnp.int32, (rows, n_cmp), 1) * NSA_CMP_STRIDE + (NSA_CMP_BLOCK - 1)
    mask = cend <= qpos
    s = jnp.where(mask, s, NEG)
    p = jnp.where(mask, jnp.exp(s - jnp.max(s, axis=-1, keepdims=True)), 0.0)
    den = jnp.sum(p, axis=-1, keepdims=True)
    p = p / jnp.where(den > 0.0, den, 1.0)
    o = _dot(p.astype(BF16), vc)
    gates = g_ref[...]
    for r in range(rep):
        gate = _head_gate(gates, 0, grp * rep + r)
        o_ref[:, r * LANES:(r + 1) * LANES] = gate * o[r * tq:(r + 1) * tq, :]

    st = _dot_nt(kc, qs) * scale
    qpos_t = qi * tq + lax.broadcasted_iota(jnp.int32, (n_cmp, rep, tq), 2).reshape(n_cmp, rows)
    cend_t = lax.broadcasted_iota(jnp.int32, (n_cmp, rows), 0) * NSA_CMP_STRIDE + (NSA_CMP_BLOCK - 1)
    mask_t = cend_t <= qpos_t
    st = jnp.where(mask_t, st, NEG)
    pt = jnp.where(mask_t, jnp.exp(st - jnp.max(st, axis=0, keepdims=True)), 0.0)
    den_t = jnp.sum(pt, axis=0, keepdims=True)
    pt = pt / jnp.where(den_t > 0.0, den_t, 1.0)
    psum = pt[:, 0:tq]
    for r in range(1, rep):
        psum = psum + pt[:, r * tq:(r + 1) * tq]
    hi, lo = _split_bf16(psum)
    ovt = ovt_ref[...]
    imp = _dot(ovt, hi) + _dot(ovt, lo)

    n_pad = imp.shape[0]
    blk = lax.broadcasted_iota(jnp.int32, (n_pad, tq), 0)
    cur = (qi * tq + lax.broadcasted_iota(jnp.int32, (n_pad, tq), 1)) // NSA_SLC_BLOCK
    forced = (blk == 0) | (blk == cur) | (blk == cur - 1)
    val = jnp.where(forced, jnp.inf, jnp.where(blk <= cur, imp, -jnp.inf))
    sel = jnp.zeros((n_pad, tq), F32)
    for _ in range(n_sel):
        top = jnp.max(val, axis=0, keepdims=True)
        first = jnp.min(jnp.where(val == top, blk, n_pad), axis=0, keepdims=True)
        pick = blk == first
        sel = jnp.where(pick, 1.0, sel)
        val = jnp.where(pick, -jnp.inf, val)
    sel_ref[...] = sel.T.astype(sel_ref.dtype)


def _nsa_win_kernel(q_ref, k_ref, v_ref, g_ref, prev_ref, o_ref, m_sc, l_sc, acc, *, scale, tq, rep, nwin, window):
    qi = pl.program_id(2)
    kk = pl.program_id(3)
    grp = pl.program_id(1)
    kb = qi - (nwin - 1) + kk
    rows = rep * tq

    @pl.when(kk == 0)
    def _():
        m_sc[...] = jnp.full_like(m_sc, NEG)
        l_sc[...] = jnp.zeros_like(l_sc)
        acc[...] = jnp.zeros_like(acc)

    @pl.when(kb >= 0)
    def _():
        qs = jnp.concatenate([q_ref[:, r * LANES:(r + 1) * LANES] for r in range(rep)], axis=0)
        s = _dot_nt(qs, k_ref[...]) * scale
        qpos = qi * tq + lax.broadcasted_iota(jnp.int32, (rep, tq, tq), 1).reshape(rows, tq)
        kpos = kb * tq + lax.broadcasted_iota(jnp.int32, (rows, tq), 1)
        mask = (kpos <= qpos) & (kpos > qpos - window)
        s = jnp.where(mask, s, NEG)
        m_prev = m_sc[...]
        m_new = jnp.maximum(m_prev, jnp.max(s, axis=-1, keepdims=True))
        alpha = jnp.exp(m_prev - m_new)
        p = jnp.where(mask, jnp.exp(s - m_new), 0.0)
        l_sc[...] = alpha * l_sc[...] + jnp.sum(p, axis=-1, keepdims=True)
        acc[...] = alpha * acc[...] + _dot(p.astype(BF16), v_ref[...])
        m_sc[...] = m_new

    @pl.when(kk == nwin - 1)
    def _():
        o = acc[...] / l_sc[...]
        gates = g_ref[...]
        for r in range(rep):
            gate = _head_gate(gates, 2, grp * rep + r)
            o_ref[:, r * LANES:(r + 1) * LANES] = (prev_ref[:, r * LANES:(r + 1) * LANES]
                                                   + gate * o[r * tq:(r + 1) * tq, :])


def _nsa_slc_kernel(qt_ref, kt_ref, q_ref, k_ref, v_ref, sel_ref, g_ref, prev_ref, o_ref,
                    m_sc, l_sc, acc, *, scale, t, rep):
    step = pl.program_id(2)
    grp = pl.program_id(1)
    qi = qt_ref[step]
    ki = kt_ref[step]
    rows = rep * t

    @pl.when(ki == 0)
    def _():
        m_sc[...] = jnp.full_like(m_sc, NEG)
        l_sc[...] = jnp.zeros_like(l_sc)
        acc[...] = jnp.zeros_like(acc)

    qs = jnp.concatenate([q_ref[:, r * LANES:(r + 1) * LANES] for r in range(rep)], axis=0)
    s = (_dot_nt(qs, k_ref[...]) * scale).reshape(rep, t, t)
    n_pad = sel_ref.shape[1]
    blk = lax.broadcasted_iota(jnp.int32, (n_pad, t), 0)
    kblk = (ki * t + lax.broadcasted_iota(jnp.int32, (n_pad, t), 1)) // NSA_SLC_BLOCK
    expand = jnp.where(blk == kblk, 1.0, 0.0).astype(BF16)
    chosen = _dot(sel_ref[...], expand) > 0.5
    mask = (chosen & _causal_mask(qi, ki, t, t))[None]
    s = jnp.where(mask, s, NEG)
    m_prev = m_sc[...]
    m_new = jnp.maximum(m_prev, jnp.max(s, axis=-1, keepdims=True))
    alpha = jnp.exp(m_prev - m_new)
    p = jnp.where(mask, jnp.exp(s - m_new), 0.0)
    l_sc[...] = alpha * l_sc[...] + jnp.sum(p, axis=-1, keepdims=True)
    pv = _dot(p.reshape(rows, t).astype(BF16), v_ref[...]).reshape(rep, t, LANES)
    acc[...] = alpha * acc[...] + pv
    m_sc[...] = m_new

    @pl.when(ki == qi)
    def _():
        l = l_sc[...]
        o = acc[...] / jnp.where(l > 0.0, l, 1.0)
        gates = g_ref[...]
        for r in range(rep):
            gate = _head_gate(gates, 1, grp * rep + r)
            o_ref[:, r * LANES:(r + 1) * LANES] = (prev_ref[:, r * LANES:(r + 1) * LANES]
                                                   + gate * o[r]).astype(o_ref.dtype)


def _nsa_attention(hres, norm_w, w_in, cmp_pe, cmp_w, w_out, batch, seq):
    m, d = hres.shape
    heads, groups = NSA_HEADS, NSA_GROUPS
    rep = heads // groups
    dh = d // heads
    assert dh == LANES
    kvw = groups * dh
    n_main = heads * dh + 6 * kvw
    scale = dh ** -0.5
    gw = rep * LANES

    cos, sin = _rope_tables(np.arange(seq), dh // 2)
    proj = _linear(hres, w_in[:, :n_main].astype(BF16), n_out=n_main, norm_w=norm_w,
                   rope=(cos, sin, 0, heads * dh // 512, seq))
    w_gate = jnp.pad(w_in[:, n_main:], ((0, 0), (0, LANES - 3 * heads))).astype(BF16)
    gates = _linear(hres, w_gate, n_out=LANES, norm_w=norm_w, act="sigmoid", out_dtype=F32)

    n_chunk = seq // NSA_CMP_STRIDE
    assert NSA_CMP_BLOCK == 2 * NSA_CMP_STRIDE
    cmp_end = np.arange(n_chunk) * NSA_CMP_STRIDE + NSA_CMP_BLOCK - 1
    ccos, csin = _rope_tables(cmp_end, dh // 2)

    def chunks(i):
        lo = heads * dh + i * kvw
        x = proj[:, lo:lo + kvw].reshape(batch, n_chunk, NSA_CMP_STRIDE, groups, dh)
        return x.transpose(0, 3, 1, 2, 4).reshape(batch * groups, n_chunk, NSA_CMP_STRIDE * dh)

    def cmp_weights(i):
        return (cmp_w[i].reshape(NSA_CMP_BLOCK * dh, dh).astype(BF16),
                jnp.broadcast_to(cmp_pe[i].reshape(1, NSA_CMP_BLOCK * dh), (8, NSA_CMP_BLOCK * dh)).astype(BF16))

    k_cmp = _nsa_compress(chunks(0), *cmp_weights(0), ccos, csin, True)
    v_cmp = _nsa_compress(chunks(1), *cmp_weights(1), ccos, csin, False)

    n_slc = seq // NSA_SLC_BLOCK
    n_sel = min(NSA_SLC_TOPK, n_slc)
    n_pad = max(LANES, n_slc)
    n_cmp = n_chunk - NSA_CMP_BLOCK // NSA_CMP_STRIDE + 1
    cs = np.arange(n_chunk)[None, :] * NSA_CMP_STRIDE
    js = np.arange(n_pad)[:, None] * NSA_SLC_BLOCK
    ovt = ((cs < js + NSA_SLC_BLOCK) & (cs + NSA_CMP_BLOCK > js)
           & (np.arange(n_chunk)[None, :] < n_cmp) & (np.arange(n_pad)[:, None] < n_slc))
    ovt = jnp.asarray(ovt, BF16)

    tq = min(256, seq)
    nq = seq // tq
    o_c, sel = pl.pallas_call(
        functools.partial(_nsa_cmp_kernel, scale=scale, tq=tq, rep=rep, n_sel=n_sel),
        grid=(batch, groups, nq),
        in_specs=[
            pl.BlockSpec((tq, gw), lambda b, g, i: (b * nq + i, g)),
            pl.BlockSpec((1, n_chunk, dh), lambda b, g, i: (b * groups + g, 0, 0)),
            pl.BlockSpec((1, n_chunk, dh), lambda b, g, i: (b * groups + g, 0, 0)),
            pl.BlockSpec(ovt.shape, lambda b, g, i: (0, 0)),
            pl.BlockSpec((tq, LANES), lambda b, g, i: (b * nq + i, 0)),
        ],
        out_specs=[
            pl.BlockSpec((tq, gw), lambda b, g, i: (b * nq + i, g)),
            pl.BlockSpec((tq, n_pad), lambda b, g, i: ((b * groups + g) * nq + i, 0)),
        ],
        out_shape=[jax.ShapeDtypeStruct((m, heads * dh), F32),
                   jax.ShapeDtypeStruct((batch * groups * seq, n_pad), BF16)],
        compiler_params=_cparams(("parallel", "parallel", "arbitrary")),
    )(proj, k_cmp, v_cmp, ovt, gates)

    assert NSA_WINDOW % tq == 0 or seq < NSA_WINDOW
    nwin = min(NSA_WINDOW // tq + 1, nq)
    kv_blk = lambda i: (heads * dh + i * kvw) // LANES
    o_cw = pl.pallas_call(
        functools.partial(_nsa_win_kernel, scale=scale, tq=tq, rep=rep, nwin=nwin, window=NSA_WINDOW),
        grid=(batch, groups, nq, nwin),
        in_specs=[
            pl.BlockSpec((tq, gw), lambda b, g, i, kk: (b * nq + i, g)),
            pl.BlockSpec((tq, LANES), lambda b, g, i, kk: (b * nq + jnp.maximum(i - (nwin - 1) + kk, 0), kv_blk(4) + g)),
            pl.BlockSpec((tq, LANES), lambda b, g, i, kk: (b * nq + jnp.maximum(i - (nwin - 1) + kk, 0), kv_blk(5) + g)),
            pl.BlockSpec((tq, LANES), lambda b, g, i, kk: (b * nq + i, 0)),
            pl.BlockSpec((tq, gw), lambda b, g, i, kk: (b * nq + i, g)),
        ],
        out_specs=pl.BlockSpec((tq, gw), lambda b, g, i, kk: (b * nq + i, g)),
        out_shape=jax.ShapeDtypeStruct((m, heads * dh), F32),
        scratch_shapes=[pltpu.VMEM((rep * tq, 1), F32), pltpu.VMEM((rep * tq, 1), F32),
                        pltpu.VMEM((rep * tq, LANES), F32)],
        compiler_params=_cparams(("parallel", "parallel", "parallel", "arbitrary")),
    )(proj, proj, proj, gates, o_c)

    t = tq
    qt, kt = _tri_tables(nq)
    attn = pl.pallas_call(
        functools.partial(_nsa_slc_kernel, scale=scale, t=t, rep=rep),
        grid_spec=pltpu.PrefetchScalarGridSpec(
            num_scalar_prefetch=2,
            grid=(batch, groups, qt.shape[0]),
            in_specs=[
                pl.BlockSpec((t, gw), lambda b, g, s, qt, kt: (b * nq + qt[s], g)),
                pl.BlockSpec((t, LANES), lambda b, g, s, qt, kt: (b * nq + kt[s], kv_blk(2) + g)),
                pl.BlockSpec((t, LANES), lambda b, g, s, qt, kt: (b * nq + kt[s], kv_blk(3) + g)),
                pl.BlockSpec((t, n_pad), lambda b, g, s, qt, kt: ((b * groups + g) * nq + qt[s], 0)),
                pl.BlockSpec((t, LANES), lambda b, g, s, qt, kt: (b * nq + qt[s], 0)),
                pl.BlockSpec((t, gw), lambda b, g, s, qt, kt: (b * nq + qt[s], g)),
            ],
            out_specs=pl.BlockSpec((t, gw), lambda b, g, s, qt, kt: (b * nq + qt[s], g)),
            scratch_shapes=[pltpu.VMEM((rep, t, 1), F32), pltpu.VMEM((rep, t, 1), F32),
                            pltpu.VMEM((rep, t, LANES), F32)],
        ),
        out_shape=jax.ShapeDtypeStruct((m, heads * dh), BF16),
        compiler_params=_cparams(("parallel", "parallel", "arbitrary")),
    )(qt, kt, proj, proj, proj, sel, gates, o_cw)
    return _linear(attn, w_out.astype(BF16), n_out=d, residual=hres, out_dtype=F32)


def _swiglu(hres, norm_w, w_gu, w_down):
    d = hres.shape[1]
    ff = w_down.shape[0]
    tn = 512
    hidden = _linear(hres, w_gu.astype(BF16), n_out=ff, w_tile_offsets=(0, ff // tn), norm_w=norm_w,
                     act="swiglu", tn=tn)
    return _linear(hidden, w_down.astype(BF16), n_out=d, residual=hres, out_dtype=F32)


def _router_kernel(x_ref, nw_ref, w_ref, b_ref, f_ref, idx_ref, gate_ref, *, n_experts):
    xf = x_ref[...]
    ms = jnp.mean(xf * xf, axis=-1, keepdims=True)
    f = xf * lax.rsqrt(ms + NORM_EPS) * nw_ref[...]
    f_ref[...] = f
    fh, fl = _split_bf16(f)
    w = w_ref[...]
    wh, wl = _split_bf16(w)
    logits = _dot(fh, wh) + _dot(fl, wh) + _dot(fh, wl) + b_ref[...]
    lane = lax.broadcasted_iota(jnp.int32, logits.shape, 1)
    logits = jnp.where(lane < n_experts, logits, -jnp.inf)
    v1 = jnp.max(logits, axis=-1, keepdims=True)
    i1 = jnp.min(jnp.where(logits == v1, lane, LANES), axis=-1, keepdims=True)
    rest = jnp.where(lane == i1, -jnp.inf, logits)
    v2 = jnp.max(rest, axis=-1, keepdims=True)
    i2 = jnp.min(jnp.where(rest == v2, lane, LANES), axis=-1, keepdims=True)
    e2 = jnp.exp(v2 - v1)
    g1 = 1.0 / (1.0 + e2)
    g2 = e2 / (1.0 + e2)
    idx_ref[...] = jnp.where(lane == 0, i1, jnp.where(lane == 1, i2, 0))
    gate_ref[...] = jnp.where(lane == 0, g1, jnp.where(lane == 1, g2, 0.0))


def _gather_kernel(src_ref, nvalid_ref, x_hbm, o_ref, buf, sem):
    blk = pl.program_id(0)
    rows = buf.shape[0]
    nvalid = nvalid_ref[blk]

    @pl.when(blk == 0)
    def _():
        buf[...] = jnp.zeros_like(buf)

    def row_copy(r):
        return pltpu.make_async_copy(x_hbm.at[pl.ds(src_ref[blk * rows + r], 1), :],
                                     buf.at[pl.ds(r, 1), :], sem)

    def start(r, c):
        row_copy(r).start()
        return c

    def wait(r, c):
        row_copy(r).wait()
        return c

    lax.fori_loop(0, nvalid, start, 0)
    lax.fori_loop(0, nvalid, wait, 0)
    rid = lax.broadcasted_iota(jnp.int32, (rows, 1), 0)
    o_ref[...] = jnp.where(rid < nvalid, buf[...], 0.0).astype(o_ref.dtype)


def _expert_up_kernel(be_ref, nv_ref, x_ref, wg_ref, wu_ref, o_ref):
    blk = pl.program_id(1)

    @pl.when(nv_ref[blk] > 0)
    def _():
        x = x_ref[...]
        g = _dot(x, wg_ref[0])
        u = _dot(x, wu_ref[0])
        o_ref[...] = (g * jax.nn.sigmoid(g) * u).astype(o_ref.dtype)


def _expert_down_kernel(be_ref, nv_ref, h_ref, w_ref, o_ref):
    blk = pl.program_id(1)

    @pl.when(nv_ref[blk] > 0)
    def _():
        o_ref[...] = _dot(h_ref[...], w_ref[0])


def _combine_kernel(dst_ref, y_hbm, res_ref, gate_ref, o_ref, buf0, buf1, sem):
    i = pl.program_id(0)
    rows = buf0.shape[0]

    def row_copy(r, slot, buf):
        return pltpu.make_async_copy(y_hbm.at[pl.ds(dst_ref[(i * rows + r) * TOP_K + slot], 1), :],
                                     buf.at[pl.ds(r, 1), :], sem.at[slot])

    def start(r, c):
        row_copy(r, 0, buf0).start()
        row_copy(r, 1, buf1).start()
        return c

    def wait(r, c):
        row_copy(r, 0, buf0).wait()
        row_copy(r, 1, buf1).wait()
        return c

    lax.fori_loop(0, rows, start, 0)
    lax.fori_loop(0, rows, wait, 0)
    gate = gate_ref[...]
    o_ref[...] = res_ref[...] + (buf0[...] * gate[:, 0:1] + buf1[...] * gate[:, 1:2])


def _moe(hres, norm_w, w_router, b_router, w_gu, w_down):
    m, d = hres.shape
    n_exp = w_router.shape[1]
    ff = w_down.shape[1]
    rb = MOE_ROW_BLOCK
    assert TOP_K == 2 and m % rb == 0
    tm = min(512, m)

    w_r = jnp.pad(w_router, ((0, 0), (0, LANES - n_exp))).astype(F32)
    b_r = jnp.pad(b_router, (0, LANES - n_exp)).reshape(1, LANES).astype(F32)
    f, idx, gate = pl.pallas_call(
        functools.partial(_router_kernel, n_experts=n_exp),
        grid=(m // tm,),
        in_specs=[pl.BlockSpec((tm, d), lambda i: (i, 0)),
                  pl.BlockSpec((1, d), lambda i: (0, 0)),
                  pl.BlockSpec((d, LANES), lambda i: (0, 0)),
                  pl.BlockSpec((1, LANES), lambda i: (0, 0))],
        out_specs=[pl.BlockSpec((tm, d), lambda i: (i, 0)),
                   pl.BlockSpec((tm, LANES), lambda i: (i, 0)),
                   pl.BlockSpec((tm, LANES), lambda i: (i, 0))],
        out_shape=[jax.ShapeDtypeStruct((m, d), F32),
                   jax.ShapeDtypeStruct((m, LANES), jnp.int32),
                   jax.ShapeDtypeStruct((m, LANES), F32)],
        compiler_params=_cparams(("parallel",)),
    )(hres, norm_w.reshape(1, d).astype(F32), w_r, b_r)

    nk = m * TOP_K
    e_flat = idx[:, :TOP_K].reshape(nk)
    onehot = (e_flat[:, None] == jnp.arange(n_exp, dtype=jnp.int32)[None, :]).astype(jnp.int32)
    csum = jnp.cumsum(onehot, axis=0)
    rank = jnp.sum((csum - onehot) * onehot, axis=1)
    counts = csum[-1]
    padded = (counts + rb - 1) // rb * rb
    pend = jnp.cumsum(padded)
    pstart = pend - padded
    dest = (pstart[e_flat] + rank).astype(jnp.int32)
    n_rows = (-(-nk // rb)) * rb + n_exp * rb
    n_blocks = n_rows // rb
    src_tok = jnp.zeros((n_rows,), jnp.int32).at[dest].set(jnp.arange(nk, dtype=jnp.int32) // TOP_K)
    blk_start = jnp.arange(n_blocks, dtype=jnp.int32) * rb
    blk_expert = jnp.minimum(jnp.searchsorted(pend, blk_start, side="right"), n_exp - 1).astype(jnp.int32)
    nvalid = jnp.clip(pstart[blk_expert] + counts[blk_expert] - blk_start, 0, rb).astype(jnp.int32)

    xbuf = pl.pallas_call(
        _gather_kernel,
        grid_spec=pltpu.PrefetchScalarGridSpec(
            num_scalar_prefetch=2,
            grid=(n_blocks,),
            in_specs=[pl.BlockSpec(memory_space=pl.ANY)],
            out_specs=pl.BlockSpec((rb, d), lambda i, src, nv: (i, 0)),
            scratch_shapes=[pltpu.VMEM((rb, d), F32), pltpu.SemaphoreType.DMA(())],
        ),
        out_shape=jax.ShapeDtypeStruct((n_rows, d), BF16),
        compiler_params=_cparams(("arbitrary",)),
    )(src_tok, nvalid, f)

    tn = 512
    w_gu_b = w_gu.astype(BF16)
    hidden = pl.pallas_call(
        _expert_up_kernel,
        grid_spec=pltpu.PrefetchScalarGridSpec(
            num_scalar_prefetch=2,
            grid=(ff // tn, n_blocks),
            in_specs=[pl.BlockSpec((rb, d), lambda j, i, be, nv: (i, 0)),
                      pl.BlockSpec((1, d, tn), lambda j, i, be, nv: (be[i], 0, j)),
                      pl.BlockSpec((1, d, tn), lambda j, i, be, nv: (be[i], 0, j + ff // tn))],
            out_specs=pl.BlockSpec((rb, tn), lambda j, i, be, nv: (i, j)),
        ),
        out_shape=jax.ShapeDtypeStruct((n_rows, ff), BF16),
        compiler_params=_cparams(("arbitrary", "arbitrary")),
    )(blk_expert, nvalid, xbuf, w_gu_b, w_gu_b)

    tnd = min(512, d)
    ybuf = pl.pallas_call(
        _expert_down_kernel,
        grid_spec=pltpu.PrefetchScalarGridSpec(
            num_scalar_prefetch=2,
            grid=(d // tnd, n_blocks),
            in_specs=[pl.BlockSpec((rb, ff), lambda j, i, be, nv: (i, 0)),
                      pl.BlockSpec((1, ff, tnd), lambda j, i, be, nv: (be[i], 0, j))],
            out_specs=pl.BlockSpec((rb, tnd), lambda j, i, be, nv: (i, j)),
        ),
        out_shape=jax.ShapeDtypeStruct((n_rows, d), F32),
        compiler_params=_cparams(("arbitrary", "arbitrary")),
    )(blk_expert, nvalid, hidden, w_down.astype(BF16))

    return pl.pallas_call(
        _combine_kernel,
        grid_spec=pltpu.PrefetchScalarGridSpec(
            num_scalar_prefetch=1,
            grid=(m // tm,),
            in_specs=[pl.BlockSpec(memory_space=pl.ANY),
                      pl.BlockSpec((tm, d), lambda i, dst: (i, 0)),
                      pl.BlockSpec((tm, LANES), lambda i, dst: (i, 0))],
            out_specs=pl.BlockSpec((tm, d), lambda i, dst: (i, 0)),
            scratch_shapes=[pltpu.VMEM((tm, d), F32), pltpu.VMEM((tm, d), F32),
                            pltpu.SemaphoreType.DMA((TOP_K,))],
        ),
        out_shape=jax.ShapeDtypeStruct((m, d), F32),
        compiler_params=_cparams(("arbitrary",)),
    )(dest, ybuf, hres, gate)


def kernel(x, attn_norm_w, ffn_norm_w, final_norm_w, da_w_in, da_lambda, da_subln_w, da_w_out, nsa_w_in, nsa_cmp_pe, nsa_cmp_w, nsa_w_out, mla_w_in, mla_q_norm_w, mla_w_uq, mla_kv_norm_w, mla_w_ukv, mla_w_out, sb_w_in, sb_w_out, ffn_w_gu, ffn_w_down, moe_w_router, moe_b_router, moe_w_gu, moe_w_down):
    batch, seq, d = x.shape
    depth = attn_norm_w.shape[0]
    h = x.reshape(batch * seq, d)
    for i in range(depth):
        kind = i % 4
        j = i // 4
        if kind == 0:
            lam_init = 0.8 - 0.6 * math.exp(-0.3 * i)
            h = _diff_attention(h, attn_norm_w[i], da_w_in[j], da_lambda[j], da_subln_w[j], da_w_out[j],
                                lam_init, batch, seq)
        elif kind == 1:
            h = _nsa_attention(h, attn_norm_w[i], nsa_w_in[j], nsa_cmp_pe[j], nsa_cmp_w[j], nsa_w_out[j],
                               batch, seq)
        elif kind == 2:
            h = _mla_attention(h, attn_norm_w[i], mla_w_in[j], mla_q_norm_w[j], mla_w_uq[j],
                               mla_kv_norm_w[j], mla_w_ukv[j], mla_w_out[j], batch, seq)
        else:
            h = _sb_attention(h, attn_norm_w[i], sb_w_in[j], sb_w_out[j], batch, seq)
        if i % 2 == 0:
            h = _swiglu(h, ffn_norm_w[i], ffn_w_gu[i // 2], ffn_w_down[i // 2])
        else:
            h = _moe(h, ffn_norm_w[i], moe_w_router[i // 2], moe_b_router[i // 2],
                     moe_w_gu[i // 2], moe_w_down[i // 2])
    return _rmsnorm(h, final_norm_w.astype(F32)).reshape(batch, seq, d)
```

```python
import functools
import math

import numpy as np
import jax
import jax.numpy as jnp
from jax import lax
from jax.experimental import pallas as pl
from jax.experimental.pallas import tpu as pltpu

ROPE_THETA = 10000.0
NORM_EPS = 1e-6

DA_QK_DIM = 64
DA_V_DIM = 2 * DA_QK_DIM

NSA_HEADS = 16
NSA_GROUPS = 4
NSA_CMP_BLOCK = 32
NSA_CMP_STRIDE = 16
NSA_SLC_BLOCK = 64
NSA_SLC_TOPK = 16
NSA_WINDOW = 512

MLA_HEADS = 16
MLA_Q_RANK = 512
MLA_KV_RANK = 512
MLA_NOPE_DIM = 128
MLA_ROPE_DIM = 64
MLA_V_DIM = 128

SB_HEADS = 16

N_EXPERTS = 8
TOP_K = 2
MOE_ROW_BLOCK = 512

LANES = 128
HALF_LANES = LANES // 2
V7X_VMEM_BYTES = 64 * 1024 * 1024
VMEM_LIMIT = V7X_VMEM_BYTES * 7 // 8
NEG = -1e30
LOG2E = math.log2(math.e)

F32 = jnp.float32
BF16 = jnp.bfloat16


def _cparams(sem):
    return pltpu.CompilerParams(dimension_semantics=sem, vmem_limit_bytes=VMEM_LIMIT)


def _dot(a, b):
    return jnp.dot(a, b, preferred_element_type=F32)


def _dot_nt(a, b):
    return lax.dot_general(a, b, (((1,), (1,)), ((), ())), preferred_element_type=F32)


def _split_bf16(x):
    hi = x.astype(BF16)
    lo = (x - hi.astype(F32)).astype(BF16)
    return hi, lo


def _linear_kernel(*refs, has_norm, prologue, n_w, has_res, rope, act, transpose_out):
    it = iter(refs)
    x_ref = next(it)
    nw_ref = next(it) if has_norm else None
    w_refs = [next(it) for _ in range(n_w)]
    cos_ref = next(it) if rope else None
    sin_ref = next(it) if rope else None
    res_ref = next(it) if has_res else None
    o_ref = next(it)
    xs_ref = next(it) if prologue else None
    j = pl.program_id(1)

    if prologue:
        @pl.when(j == 0)
        def _():
            xf = x_ref[...].astype(F32)
            if has_norm:
                ms = jnp.mean(xf * xf, axis=-1, keepdims=True)
                xf = xf * lax.rsqrt(ms + NORM_EPS) * nw_ref[...]
            xs_ref[...] = xf.astype(BF16)
        xb = xs_ref[...]
    else:
        xb = x_ref[...]

    y = _dot(xb, w_refs[0][...])
    if act == "swiglu":
        u = _dot(xb, w_refs[1][...])
        y = y * jax.nn.sigmoid(y) * u
    elif act == "sigmoid":
        y = jax.nn.sigmoid(y)
    if has_res:
        y = y + res_ref[...]

    if rope:
        lo, hi = rope

        @pl.when((j >= lo) & (j < hi))
        def _():
            cos = cos_ref[...]
            sin = sin_ref[...]
            for c in range(y.shape[1] // LANES):
                yc = y[:, c * LANES:(c + 1) * LANES]
                yc = yc * cos + pltpu.roll(yc, HALF_LANES, 1) * sin
                o_ref[:, c * LANES:(c + 1) * LANES] = yc.astype(o_ref.dtype)

        @pl.when((j < lo) | (j >= hi))
        def _():
            o_ref[...] = y.astype(o_ref.dtype)
    elif transpose_out:
        o_ref[...] = y.T.astype(o_ref.dtype)
    else:
        o_ref[...] = y.astype(o_ref.dtype)


def _linear(x, w, *, n_out, k=None, x_col_block=0, w_tile_offsets=(0,), norm_w=None, residual=None,
            rope=None, act=None, out_dtype=None, tm=512, tn=512, transpose_out=False, name="linear"):
    m = x.shape[0]
    k = x.shape[1] if k is None else k
    tm = min(tm, m)
    tn = min(tn, n_out)
    assert m % tm == 0 and n_out % tn == 0 and w.shape[0] == k
    out_dtype = BF16 if out_dtype is None else out_dtype
    has_norm = norm_w is not None
    prologue = has_norm or x.dtype != BF16
    grid = (m // tm, n_out // tn)

    in_specs = [pl.BlockSpec((tm, k), lambda i, j: (i, x_col_block))]
    args = [x]
    if has_norm:
        in_specs.append(pl.BlockSpec((1, k), lambda i, j: (0, 0)))
        args.append(norm_w.reshape(1, k).astype(F32))
    for off in w_tile_offsets:
        in_specs.append(pl.BlockSpec((k, tn), lambda i, j, off=off: (0, j + off)))
        args.append(w)
    rope_range = None
    if rope is not None:
        cos, sin, lo, hi, seq = rope
        assert seq % tm == 0
        nrep = seq // tm
        for tab in (cos, sin):
            in_specs.append(pl.BlockSpec((tm, LANES), lambda i, j: (i % nrep, 0)))
            args.append(tab)
        rope_range = (lo, hi)
    if residual is not None:
        in_specs.append(pl.BlockSpec((tm, tn), lambda i, j: (i, j)))
        args.append(residual)
    scratch = [pltpu.VMEM((tm, k), BF16)] if prologue else []

    kern = functools.partial(_linear_kernel, has_norm=has_norm, prologue=prologue,
                             n_w=len(w_tile_offsets), has_res=residual is not None,
                             rope=rope_range, act=act, transpose_out=transpose_out)
    if transpose_out:
        assert rope is None
        out_specs = pl.BlockSpec((tn, tm), lambda i, j: (j, i))
        out_shape = jax.ShapeDtypeStruct((n_out, m), out_dtype)
    else:
        out_specs = pl.BlockSpec((tm, tn), lambda i, j: (i, j))
        out_shape = jax.ShapeDtypeStruct((m, n_out), out_dtype)
    return pl.pallas_call(
        kern,
        grid=grid,
        in_specs=in_specs,
        out_specs=out_specs,
        out_shape=out_shape,
        scratch_shapes=scratch,
        compiler_params=_cparams(("parallel", "arbitrary")),
        name=name,
    )(*args)


def _rmsnorm_kernel(x_ref, w_ref, o_ref):
    xf = x_ref[...]
    ms = jnp.mean(xf * xf, axis=-1, keepdims=True)
    o_ref[...] = xf * lax.rsqrt(ms + NORM_EPS) * w_ref[...]


def _rmsnorm(x, w, tm=512):
    m, d = x.shape
    tm = min(tm, m)
    return pl.pallas_call(
        _rmsnorm_kernel,
        grid=(m // tm,),
        in_specs=[pl.BlockSpec((tm, d), lambda i: (i, 0)), pl.BlockSpec((1, d), lambda i: (0, 0))],
        out_specs=pl.BlockSpec((tm, d), lambda i: (i, 0)),
        out_shape=jax.ShapeDtypeStruct((m, d), F32),
        compiler_params=_cparams(("parallel",)),
    )(x, w.reshape(1, d))


def _rope_tables(pos, half):
    inv_freq = ROPE_THETA ** (-np.arange(half, dtype=np.float64) / half)
    ang = np.asarray(pos, np.float64)[:, None] * inv_freq[None, :]
    reps = LANES // half
    cos = np.tile(np.cos(ang), (1, reps))
    sin = np.tile(np.sin(ang), (1, reps))
    sign = np.where(np.arange(LANES) < HALF_LANES, -1.0, 1.0)
    return jnp.asarray(cos, F32), jnp.asarray(sin * sign, F32)


def _tri_tables(nq, reverse=False):
    qi, ki = [], []
    for q in range(nq):
        for kk in (range(q, -1, -1) if reverse else range(q + 1)):
            qi.append(q)
            ki.append(kk)
    return jnp.asarray(qi, jnp.int32), jnp.asarray(ki, jnp.int32)


def _chain_init(m_sc, l_sc, acc_sc):
    m_sc[...] = jnp.full_like(m_sc, NEG)
    l_sc[...] = jnp.zeros_like(l_sc)
    acc_sc[...] = jnp.zeros_like(acc_sc)


def _chain_update(c, s, mask, queries_may_be_empty, v_t, m_sc, l_sc, acc_sc, c2):
    if mask is not None:
        s = jnp.where(mask, s, NEG)
    m_prev = m_sc[c]
    m_new = jnp.maximum(m_prev, jnp.max(s, axis=0, keepdims=True))
    alpha = jnp.exp2((m_prev - m_new) * c2)
    p = jnp.exp2((s - m_new) * c2)
    if queries_may_be_empty:
        p = jnp.where(mask, p, 0.0)
    l_sc[c] = alpha * l_sc[c] + jnp.sum(p, axis=0, keepdims=True)
    acc_sc[c] = alpha * acc_sc[c] + _dot(v_t, p.astype(BF16))
    m_sc[c] = m_new


def _chain_scratch(n_chains, t):
    return [pltpu.VMEM((n_chains, 1, t), F32), pltpu.VMEM((n_chains, 1, t), F32),
            pltpu.VMEM((n_chains, LANES, t), F32)]


def _tile_causal(t):
    return lax.broadcasted_iota(jnp.int32, (t, t), 0) <= lax.broadcasted_iota(jnp.int32, (t, t), 1)


def _head_rows(ref, h):
    return ref[h * LANES:(h + 1) * LANES, :]


def _heads(ref, h):
    return ref[:, h * LANES:(h + 1) * LANES]


def _da_kernel(qt_ref, kt_ref, q_ref, k_ref, vt_ref, lam_ref, sub_ref, o_ref, m_sc, l_sc, acc_sc,
               *, c2, lam_init, t, hb):
    step = pl.program_id(2)
    qi = qt_ref[step]
    ki = kt_ref[step]

    @pl.when(ki == 0)
    def _():
        _chain_init(m_sc, l_sc, acc_sc)

    lane = lax.broadcasted_iota(jnp.int32, (1, LANES), 1)
    first_map = (lane % HALF_LANES) < (HALF_LANES // 2)

    def update(mask):
        for h in range(hb):
            q = _heads(q_ref, h)
            k = _heads(k_ref, h)
            v_t = _head_rows(vt_ref, h)
            zero = jnp.zeros_like(q)
            for j, qm in enumerate((jnp.where(first_map, q, zero), jnp.where(first_map, zero, q))):
                _chain_update(2 * h + j, _dot_nt(k, qm), mask, False, v_t, m_sc, l_sc, acc_sc, c2)

    @pl.when(ki < qi)
    def _():
        update(None)

    @pl.when(ki == qi)
    def _():
        update(_tile_causal(t))
        lam = lam_ref[...]
        lam_full = (jnp.exp(jnp.sum(lam[0:1] * lam[1:2], axis=-1, keepdims=True))
                    - jnp.exp(jnp.sum(lam[2:3] * lam[3:4], axis=-1, keepdims=True)) + lam_init)
        for h in range(hb):
            o = (acc_sc[2 * h] * (1.0 / l_sc[2 * h])
                 - lam_full * (acc_sc[2 * h + 1] * (1.0 / l_sc[2 * h + 1])))
            ms = jnp.mean(o * o, axis=0, keepdims=True)
            o = o * lax.rsqrt(ms + NORM_EPS) * sub_ref[...]
            o_ref[:, h * LANES:(h + 1) * LANES] = (o * (1.0 - lam_init)).T.astype(o_ref.dtype)


def _da_column_order(heads):
    dk = DA_QK_DIM
    hd = dk // 2
    per = 4 * dk + DA_V_DIM
    q_idx, k_idx, v_idx = [], [], []
    for h in range(heads):
        base = h * per
        for out, off in ((q_idx, 0), (k_idx, 2 * dk)):
            a = base + off + np.arange(dk)
            b = base + off + dk + np.arange(dk)
            out += [a[:hd], b[:hd], a[hd:], b[hd:]]
        v_idx.append(base + 4 * dk + np.arange(DA_V_DIM))
    return np.concatenate(q_idx + k_idx + v_idx)


def _diff_attention(hres, norm_w, w_in, lam, subln_w, w_out, lam_init, batch, seq):
    m, d = hres.shape
    heads = w_in.shape[1] // (4 * DA_QK_DIM + DA_V_DIM)
    hw = heads * LANES
    w_perm = w_in[:, _da_column_order(heads)].astype(BF16)
    cos, sin = _rope_tables(np.arange(seq), DA_QK_DIM // 2)
    tn = 512
    proj = _linear(hres, w_perm[:, :2 * hw], n_out=2 * hw, norm_w=norm_w, tn=tn,
                   rope=(cos, sin, 0, 2 * hw // tn, seq), name="da_qk_proj")
    v_t = _linear(hres, w_perm[:, 2 * hw:], n_out=hw, norm_w=norm_w, tn=tn, transpose_out=True,
                  name="da_vt_proj")

    t = min(512, seq)
    nq = seq // t
    qt, kt = _tri_tables(nq)
    hb = 2
    hg = heads // hb
    bw = hb * LANES
    kern = functools.partial(_da_kernel, c2=DA_QK_DIM ** -0.5 * LOG2E, lam_init=lam_init, t=t, hb=hb)
    attn = pl.pallas_call(
        kern,
        grid_spec=pltpu.PrefetchScalarGridSpec(
            num_scalar_prefetch=2,
            grid=(batch, hg, qt.shape[0]),
            in_specs=[
                pl.BlockSpec((t, bw), lambda b, h, s, qt, kt: (b * nq + qt[s], h)),
                pl.BlockSpec((t, bw), lambda b, h, s, qt, kt: (b * nq + kt[s], hg + h)),
                pl.BlockSpec((bw, t), lambda b, h, s, qt, kt: (h, b * nq + kt[s])),
                pl.BlockSpec(lam.shape, lambda b, h, s, qt, kt: (0, 0)),
                pl.BlockSpec((DA_V_DIM, 1), lambda b, h, s, qt, kt: (0, 0)),
            ],
            out_specs=pl.BlockSpec((t, bw), lambda b, h, s, qt, kt: (b * nq + qt[s], h)),
            scratch_shapes=_chain_scratch(2 * hb, t),
        ),
        out_shape=jax.ShapeDtypeStruct((m, hw), BF16),
        compiler_params=_cparams(("parallel", "parallel", "arbitrary")),
        name="da_attention",
    )(qt, kt, proj, proj, v_t, lam.astype(F32), subln_w.reshape(DA_V_DIM, 1).astype(F32))
    return _linear(attn, w_out.astype(BF16), n_out=d, residual=hres, out_dtype=F32, name="da_out_proj")


def _mla_kernel(qt_ref, kt_ref, qn_ref, qr_ref, kn_ref, kr_ref, vt_ref, o_ref, m_sc, l_sc, acc_sc,
                *, c2, t, hb):
    step = pl.program_id(2)
    qi = qt_ref[step]
    ki = kt_ref[step]

    @pl.when(ki == 0)
    def _():
        _chain_init(m_sc, l_sc, acc_sc)

    kr = kr_ref[...].astype(BF16)

    def update(mask):
        for h in range(hb):
            q = jnp.concatenate([_heads(qn_ref, h), _heads(qr_ref, h)], axis=1)
            k = jnp.concatenate([_heads(kn_ref, h), kr], axis=1)
            _chain_update(h, _dot_nt(k, q), mask, False, _head_rows(vt_ref, h), m_sc, l_sc, acc_sc, c2)

    @pl.when(ki < qi)
    def _():
        update(None)

    @pl.when(ki == qi)
    def _():
        update(_tile_causal(t))
        for h in range(hb):
            o_ref[:, h * LANES:(h + 1) * LANES] = (acc_sc[h] * (1.0 / l_sc[h])).T.astype(o_ref.dtype)


def _mla_q_columns(heads):
    per = MLA_NOPE_DIM + MLA_ROPE_DIM
    hd = MLA_ROPE_DIM // 2
    idx, keep = [], []
    for h in range(heads):
        idx.append(h * per + np.arange(MLA_NOPE_DIM))
        keep.append(np.ones(MLA_NOPE_DIM))
    for h in range(heads):
        s = h % 2
        g_idx = np.zeros(LANES, np.int64)
        g_keep = np.zeros(LANES)
        r0 = h * per + MLA_NOPE_DIM
        g_idx[s * hd:(s + 1) * hd] = r0 + np.arange(hd)
        g_idx[HALF_LANES + s * hd:HALF_LANES + (s + 1) * hd] = r0 + hd + np.arange(hd)
        g_keep[s * hd:(s + 1) * hd] = 1.0
        g_keep[HALF_LANES + s * hd:HALF_LANES + (s + 1) * hd] = 1.0
        idx.append(g_idx)
        keep.append(g_keep)
    return np.concatenate(idx), np.concatenate(keep)


def _mla_attention(hres, norm_w, w_in, q_norm_w, w_uq, kv_norm_w, w_ukv, w_out, batch, seq):
    m, d = hres.shape
    heads = MLA_HEADS
    hw = heads * LANES
    hd = MLA_ROPE_DIM // 2
    assert MLA_Q_RANK == MLA_KV_RANK and MLA_NOPE_DIM == LANES and MLA_V_DIM == LANES
    rank = MLA_Q_RANK
    r0 = 2 * rank
    kr_cols = np.concatenate([r0 + np.arange(hd), r0 + np.arange(hd),
                              r0 + hd + np.arange(hd), r0 + hd + np.arange(hd)])
    w_in_x = jnp.concatenate([w_in[:, :2 * rank], w_in[:, kr_cols]], axis=1).astype(BF16)
    cos, sin = _rope_tables(np.arange(seq), hd)
    n_in = 2 * rank + LANES
    proj = _linear(hres, w_in_x, n_out=n_in, norm_w=norm_w, tn=LANES, out_dtype=F32,
                   rope=(cos, sin, 2 * rank // LANES, n_in // LANES, seq), name="mla_in_proj")

    q_idx, q_keep = _mla_q_columns(heads)
    w_uq_x = (w_uq[:, q_idx] * jnp.asarray(q_keep, F32)[None, :]).astype(BF16)
    tn = 512
    qcat = _linear(proj, w_uq_x, n_out=2 * hw, k=rank, x_col_block=0, norm_w=q_norm_w, tn=tn,
                   rope=(cos, sin, hw // tn, 2 * hw // tn, seq), name="mla_q_proj")
    k_nope = _linear(proj, w_ukv[:, :, :MLA_NOPE_DIM].reshape(rank, hw).astype(BF16), n_out=hw, k=rank,
                     x_col_block=1, norm_w=kv_norm_w, tn=tn, name="mla_k_proj")
    v_t = _linear(proj, w_ukv[:, :, MLA_NOPE_DIM:].reshape(rank, hw).astype(BF16), n_out=hw, k=rank,
                  x_col_block=1, norm_w=kv_norm_w, tn=tn, transpose_out=True, name="mla_vt_proj")

    t = min(512, seq)
    nq = seq // t
    qt, kt = _tri_tables(nq)
    kr_block = 2 * rank // LANES
    hb = 4
    hg = heads // hb
    bw = hb * LANES
    kern = functools.partial(_mla_kernel, c2=(MLA_NOPE_DIM + MLA_ROPE_DIM) ** -0.5 * LOG2E, t=t, hb=hb)
    attn = pl.pallas_call(
        kern,
        grid_spec=pltpu.PrefetchScalarGridSpec(
            num_scalar_prefetch=2,
            grid=(batch, hg, qt.shape[0]),
            in_specs=[
                pl.BlockSpec((t, bw), lambda b, h, s, qt, kt: (b * nq + qt[s], h)),
                pl.BlockSpec((t, bw), lambda b, h, s, qt, kt: (b * nq + qt[s], hg + h)),
                pl.BlockSpec((t, bw), lambda b, h, s, qt, kt: (b * nq + kt[s], h)),
                pl.BlockSpec((t, LANES), lambda b, h, s, qt, kt: (b * nq + kt[s], kr_block)),
                pl.BlockSpec((bw, t), lambda b, h, s, qt, kt: (h, b * nq + kt[s])),
            ],
            out_specs=pl.BlockSpec((t, bw), lambda b, h, s, qt, kt: (b * nq + qt[s], h)),
            scratch_shapes=_chain_scratch(hb, t),
        ),
        out_shape=jax.ShapeDtypeStruct((m, hw), BF16),
        compiler_params=_cparams(("parallel", "parallel", "arbitrary")),
        name="mla_attention",
    )(qt, kt, qcat, qcat, k_nope, proj, v_t)
    return _linear(attn, w_out.astype(BF16), n_out=d, residual=hres, out_dtype=F32, name="mla_out_proj")


def _sb_kernel(qt_ref, kt_ref, q_ref, k_ref, v_ref, tri_ref, o_ref, carry_sc, acc_sc, *, scale, t, hb):
    step = pl.program_id(2)
    qi = qt_ref[step]
    ki = kt_ref[step]

    @pl.when(ki == qi)
    def _():
        carry_sc[...] = jnp.zeros_like(carry_sc)
        acc_sc[...] = jnp.zeros_like(acc_sc)

    tri = tri_ref[...]

    def update(masked):
        for h in range(hb):
            q = _heads(q_ref, h)
            carry = carry_sc[h]
            acc = acc_sc[h]
            for c in range(t // LANES - 1, -1, -1):
                kc = k_ref[c * LANES:(c + 1) * LANES, h * LANES:(h + 1) * LANES]
                vc = v_ref[c * LANES:(c + 1) * LANES, h * LANES:(h + 1) * LANES]
                z = _dot_nt(q, kc) * scale
                log_beta = jnp.minimum(z, 0.0) - jnp.log(1.0 + jnp.exp(-jnp.abs(z)))
                log_rest = log_beta - z
                if masked:
                    qpos = lax.broadcasted_iota(jnp.int32, z.shape, 0)
                    kpos = c * LANES + lax.broadcasted_iota(jnp.int32, z.shape, 1)
                    mask = kpos < qpos
                    log_rest = jnp.where(mask, log_rest, 0.0)
                hi, lo = _split_bf16(log_rest)
                tail = _dot(hi, tri) + _dot(lo, tri) + carry
                a = jnp.exp(log_beta + tail)
                if masked:
                    a = jnp.where(mask, a, 0.0)
                acc = acc + _dot(a.astype(BF16), vc)
                carry = carry + jnp.sum(log_rest, axis=-1, keepdims=True)
            carry_sc[h] = carry
            acc_sc[h] = acc

    @pl.when(ki == qi)
    def _():
        update(True)

    @pl.when(ki < qi)
    def _():
        update(False)

    @pl.when(ki == 0)
    def _():
        for h in range(hb):
            o_ref[:, h * LANES:(h + 1) * LANES] = acc_sc[h].astype(o_ref.dtype)


def _sb_attention(hres, norm_w, w_in, w_out, batch, seq):
    m, d = hres.shape
    heads = SB_HEADS
    hw = heads * LANES
    assert w_in.shape[1] == 3 * hw
    qkv = _linear(hres, w_in.astype(BF16), n_out=3 * hw, norm_w=norm_w, name="sb_qkv_proj")
    t = min(512, seq)
    nq = seq // t
    qt, kt = _tri_tables(nq, reverse=True)
    tri = jnp.asarray(np.arange(LANES)[:, None] > np.arange(LANES)[None, :], BF16)
    hb = 4
    hg = heads // hb
    bw = hb * LANES
    kern = functools.partial(_sb_kernel, scale=LANES ** -0.5, t=t, hb=hb)
    attn = pl.pallas_call(
        kern,
        grid_spec=pltpu.PrefetchScalarGridSpec(
            num_scalar_prefetch=2,
            grid=(batch, hg, qt.shape[0]),
            in_specs=[
                pl.BlockSpec((t, bw), lambda b, h, s, qt, kt: (b * nq + qt[s], h)),
                pl.BlockSpec((t, bw), lambda b, h, s, qt, kt: (b * nq + kt[s], hg + h)),
                pl.BlockSpec((t, bw), lambda b, h, s, qt, kt: (b * nq + kt[s], 2 * hg + h)),
                pl.BlockSpec((LANES, LANES), lambda b, h, s, qt, kt: (0, 0)),
            ],
            out_specs=pl.BlockSpec((t, bw), lambda b, h, s, qt, kt: (b * nq + qt[s], h)),
            scratch_shapes=[pltpu.VMEM((hb, t, 1), F32), pltpu.VMEM((hb, t, LANES), F32)],
        ),
        out_shape=jax.ShapeDtypeStruct((m, hw), BF16),
        compiler_params=_cparams(("parallel", "parallel", "arbitrary")),
        name="sb_attention",
    )(qt, kt, qkv, qkv, qkv, tri)
    return _linear(attn, w_out.astype(BF16), n_out=d, residual=hres, out_dtype=F32, name="sb_out_proj")


def _nsa_compress_kernel(x_ref, w_ref, pe_ref, cos_ref, sin_ref, o_ref, *, rope):
    x = x_ref[0]
    half = w_ref.shape[0] // 2
    y_lo = _dot(x, w_ref[:half, :])
    y_hi = _dot(x, w_ref[half:, :])
    n = y_hi.shape[0]
    y = y_lo + pltpu.roll(y_hi, n - 1, 0) + _dot(pe_ref[...], w_ref[...])[0:1]
    if rope:
        y = y * cos_ref[...] + pltpu.roll(y, HALF_LANES, 1) * sin_ref[...]
    o_ref[0] = y.astype(o_ref.dtype)


def _nsa_compress(x_chunks, w, pe, cos, sin, rope):
    bg, n_chunk, kdim = x_chunks.shape
    dh = w.shape[1]
    return pl.pallas_call(
        functools.partial(_nsa_compress_kernel, rope=rope),
        grid=(bg,),
        in_specs=[pl.BlockSpec((1, n_chunk, kdim), lambda i: (i, 0, 0)),
                  pl.BlockSpec(w.shape, lambda i: (0, 0)),
                  pl.BlockSpec(pe.shape, lambda i: (0, 0)),
                  pl.BlockSpec(cos.shape, lambda i: (0, 0)),
                  pl.BlockSpec(sin.shape, lambda i: (0, 0))],
        out_specs=pl.BlockSpec((1, n_chunk, dh), lambda i: (i, 0, 0)),
        out_shape=jax.ShapeDtypeStruct((bg, n_chunk, dh), BF16),
        compiler_params=_cparams(("parallel",)),
    )(x_chunks, w, pe, cos, sin)


def _head_gate(gates, branch, head):
    lane = lax.broadcasted_iota(jnp.int32, gates.shape, 1)
    return jnp.sum(jnp.where(lane == branch * NSA_HEADS + head, gates, 0.0), axis=-1, keepdims=True)


def _nsa_cmp_kernel(q_ref, kc_ref, vc_ref, ovt_ref, g_ref, o_ref, sel_ref, *, scale, tq, rep, n_sel):
    qi = pl.program_id(2)
    grp = pl.program_id(1)
    qs = jnp.concatenate([q_ref[:, r * LANES:(r + 1) * LANES] for r in range(rep)], axis=0)
    kc = kc_ref[0]
    vc = vc_ref[0]
    n_cmp = kc.shape[0]
    rows = rep * tq

    s = _dot_nt(qs, kc) * scale
    qpos = qi * tq + lax.broadcasted_iota(jnp.int32, (rep, tq, n_cmp), 1).reshape(rows, n_cmp)
    cend = lax.broadcasted_iota(jnp.int32, (rows, n_cmp), 1) * NSA_CMP_STRIDE + (NSA_CMP_BLOCK - 1)
    mask = cend <= qpos
    s = jnp.where(mask, s, NEG)
    p = jnp.where(mask, jnp.exp(s - jnp.max(s, axis=-1, keepdims=True)), 0.0)
    den = jnp.sum(p, axis=-1, keepdims=True)
    p = p / jnp.where(den > 0.0, den, 1.0)
    o = _dot(p.astype(BF16), vc)
    gates = g_ref[...]
    for r in range(rep):
        gate = _head_gate(gates, 0, grp * rep + r)
        o_ref[:, r * LANES:(r + 1) * LANES] = gate * o[r * tq:(r + 1) * tq, :]

    st = _dot_nt(kc, qs) * scale
    qpos_t = qi * tq + lax.broadcasted_iota(jnp.int32, (n_cmp, rep, tq), 2).reshape(n_cmp, rows)
    cend_t = lax.broadcasted_iota(jnp.int32, (n_cmp, rows), 0) * NSA_CMP_STRIDE + (NSA_CMP_BLOCK - 1)
    mask_t = cend_t <= qpos_t
    st = jnp.where(mask_t, st, NEG)
    pt = jnp.where(mask_t, jnp.exp(st - jnp.max(st, axis=0, keepdims=True)), 0.0)
    den_t = jnp.sum(pt, axis=0, keepdims=True)
    pt = pt / jnp.where(den_t > 0.0, den_t, 1.0)
    psum = pt[:, 0:tq]
    for r in range(1, rep):
        psum = psum + pt[:, r * tq:(r + 1) * tq]
    hi, lo = _split_bf16(psum)
    ovt = ovt_ref[...]
    imp = _dot(ovt, hi) + _dot(ovt, lo)

    n_pad = imp.shape[0]
    blk = lax.broadcasted_iota(jnp.int32, (n_pad, tq), 0)
    cur = (qi * tq + lax.broadcasted_iota(jnp.int32, (n_pad, tq), 1)) // NSA_SLC_BLOCK
    forced = (blk == 0) | (blk == cur) | (blk == cur - 1)
    val = jnp.where(forced, jnp.inf, jnp.where(blk <= cur, imp, -jnp.inf))
    sel = jnp.zeros((n_pad, tq), F32)
    for _ in range(n_sel):
        top = jnp.max(val, axis=0, keepdims=True)
        first = jnp.min(jnp.where(val == top, blk, n_pad), axis=0, keepdims=True)
        pick = blk == first
        sel = jnp.where(pick, 1.0, sel)
        val = jnp.where(pick, -jnp.inf, val)
    sel_ref[...] = sel.astype(sel_ref.dtype)


def _nsa_win_kernel(q_ref, k_ref, vt_ref, g_ref, prev_ref, o_ref, m_sc, l_sc, acc_sc, *, c2, t, rep, nwin):
    qi = pl.program_id(2)
    kk = pl.program_id(3)
    grp = pl.program_id(1)
    kb = qi - (nwin - 1) + kk

    @pl.when(kk == 0)
    def _():
        _chain_init(m_sc, l_sc, acc_sc)

    def update(mask):
        k = k_ref[...]
        v_t = vt_ref[...]
        for r in range(rep):
            _chain_update(r, _dot_nt(k, _heads(q_ref, r)), mask, mask is not None, v_t, m_sc, l_sc, acc_sc, c2)

    @pl.when((kk == 0) & (kb >= 0))
    def _():
        update(jnp.logical_not(_tile_causal(t)))

    @pl.when((kk > 0) & (kk < nwin - 1) & (kb >= 0))
    def _():
        update(None)

    @pl.when(kk == nwin - 1)
    def _():
        update(_tile_causal(t))
        gates = g_ref[...]
        for r in range(rep):
            gate = _head_gate(gates, 2, grp * rep + r)
            o = (acc_sc[r] * (1.0 / l_sc[r])).T
            o_ref[:, r * LANES:(r + 1) * LANES] = _heads(prev_ref, r) + gate * o


def _nsa_slc_kernel(qt_ref, kt_ref, q_ref, k_ref, vt_ref, sel_ref, g_ref, prev_ref, o_ref,
                    m_sc, l_sc, acc_sc, *, c2, t, rep):
    step = pl.program_id(2)
    grp = pl.program_id(1)
    qi = qt_ref[step]
    ki = kt_ref[step]

    @pl.when(ki == 0)
    def _():
        _chain_init(m_sc, l_sc, acc_sc)

    n_pad = sel_ref.shape[0]
    blk = lax.broadcasted_iota(jnp.int32, (t, n_pad), 1)
    kblk = (ki * t + lax.broadcasted_iota(jnp.int32, (t, n_pad), 0)) // NSA_SLC_BLOCK
    expand = jnp.where(blk == kblk, 1.0, 0.0).astype(BF16)
    chosen = _dot(expand, sel_ref[...]) > 0.5

    def update(mask):
        k = k_ref[...]
        v_t = vt_ref[...]
        for r in range(rep):
            _chain_update(r, _dot_nt(k, _heads(q_ref, r)), mask, True, v_t, m_sc, l_sc, acc_sc, c2)

    @pl.when(ki < qi)
    def _():
        update(chosen)

    @pl.when(ki == qi)
    def _():
        update(chosen & _tile_causal(t))
        gates = g_ref[...]
        for r in range(rep):
            gate = _head_gate(gates, 1, grp * rep + r)
            l = l_sc[r]
            o = (acc_sc[r] * (1.0 / jnp.where(l > 0.0, l, 1.0))).T
            o_ref[:, r * LANES:(r + 1) * LANES] = (_heads(prev_ref, r) + gate * o).astype(o_ref.dtype)


def _nsa_attention(hres, norm_w, w_in, cmp_pe, cmp_w, w_out, batch, seq):
    m, d = hres.shape
    heads, groups = NSA_HEADS, NSA_GROUPS
    rep = heads // groups
    dh = d // heads
    assert dh == LANES
    kvw = groups * dh
    n_main = heads * dh + 6 * kvw
    scale = dh ** -0.5
    gw = rep * LANES

    hq = heads * dh
    cos, sin = _rope_tables(np.arange(seq), dh // 2)
    kv_cols = lambda i: np.arange(hq + i * kvw, hq + (i + 1) * kvw)
    main_cols = np.concatenate([np.arange(hq), kv_cols(0), kv_cols(1), kv_cols(2), kv_cols(4)])
    proj = _linear(hres, w_in[:, main_cols].astype(BF16), n_out=main_cols.size, norm_w=norm_w,
                   rope=(cos, sin, 0, hq // 512, seq), name="nsa_qk_proj")
    v_t = _linear(hres, w_in[:, np.concatenate([kv_cols(3), kv_cols(5)])].astype(BF16), n_out=2 * kvw,
                  norm_w=norm_w, transpose_out=True, name="nsa_vt_proj")
    w_gate = jnp.pad(w_in[:, n_main:], ((0, 0), (0, LANES - 3 * heads))).astype(BF16)
    gates = _linear(hres, w_gate, n_out=LANES, norm_w=norm_w, act="sigmoid", out_dtype=F32,
                    name="nsa_gate_proj")
    k_slc_blk = (hq + 2 * kvw) // LANES
    k_win_blk = (hq + 3 * kvw) // LANES

    n_chunk = seq // NSA_CMP_STRIDE
    assert NSA_CMP_BLOCK == 2 * NSA_CMP_STRIDE
    cmp_end = np.arange(n_chunk) * NSA_CMP_STRIDE + NSA_CMP_BLOCK - 1
    ccos, csin = _rope_tables(cmp_end, dh // 2)

    def chunks(i):
        lo = hq + i * kvw
        x = proj[:, lo:lo + kvw].reshape(batch, n_chunk, NSA_CMP_STRIDE, groups, dh)
        return x.transpose(0, 3, 1, 2, 4).reshape(batch * groups, n_chunk, NSA_CMP_STRIDE * dh)

    def cmp_weights(i):
        return (cmp_w[i].reshape(NSA_CMP_BLOCK * dh, dh).astype(BF16),
                jnp.broadcast_to(cmp_pe[i].reshape(1, NSA_CMP_BLOCK * dh), (8, NSA_CMP_BLOCK * dh)).astype(BF16))

    k_cmp = _nsa_compress(chunks(0), *cmp_weights(0), ccos, csin, True)
    v_cmp = _nsa_compress(chunks(1), *cmp_weights(1), ccos, csin, False)

    n_slc = seq // NSA_SLC_BLOCK
    n_sel = min(NSA_SLC_TOPK, n_slc)
    n_pad = max(LANES, n_slc)
    n_cmp = n_chunk - NSA_CMP_BLOCK // NSA_CMP_STRIDE + 1
    cs = np.arange(n_chunk)[None, :] * NSA_CMP_STRIDE
    js = np.arange(n_pad)[:, None] * NSA_SLC_BLOCK
    ovt = ((cs < js + NSA_SLC_BLOCK) & (cs + NSA_CMP_BLOCK > js)
           & (np.arange(n_chunk)[None, :] < n_cmp) & (np.arange(n_pad)[:, None] < n_slc))
    ovt = jnp.asarray(ovt, BF16)

    tq = min(256, seq)
    nqc = seq // tq
    o_c, sel = pl.pallas_call(
        functools.partial(_nsa_cmp_kernel, scale=scale, tq=tq, rep=rep, n_sel=n_sel),
        grid=(batch, groups, nqc),
        in_specs=[
            pl.BlockSpec((tq, gw), lambda b, g, i: (b * nqc + i, g)),
            pl.BlockSpec((1, n_chunk, dh), lambda b, g, i: (b * groups + g, 0, 0)),
            pl.BlockSpec((1, n_chunk, dh), lambda b, g, i: (b * groups + g, 0, 0)),
            pl.BlockSpec(ovt.shape, lambda b, g, i: (0, 0)),
            pl.BlockSpec((tq, LANES), lambda b, g, i: (b * nqc + i, 0)),
        ],
        out_specs=[
            pl.BlockSpec((tq, gw), lambda b, g, i: (b * nqc + i, g)),
            pl.BlockSpec((n_pad, tq), lambda b, g, i: (b * groups + g, i)),
        ],
        out_shape=[jax.ShapeDtypeStruct((m, heads * dh), F32),
                   jax.ShapeDtypeStruct((batch * groups * n_pad, seq), BF16)],
        compiler_params=_cparams(("parallel", "parallel", "arbitrary")),
        name="nsa_compressed_select",
    )(proj, k_cmp, v_cmp, ovt, gates)

    t = min(512, seq)
    nq = seq // t
    c2 = scale * LOG2E
    assert NSA_WINDOW % t == 0
    nwin = NSA_WINDOW // t + 1
    key_tile = lambda i, kk: jnp.maximum(i - (nwin - 1) + kk, 0)
    o_cw = pl.pallas_call(
        functools.partial(_nsa_win_kernel, c2=c2, t=t, rep=rep, nwin=nwin),
        grid=(batch, groups, nq, nwin),
        in_specs=[
            pl.BlockSpec((t, gw), lambda b, g, i, kk: (b * nq + i, g)),
            pl.BlockSpec((t, LANES), lambda b, g, i, kk: (b * nq + key_tile(i, kk), k_win_blk + g)),
            pl.BlockSpec((LANES, t), lambda b, g, i, kk: (groups + g, b * nq + key_tile(i, kk))),
            pl.BlockSpec((t, LANES), lambda b, g, i, kk: (b * nq + i, 0)),
            pl.BlockSpec((t, gw), lambda b, g, i, kk: (b * nq + i, g)),
        ],
        out_specs=pl.BlockSpec((t, gw), lambda b, g, i, kk: (b * nq + i, g)),
        out_shape=jax.ShapeDtypeStruct((m, heads * dh), F32),
        scratch_shapes=_chain_scratch(rep, t),
        compiler_params=_cparams(("parallel", "parallel", "parallel", "arbitrary")),
        name="nsa_window",
    )(proj, proj, v_t, gates, o_c)

    qt, kt = _tri_tables(nq)
    attn = pl.pallas_call(
        functools.partial(_nsa_slc_kernel, c2=c2, t=t, rep=rep),
        grid_spec=pltpu.PrefetchScalarGridSpec(
            num_scalar_prefetch=2,
            grid=(batch, groups, qt.shape[0]),
            in_specs=[
                pl.BlockSpec((t, gw), lambda b, g, s, qt, kt: (b * nq + qt[s], g)),
                pl.BlockSpec((t, LANES), lambda b, g, s, qt, kt: (b * nq + kt[s], k_slc_blk + g)),
                pl.BlockSpec((LANES, t), lambda b, g, s, qt, kt: (g, b * nq + kt[s])),
                pl.BlockSpec((n_pad, t), lambda b, g, s, qt, kt: (b * groups + g, qt[s])),
                pl.BlockSpec((t, LANES), lambda b, g, s, qt, kt: (b * nq + qt[s], 0)),
                pl.BlockSpec((t, gw), lambda b, g, s, qt, kt: (b * nq + qt[s], g)),
            ],
            out_specs=pl.BlockSpec((t, gw), lambda b, g, s, qt, kt: (b * nq + qt[s], g)),
            scratch_shapes=_chain_scratch(rep, t),
        ),
        out_shape=jax.ShapeDtypeStruct((m, heads * dh), BF16),
        compiler_params=_cparams(("parallel", "parallel", "arbitrary")),
        name="nsa_selected",
    )(qt, kt, proj, proj, v_t, sel, gates, o_cw)
    return _linear(attn, w_out.astype(BF16), n_out=d, residual=hres, out_dtype=F32, name="nsa_out_proj")


def _swiglu(hres, norm_w, w_gu, w_down):
    d = hres.shape[1]
    ff = w_down.shape[0]
    tn = 512
    hidden = _linear(hres, w_gu.astype(BF16), n_out=ff, w_tile_offsets=(0, ff // tn), norm_w=norm_w,
                     act="swiglu", tn=tn, name="ffn_gate_up")
    return _linear(hidden, w_down.astype(BF16), n_out=d, residual=hres, out_dtype=F32, name="ffn_down")


def _router_kernel(x_ref, nw_ref, w_ref, b_ref, f_ref, idx_ref, gate_ref, *, n_experts):
    xf = x_ref[...]
    ms = jnp.mean(xf * xf, axis=-1, keepdims=True)
    f = xf * lax.rsqrt(ms + NORM_EPS) * nw_ref[...]
    f_ref[...] = f
    fh, fl = _split_bf16(f)
    w = w_ref[...]
    wh, wl = _split_bf16(w)
    logits = _dot(fh, wh) + _dot(fl, wh) + _dot(fh, wl) + b_ref[...]
    lane = lax.broadcasted_iota(jnp.int32, logits.shape, 1)
    logits = jnp.where(lane < n_experts, logits, -jnp.inf)
    v1 = jnp.max(logits, axis=-1, keepdims=True)
    i1 = jnp.min(jnp.where(logits == v1, lane, LANES), axis=-1, keepdims=True)
    rest = jnp.where(lane == i1, -jnp.inf, logits)
    v2 = jnp.max(rest, axis=-1, keepdims=True)
    i2 = jnp.min(jnp.where(rest == v2, lane, LANES), axis=-1, keepdims=True)
    e2 = jnp.exp(v2 - v1)
    g1 = 1.0 / (1.0 + e2)
    g2 = e2 / (1.0 + e2)
    idx_ref[...] = jnp.where(lane == 0, i1, jnp.where(lane == 1, i2, 0))
    gate_ref[...] = jnp.where(lane == 0, g1, jnp.where(lane == 1, g2, 0.0))


def _gather_kernel(src_ref, nvalid_ref, x_hbm, o_ref, buf, sem):
    blk = pl.program_id(0)
    rows = buf.shape[0]
    nvalid = nvalid_ref[blk]

    @pl.when(blk == 0)
    def _():
        buf[...] = jnp.zeros_like(buf)

    def row_copy(r):
        return pltpu.make_async_copy(x_hbm.at[pl.ds(src_ref[blk * rows + r], 1), :],
                                     buf.at[pl.ds(r, 1), :], sem)

    def start(r, c):
        row_copy(r).start()
        return c

    def wait(r, c):
        row_copy(r).wait()
        return c

    lax.fori_loop(0, nvalid, start, 0)
    lax.fori_loop(0, nvalid, wait, 0)
    rid = lax.broadcasted_iota(jnp.int32, (rows, 1), 0)
    o_ref[...] = jnp.where(rid < nvalid, buf[...], 0.0).astype(o_ref.dtype)


def _weight_tile_is_new(be_ref, blk):
    return (blk == 0) | (be_ref[blk] != be_ref[jnp.maximum(blk - 1, 0)])


def _expert_up_kernel(be_ref, nv_ref, x_ref, wg_ref, wu_ref, o_ref, wg_sc, wu_sc):
    blk = pl.program_id(1)

    @pl.when(_weight_tile_is_new(be_ref, blk))
    def _():
        wg_sc[...] = wg_ref[0].astype(BF16)
        wu_sc[...] = wu_ref[0].astype(BF16)

    @pl.when(nv_ref[blk] > 0)
    def _():
        x = x_ref[...]
        g = _dot(x, wg_sc[...])
        u = _dot(x, wu_sc[...])
        o_ref[...] = (g * jax.nn.sigmoid(g) * u).astype(o_ref.dtype)

    @pl.when(nv_ref[blk] == 0)
    def _():
        o_ref[...] = jnp.zeros_like(o_ref)


def _expert_down_kernel(be_ref, nv_ref, h_ref, w_ref, o_ref, w_sc):
    blk = pl.program_id(1)

    @pl.when(_weight_tile_is_new(be_ref, blk))
    def _():
        w_sc[...] = w_ref[0].astype(BF16)

    @pl.when(nv_ref[blk] > 0)
    def _():
        o_ref[...] = _dot(h_ref[...], w_sc[...])

    @pl.when(nv_ref[blk] == 0)
    def _():
        o_ref[...] = jnp.zeros_like(o_ref)


def _combine_kernel(dst_ref, y_hbm, res_ref, gate_ref, o_ref, buf0, buf1, sem):
    i = pl.program_id(0)
    rows = buf0.shape[0]

    def row_copy(r, slot, buf):
        return pltpu.make_async_copy(y_hbm.at[pl.ds(dst_ref[(i * rows + r) * TOP_K + slot], 1), :],
                                     buf.at[pl.ds(r, 1), :], sem.at[slot])

    def start(r, c):
        row_copy(r, 0, buf0).start()
        row_copy(r, 1, buf1).start()
        return c

    def wait(r, c):
        row_copy(r, 0, buf0).wait()
        row_copy(r, 1, buf1).wait()
        return c

    lax.fori_loop(0, rows, start, 0)
    lax.fori_loop(0, rows, wait, 0)
    gate = gate_ref[...]
    o_ref[...] = res_ref[...] + (buf0[...] * gate[:, 0:1] + buf1[...] * gate[:, 1:2])


def _moe(hres, norm_w, w_router, b_router, w_gu, w_down):
    m, d = hres.shape
    n_exp = w_router.shape[1]
    ff = w_down.shape[1]
    rb = MOE_ROW_BLOCK
    assert TOP_K == 2 and m % rb == 0
    tm = min(512, m)

    w_r = jnp.pad(w_router, ((0, 0), (0, LANES - n_exp))).astype(F32)
    b_r = jnp.pad(b_router, (0, LANES - n_exp)).reshape(1, LANES).astype(F32)
    f, idx, gate = pl.pallas_call(
        functools.partial(_router_kernel, n_experts=n_exp),
        grid=(m // tm,),
        in_specs=[pl.BlockSpec((tm, d), lambda i: (i, 0)),
                  pl.BlockSpec((1, d), lambda i: (0, 0)),
                  pl.BlockSpec((d, LANES), lambda i: (0, 0)),
                  pl.BlockSpec((1, LANES), lambda i: (0, 0))],
        out_specs=[pl.BlockSpec((tm, d), lambda i: (i, 0)),
                   pl.BlockSpec((tm, LANES), lambda i: (i, 0)),
                   pl.BlockSpec((tm, LANES), lambda i: (i, 0))],
        out_shape=[jax.ShapeDtypeStruct((m, d), F32),
                   jax.ShapeDtypeStruct((m, LANES), jnp.int32),
                   jax.ShapeDtypeStruct((m, LANES), F32)],
        compiler_params=_cparams(("parallel",)),
        name="moe_router",
    )(hres, norm_w.reshape(1, d).astype(F32), w_r, b_r)

    nk = m * TOP_K
    e_flat = idx[:, :TOP_K].reshape(nk)
    onehot = (e_flat[:, None] == jnp.arange(n_exp, dtype=jnp.int32)[None, :]).astype(jnp.int32)
    csum = jnp.cumsum(onehot, axis=0)
    rank = jnp.sum((csum - onehot) * onehot, axis=1)
    counts = csum[-1]
    padded = (counts + rb - 1) // rb * rb
    pend = jnp.cumsum(padded)
    pstart = pend - padded
    dest = (pstart[e_flat] + rank).astype(jnp.int32)
    n_rows = (-(-nk // rb)) * rb + n_exp * rb
    n_blocks = n_rows // rb
    src_tok = jnp.zeros((n_rows,), jnp.int32).at[dest].set(jnp.arange(nk, dtype=jnp.int32) // TOP_K)
    blk_start = jnp.arange(n_blocks, dtype=jnp.int32) * rb
    blk_expert = jnp.minimum(jnp.searchsorted(pend, blk_start, side="right"), n_exp - 1).astype(jnp.int32)
    nvalid = jnp.clip(pstart[blk_expert] + counts[blk_expert] - blk_start, 0, rb).astype(jnp.int32)

    xbuf = pl.pallas_call(
        _gather_kernel,
        grid_spec=pltpu.PrefetchScalarGridSpec(
            num_scalar_prefetch=2,
            grid=(n_blocks,),
            in_specs=[pl.BlockSpec(memory_space=pl.ANY)],
            out_specs=pl.BlockSpec((rb, d), lambda i, src, nv: (i, 0)),
            scratch_shapes=[pltpu.VMEM((rb, d), F32), pltpu.SemaphoreType.DMA(())],
        ),
        out_shape=jax.ShapeDtypeStruct((n_rows, d), BF16),
        compiler_params=_cparams(("arbitrary",)),
        name="moe_gather",
    )(src_tok, nvalid, f)

    tn = 512
    hidden = pl.pallas_call(
        _expert_up_kernel,
        grid_spec=pltpu.PrefetchScalarGridSpec(
            num_scalar_prefetch=2,
            grid=(ff // tn, n_blocks),
            in_specs=[pl.BlockSpec((rb, d), lambda j, i, be, nv: (i, 0)),
                      pl.BlockSpec((1, d, tn), lambda j, i, be, nv: (be[i], 0, j)),
                      pl.BlockSpec((1, d, tn), lambda j, i, be, nv: (be[i], 0, j + ff // tn))],
            out_specs=pl.BlockSpec((rb, tn), lambda j, i, be, nv: (i, j)),
            scratch_shapes=[pltpu.VMEM((d, tn), BF16), pltpu.VMEM((d, tn), BF16)],
        ),
        out_shape=jax.ShapeDtypeStruct((n_rows, ff), BF16),
        compiler_params=_cparams(("arbitrary", "arbitrary")),
        name="moe_expert_up",
    )(blk_expert, nvalid, xbuf, w_gu, w_gu)

    tnd = min(512, d)
    ybuf = pl.pallas_call(
        _expert_down_kernel,
        grid_spec=pltpu.PrefetchScalarGridSpec(
            num_scalar_prefetch=2,
            grid=(d // tnd, n_blocks),
            in_specs=[pl.BlockSpec((rb, ff), lambda j, i, be, nv: (i, 0)),
                      pl.BlockSpec((1, ff, tnd), lambda j, i, be, nv: (be[i], 0, j))],
            out_specs=pl.BlockSpec((rb, tnd), lambda j, i, be, nv: (i, j)),
            scratch_shapes=[pltpu.VMEM((ff, tnd), BF16)],
        ),
        out_shape=jax.ShapeDtypeStruct((n_rows, d), F32),
        compiler_params=_cparams(("arbitrary", "arbitrary")),
        name="moe_expert_down",
    )(blk_expert, nvalid, hidden, w_down)

    return pl.pallas_call(
        _combine_kernel,
        grid_spec=pltpu.PrefetchScalarGridSpec(
            num_scalar_prefetch=1,
            grid=(m // tm,),
            in_specs=[pl.BlockSpec(memory_space=pl.ANY),
                      pl.BlockSpec((tm, d), lambda i, dst: (i, 0)),
                      pl.BlockSpec((tm, LANES), lambda i, dst: (i, 0))],
            out_specs=pl.BlockSpec((tm, d), lambda i, dst: (i, 0)),
            scratch_shapes=[pltpu.VMEM((tm, d), F32), pltpu.VMEM((tm, d), F32),
                            pltpu.SemaphoreType.DMA((TOP_K,))],
        ),
        out_shape=jax.ShapeDtypeStruct((m, d), F32),
        compiler_params=_cparams(("arbitrary",)),
        name="moe_combine",
    )(dest, ybuf, hres, gate)


def kernel(x, attn_norm_w, ffn_norm_w, final_norm_w, da_w_in, da_lambda, da_subln_w, da_w_out, nsa_w_in, nsa_cmp_pe, nsa_cmp_w, nsa_w_out, mla_w_in, mla_q_norm_w, mla_w_uq, mla_kv_norm_w, mla_w_ukv, mla_w_out, sb_w_in, sb_w_out, ffn_w_gu, ffn_w_down, moe_w_router, moe_b_router, moe_w_gu, moe_w_down):
    batch, seq, d = x.shape
    depth = attn_norm_w.shape[0]
    h = x.reshape(batch * seq, d)
    for i in range(depth):
        kind = i % 4
        j = i // 4
        if kind == 0:
            lam_init = 0.8 - 0.6 * math.exp(-0.3 * i)
            h = _diff_attention(h, attn_norm_w[i], da_w_in[j], da_lambda[j], da_subln_w[j], da_w_out[j],
                                lam_init, batch, seq)
        elif kind == 1:
            h = _nsa_attention(h, attn_norm_w[i], nsa_w_in[j], nsa_cmp_pe[j], nsa_cmp_w[j], nsa_w_out[j],
                               batch, seq)
        elif kind == 2:
            h = _mla_attention(h, attn_norm_w[i], mla_w_in[j], mla_q_norm_w[j], mla_w_uq[j],
                               mla_kv_norm_w[j], mla_w_ukv[j], mla_w_out[j], batch, seq)
        else:
            h = _sb_attention(h, attn_norm_w[i], sb_w_in[j], sb_w_out[j], batch, seq)
        if i % 2 == 0:
            h = _swiglu(h, ffn_norm_w[i], ffn_w_gu[i // 2], ffn_w_down[i // 2])
        else:
            h = _moe(h, ffn_norm_w[i], moe_w_router[i // 2], moe_b_router[i // 2],
                     moe_w_gu[i // 2], moe_w_down[i // 2])
    return _rmsnorm(h, final_norm_w.astype(F32)).reshape(batch, seq, d)
```

```python
import functools
import math

import numpy as np
import jax
import jax.numpy as jnp
from jax import lax
from jax.experimental import pallas as pl
from jax.experimental.pallas import tpu as pltpu

ROPE_THETA = 10000.0
NORM_EPS = 1e-6

DA_QK_DIM = 64
DA_V_DIM = 2 * DA_QK_DIM

NSA_HEADS = 16
NSA_GROUPS = 4
NSA_CMP_BLOCK = 32
NSA_CMP_STRIDE = 16
NSA_SLC_BLOCK = 64
NSA_SLC_TOPK = 16
NSA_WINDOW = 512

MLA_HEADS = 16
MLA_Q_RANK = 512
MLA_KV_RANK = 512
MLA_NOPE_DIM = 128
MLA_ROPE_DIM = 64
MLA_V_DIM = 128

SB_HEADS = 16

N_EXPERTS = 8
TOP_K = 2
MOE_ROW_BLOCK = 512

LANES = 128
HALF_LANES = LANES // 2
V7X_VMEM_BYTES = 64 * 1024 * 1024
VMEM_LIMIT = V7X_VMEM_BYTES * 7 // 8
NEG = -1e30
LOG2E = math.log2(math.e)
ONES_ROWS = 16

F32 = jnp.float32
BF16 = jnp.bfloat16


def _cparams(sem):
    return pltpu.CompilerParams(dimension_semantics=sem, vmem_limit_bytes=VMEM_LIMIT)


def _dot(a, b):
    return jnp.dot(a, b, preferred_element_type=F32)


def _dot_nt(a, b):
    return lax.dot_general(a, b, (((1,), (1,)), ((), ())), preferred_element_type=F32)


def _split_bf16(x):
    hi = x.astype(BF16)
    lo = (x - hi.astype(F32)).astype(BF16)
    return hi, lo


def _linear_kernel(*refs, has_norm, prologue, n_w, has_res, rope, act, transpose_out):
    it = iter(refs)
    x_ref = next(it)
    nw_ref = next(it) if has_norm else None
    w_refs = [next(it) for _ in range(n_w)]
    cos_ref = next(it) if rope else None
    sin_ref = next(it) if rope else None
    res_ref = next(it) if has_res else None
    o_ref = next(it)
    xs_ref = next(it) if prologue else None
    j = pl.program_id(1)

    if prologue:
        @pl.when(j == 0)
        def _():
            xf = x_ref[...].astype(F32)
            if has_norm:
                ms = jnp.mean(xf * xf, axis=-1, keepdims=True)
                xf = xf * lax.rsqrt(ms + NORM_EPS) * nw_ref[...]
            xs_ref[...] = xf.astype(BF16)
        xb = xs_ref[...]
    else:
        xb = x_ref[...]

    y = _dot(xb, w_refs[0][...])
    if act == "swiglu":
        u = _dot(xb, w_refs[1][...])
        y = y * jax.nn.sigmoid(y) * u
    elif act == "sigmoid":
        y = jax.nn.sigmoid(y)
    if has_res:
        y = y + res_ref[...]

    if rope:
        lo, hi = rope

        @pl.when((j >= lo) & (j < hi))
        def _():
            cos = cos_ref[...]
            sin = sin_ref[...]
            for c in range(y.shape[1] // LANES):
                yc = y[:, c * LANES:(c + 1) * LANES]
                yc = yc * cos + pltpu.roll(yc, HALF_LANES, 1) * sin
                o_ref[:, c * LANES:(c + 1) * LANES] = yc.astype(o_ref.dtype)

        @pl.when((j < lo) | (j >= hi))
        def _():
            o_ref[...] = y.astype(o_ref.dtype)
    elif transpose_out:
        o_ref[...] = y.T.astype(o_ref.dtype)
    else:
        o_ref[...] = y.astype(o_ref.dtype)


def _linear(x, w, *, n_out, k=None, x_col_block=0, w_tile_offsets=(0,), norm_w=None, residual=None,
            rope=None, act=None, out_dtype=None, tm=512, tn=512, transpose_out=False, name="linear"):
    m = x.shape[0]
    k = x.shape[1] if k is None else k
    tm = min(tm, m)
    tn = min(tn, n_out)
    assert m % tm == 0 and n_out % tn == 0 and w.shape[0] == k
    out_dtype = BF16 if out_dtype is None else out_dtype
    has_norm = norm_w is not None
    prologue = has_norm or x.dtype != BF16
    grid = (m // tm, n_out // tn)

    in_specs = [pl.BlockSpec((tm, k), lambda i, j: (i, x_col_block))]
    args = [x]
    if has_norm:
        in_specs.append(pl.BlockSpec((1, k), lambda i, j: (0, 0)))
        args.append(norm_w.reshape(1, k).astype(F32))
    for off in w_tile_offsets:
        in_specs.append(pl.BlockSpec((k, tn), lambda i, j, off=off: (0, j + off)))
        args.append(w)
    rope_range = None
    if rope is not None:
        cos, sin, lo, hi, seq = rope
        assert seq % tm == 0
        nrep = seq // tm
        for tab in (cos, sin):
            in_specs.append(pl.BlockSpec((tm, LANES), lambda i, j: (i % nrep, 0)))
            args.append(tab)
        rope_range = (lo, hi)
    if residual is not None:
        in_specs.append(pl.BlockSpec((tm, tn), lambda i, j: (i, j)))
        args.append(residual)
    scratch = [pltpu.VMEM((tm, k), BF16)] if prologue else []

    kern = functools.partial(_linear_kernel, has_norm=has_norm, prologue=prologue,
                             n_w=len(w_tile_offsets), has_res=residual is not None,
                             rope=rope_range, act=act, transpose_out=transpose_out)
    if transpose_out:
        assert rope is None
        out_specs = pl.BlockSpec((tn, tm), lambda i, j: (j, i))
        out_shape = jax.ShapeDtypeStruct((n_out, m), out_dtype)
    else:
        out_specs = pl.BlockSpec((tm, tn), lambda i, j: (i, j))
        out_shape = jax.ShapeDtypeStruct((m, n_out), out_dtype)
    return pl.pallas_call(
        kern,
        grid=grid,
        in_specs=in_specs,
        out_specs=out_specs,
        out_shape=out_shape,
        scratch_shapes=scratch,
        compiler_params=_cparams(("parallel", "arbitrary")),
        name=name,
    )(*args)


def _rmsnorm_kernel(x_ref, w_ref, o_ref):
    xf = x_ref[...]
    ms = jnp.mean(xf * xf, axis=-1, keepdims=True)
    o_ref[...] = xf * lax.rsqrt(ms + NORM_EPS) * w_ref[...]


def _rmsnorm(x, w, tm=512):
    m, d = x.shape
    tm = min(tm, m)
    return pl.pallas_call(
        _rmsnorm_kernel,
        grid=(m // tm,),
        in_specs=[pl.BlockSpec((tm, d), lambda i: (i, 0)), pl.BlockSpec((1, d), lambda i: (0, 0))],
        out_specs=pl.BlockSpec((tm, d), lambda i: (i, 0)),
        out_shape=jax.ShapeDtypeStruct((m, d), F32),
        compiler_params=_cparams(("parallel",)),
    )(x, w.reshape(1, d))


def _rope_tables(pos, half):
    inv_freq = ROPE_THETA ** (-np.arange(half, dtype=np.float64) / half)
    ang = np.asarray(pos, np.float64)[:, None] * inv_freq[None, :]
    reps = LANES // half
    cos = np.tile(np.cos(ang), (1, reps))
    sin = np.tile(np.sin(ang), (1, reps))
    sign = np.where(np.arange(LANES) < HALF_LANES, -1.0, 1.0)
    return jnp.asarray(cos, F32), jnp.asarray(sin * sign, F32)


def _tri_tables(nq, reverse=False):
    qi, ki = [], []
    for q in range(nq):
        for kk in (range(q, -1, -1) if reverse else range(q + 1)):
            qi.append(q)
            ki.append(kk)
    return jnp.asarray(qi, jnp.int32), jnp.asarray(ki, jnp.int32)


def _chain_init(m_sc, acc_sc):
    m_sc[...] = jnp.full_like(m_sc, NEG)
    acc_sc[...] = jnp.zeros_like(acc_sc)


def _with_ones(v_t):
    return jnp.concatenate([v_t, jnp.ones((ONES_ROWS, v_t.shape[1]), v_t.dtype)], axis=0)


def _chain_update(c, s, mask, queries_may_be_empty, v_t1, m_sc, acc_sc):
    if mask is not None:
        s = jnp.where(mask, s, NEG)
    m_prev = m_sc[c]
    m_new = jnp.maximum(m_prev, jnp.max(s, axis=0, keepdims=True))
    alpha = jnp.exp2(m_prev - m_new)
    p = jnp.exp2(s - m_new)
    if queries_may_be_empty:
        p = jnp.where(mask, p, 0.0)
    acc_sc[c] = alpha * acc_sc[c] + _dot(v_t1, p.astype(BF16))
    m_sc[c] = m_new


def _chain_result(c, acc_sc, guard_empty=False):
    acc = acc_sc[c]
    den = acc[LANES:LANES + 1]
    if guard_empty:
        den = jnp.where(den > 0.0, den, 1.0)
    return acc[:LANES] * (1.0 / den)


def _chain_scratch(n_chains, t):
    return [pltpu.VMEM((n_chains, 1, t), F32), pltpu.VMEM((n_chains, LANES + ONES_ROWS, t), F32)]


def _tile_causal(t):
    return lax.broadcasted_iota(jnp.int32, (t, t), 0) <= lax.broadcasted_iota(jnp.int32, (t, t), 1)


def _head_rows(ref, h):
    return ref[h * LANES:(h + 1) * LANES, :]


def _heads(ref, h):
    return ref[:, h * LANES:(h + 1) * LANES]


def _da_kernel(qt_ref, kt_ref, q_ref, k_ref, vt_ref, lam_ref, sub_ref, o_ref, m_sc, acc_sc,
               *, lam_init, t, hb):
    step = pl.program_id(2)
    qi = qt_ref[step]
    ki = kt_ref[step]

    @pl.when(ki == 0)
    def _():
        _chain_init(m_sc, acc_sc)

    lane = lax.broadcasted_iota(jnp.int32, (1, LANES), 1)
    first_map = (lane % HALF_LANES) < (HALF_LANES // 2)

    def update(mask):
        for h in range(hb):
            q = _heads(q_ref, h)
            k = _heads(k_ref, h)
            v_t1 = _with_ones(_head_rows(vt_ref, h))
            zero = jnp.zeros_like(q)
            for j, qm in enumerate((jnp.where(first_map, q, zero), jnp.where(first_map, zero, q))):
                _chain_update(2 * h + j, _dot_nt(k, qm), mask, False, v_t1, m_sc, acc_sc)

    @pl.when(ki < qi)
    def _():
        update(None)

    @pl.when(ki == qi)
    def _():
        update(_tile_causal(t))
        lam = lam_ref[...]
        lam_full = (jnp.exp(jnp.sum(lam[0:1] * lam[1:2], axis=-1, keepdims=True))
                    - jnp.exp(jnp.sum(lam[2:3] * lam[3:4], axis=-1, keepdims=True)) + lam_init)
        for h in range(hb):
            o = _chain_result(2 * h, acc_sc) - lam_full * _chain_result(2 * h + 1, acc_sc)
            ms = jnp.mean(o * o, axis=0, keepdims=True)
            o = o * lax.rsqrt(ms + NORM_EPS) * sub_ref[...]
            o_ref[:, h * LANES:(h + 1) * LANES] = (o * (1.0 - lam_init)).T.astype(o_ref.dtype)


def _da_column_order(heads):
    dk = DA_QK_DIM
    hd = dk // 2
    per = 4 * dk + DA_V_DIM
    q_idx, k_idx, v_idx = [], [], []
    for h in range(heads):
        base = h * per
        for out, off in ((q_idx, 0), (k_idx, 2 * dk)):
            a = base + off + np.arange(dk)
            b = base + off + dk + np.arange(dk)
            out += [a[:hd], b[:hd], a[hd:], b[hd:]]
        v_idx.append(base + 4 * dk + np.arange(DA_V_DIM))
    return np.concatenate(q_idx + k_idx + v_idx)


def _diff_attention(hres, norm_w, w_in, lam, subln_w, w_out, lam_init, batch, seq):
    m, d = hres.shape
    heads = w_in.shape[1] // (4 * DA_QK_DIM + DA_V_DIM)
    hw = heads * LANES
    col_scale = np.ones(3 * hw, np.float32)
    col_scale[:hw] = DA_QK_DIM ** -0.5 * LOG2E
    w_perm = (w_in[:, _da_column_order(heads)] * col_scale[None, :]).astype(BF16)
    cos, sin = _rope_tables(np.arange(seq), DA_QK_DIM // 2)
    tn = 512
    proj = _linear(hres, w_perm[:, :2 * hw], n_out=2 * hw, norm_w=norm_w, tn=tn,
                   rope=(cos, sin, 0, 2 * hw // tn, seq), name="da_qk_proj")
    v_t = _linear(hres, w_perm[:, 2 * hw:], n_out=hw, norm_w=norm_w, tn=tn, transpose_out=True,
                  name="da_vt_proj")

    t = min(512, seq)
    nq = seq // t
    qt, kt = _tri_tables(nq)
    hb = 4
    hg = heads // hb
    bw = hb * LANES
    kern = functools.partial(_da_kernel, lam_init=lam_init, t=t, hb=hb)
    attn = pl.pallas_call(
        kern,
        grid_spec=pltpu.PrefetchScalarGridSpec(
            num_scalar_prefetch=2,
            grid=(batch, hg, qt.shape[0]),
            in_specs=[
                pl.BlockSpec((t, bw), lambda b, h, s, qt, kt: (b * nq + qt[s], h)),
                pl.BlockSpec((t, bw), lambda b, h, s, qt, kt: (b * nq + kt[s], hg + h)),
                pl.BlockSpec((bw, t), lambda b, h, s, qt, kt: (h, b * nq + kt[s])),
                pl.BlockSpec(lam.shape, lambda b, h, s, qt, kt: (0, 0)),
                pl.BlockSpec((DA_V_DIM, 1), lambda b, h, s, qt, kt: (0, 0)),
            ],
            out_specs=pl.BlockSpec((t, bw), lambda b, h, s, qt, kt: (b * nq + qt[s], h)),
            scratch_shapes=_chain_scratch(2 * hb, t),
        ),
        out_shape=jax.ShapeDtypeStruct((m, hw), BF16),
        compiler_params=_cparams(("parallel", "parallel", "arbitrary")),
        name="da_attention",
    )(qt, kt, proj, proj, v_t, lam.astype(F32), subln_w.reshape(DA_V_DIM, 1).astype(F32))
    return _linear(attn, w_out.astype(BF16), n_out=d, residual=hres, out_dtype=F32, name="da_out_proj")


def _mla_kernel(qt_ref, kt_ref, qn_ref, qr_ref, kn_ref, kr_ref, vt_ref, o_ref, m_sc, acc_sc, *, t, hb):
    step = pl.program_id(2)
    qi = qt_ref[step]
    ki = kt_ref[step]

    @pl.when(ki == 0)
    def _():
        _chain_init(m_sc, acc_sc)

    kr = kr_ref[...].astype(BF16)

    def update(mask):
        for h in range(hb):
            q = jnp.concatenate([_heads(qn_ref, h), _heads(qr_ref, h)], axis=1)
            k = jnp.concatenate([_heads(kn_ref, h), kr], axis=1)
            _chain_update(h, _dot_nt(k, q), mask, False, _with_ones(_head_rows(vt_ref, h)), m_sc, acc_sc)

    @pl.when(ki < qi)
    def _():
        update(None)

    @pl.when(ki == qi)
    def _():
        update(_tile_causal(t))
        for h in range(hb):
            o_ref[:, h * LANES:(h + 1) * LANES] = _chain_result(h, acc_sc).T.astype(o_ref.dtype)


def _mla_q_columns(heads):
    per = MLA_NOPE_DIM + MLA_ROPE_DIM
    hd = MLA_ROPE_DIM // 2
    idx, keep = [], []
    for h in range(heads):
        idx.append(h * per + np.arange(MLA_NOPE_DIM))
        keep.append(np.ones(MLA_NOPE_DIM))
    for h in range(heads):
        s = h % 2
        g_idx = np.zeros(LANES, np.int64)
        g_keep = np.zeros(LANES)
        r0 = h * per + MLA_NOPE_DIM
        g_idx[s * hd:(s + 1) * hd] = r0 + np.arange(hd)
        g_idx[HALF_LANES + s * hd:HALF_LANES + (s + 1) * hd] = r0 + hd + np.arange(hd)
        g_keep[s * hd:(s + 1) * hd] = 1.0
        g_keep[HALF_LANES + s * hd:HALF_LANES + (s + 1) * hd] = 1.0
        idx.append(g_idx)
        keep.append(g_keep)
    return np.concatenate(idx), np.concatenate(keep)


def _mla_attention(hres, norm_w, w_in, q_norm_w, w_uq, kv_norm_w, w_ukv, w_out, batch, seq):
    m, d = hres.shape
    heads = MLA_HEADS
    hw = heads * LANES
    hd = MLA_ROPE_DIM // 2
    assert MLA_Q_RANK == MLA_KV_RANK and MLA_NOPE_DIM == LANES and MLA_V_DIM == LANES
    rank = MLA_Q_RANK
    r0 = 2 * rank
    kr_cols = np.concatenate([r0 + np.arange(hd), r0 + np.arange(hd),
                              r0 + hd + np.arange(hd), r0 + hd + np.arange(hd)])
    w_in_x = jnp.concatenate([w_in[:, :2 * rank], w_in[:, kr_cols]], axis=1).astype(BF16)
    cos, sin = _rope_tables(np.arange(seq), hd)
    n_in = 2 * rank + LANES
    proj = _linear(hres, w_in_x, n_out=n_in, norm_w=norm_w, tn=LANES, out_dtype=F32,
                   rope=(cos, sin, 2 * rank // LANES, n_in // LANES, seq), name="mla_in_proj")

    q_idx, q_keep = _mla_q_columns(heads)
    c2 = (MLA_NOPE_DIM + MLA_ROPE_DIM) ** -0.5 * LOG2E
    w_uq_x = (w_uq[:, q_idx] * jnp.asarray(q_keep * c2, F32)[None, :]).astype(BF16)
    tn = 512
    qcat = _linear(proj, w_uq_x, n_out=2 * hw, k=rank, x_col_block=0, norm_w=q_norm_w, tn=tn,
                   rope=(cos, sin, hw // tn, 2 * hw // tn, seq), name="mla_q_proj")
    k_nope = _linear(proj, w_ukv[:, :, :MLA_NOPE_DIM].reshape(rank, hw).astype(BF16), n_out=hw, k=rank,
                     x_col_block=1, norm_w=kv_norm_w, tn=tn, name="mla_k_proj")
    v_t = _linear(proj, w_ukv[:, :, MLA_NOPE_DIM:].reshape(rank, hw).astype(BF16), n_out=hw, k=rank,
                  x_col_block=1, norm_w=kv_norm_w, tn=tn, transpose_out=True, name="mla_vt_proj")

    t = min(512, seq)
    nq = seq // t
    qt, kt = _tri_tables(nq)
    kr_block = 2 * rank // LANES
    hb = min(8, heads)
    hg = heads // hb
    bw = hb * LANES
    kern = functools.partial(_mla_kernel, t=t, hb=hb)
    attn = pl.pallas_call(
        kern,
        grid_spec=pltpu.PrefetchScalarGridSpec(
            num_scalar_prefetch=2,
            grid=(batch, hg, qt.shape[0]),
            in_specs=[
                pl.BlockSpec((t, bw), lambda b, h, s, qt, kt: (b * nq + qt[s], h)),
                pl.BlockSpec((t, bw), lambda b, h, s, qt, kt: (b * nq + qt[s], hg + h)),
                pl.BlockSpec((t, bw), lambda b, h, s, qt, kt: (b * nq + kt[s], h)),
                pl.BlockSpec((t, LANES), lambda b, h, s, qt, kt: (b * nq + kt[s], kr_block)),
                pl.BlockSpec((bw, t), lambda b, h, s, qt, kt: (h, b * nq + kt[s])),
            ],
            out_specs=pl.BlockSpec((t, bw), lambda b, h, s, qt, kt: (b * nq + qt[s], h)),
            scratch_shapes=_chain_scratch(hb, t),
        ),
        out_shape=jax.ShapeDtypeStruct((m, hw), BF16),
        compiler_params=_cparams(("parallel", "parallel", "arbitrary")),
        name="mla_attention",
    )(qt, kt, qcat, qcat, k_nope, proj, v_t)
    return _linear(attn, w_out.astype(BF16), n_out=d, residual=hres, out_dtype=F32, name="mla_out_proj")


def _sb_kernel(qt_ref, kt_ref, q_ref, k_ref, v_ref, tri_ref, o_ref, carry_sc, acc_sc, *, t, hb):
    step = pl.program_id(2)
    qi = qt_ref[step]
    ki = kt_ref[step]

    @pl.when(ki == qi)
    def _():
        carry_sc[...] = jnp.zeros_like(carry_sc)
        acc_sc[...] = jnp.zeros_like(acc_sc)

    tri = tri_ref[...]
    chunk = tri.shape[0]

    def update(masked):
        if masked:
            mask = lax.broadcasted_iota(jnp.int32, (t, t), 1) < lax.broadcasted_iota(jnp.int32, (t, t), 0)
        for h in range(hb):
            z = _dot_nt(_heads(q_ref, h), _heads(k_ref, h))
            log_beta = jnp.minimum(z, 0.0) - jnp.log2(1.0 + jnp.exp2(-jnp.abs(z)))
            log_rest = log_beta - z
            if masked:
                log_rest = jnp.where(mask, log_rest, 0.0)
            carry = carry_sc[h]
            parts = []
            for c in range(t // chunk - 1, -1, -1):
                cols = slice(c * chunk, (c + 1) * chunk)
                hi, lo = _split_bf16(log_rest[:, cols])
                tail = _dot(hi, tri) + _dot(lo, tri) + carry
                parts.append(jnp.exp2(log_beta[:, cols] + tail))
                carry = carry + jnp.sum(log_rest[:, cols], axis=-1, keepdims=True)
            a = jnp.concatenate(parts[::-1], axis=1)
            if masked:
                a = jnp.where(mask, a, 0.0)
            acc_sc[h] += _dot(a.astype(BF16), _heads(v_ref, h))
            carry_sc[h] = carry

    @pl.when(ki == qi)
    def _():
        update(True)

    @pl.when(ki < qi)
    def _():
        update(False)

    @pl.when(ki == 0)
    def _():
        for h in range(hb):
            o_ref[:, h * LANES:(h + 1) * LANES] = acc_sc[h].astype(o_ref.dtype)


def _sb_attention(hres, norm_w, w_in, w_out, batch, seq):
    m, d = hres.shape
    heads = SB_HEADS
    hw = heads * LANES
    assert w_in.shape[1] == 3 * hw
    col_scale = np.ones(3 * hw, np.float32)
    col_scale[:hw] = LANES ** -0.5 * LOG2E
    qkv = _linear(hres, (w_in * col_scale[None, :]).astype(BF16), n_out=3 * hw, norm_w=norm_w,
                  name="sb_qkv_proj")
    t = min(512, seq)
    nq = seq // t
    qt, kt = _tri_tables(nq, reverse=True)
    chunk = min(2 * LANES, t)
    tri = jnp.asarray(np.arange(chunk)[:, None] > np.arange(chunk)[None, :], BF16)
    hb = 4
    hg = heads // hb
    bw = hb * LANES
    kern = functools.partial(_sb_kernel, t=t, hb=hb)
    attn = pl.pallas_call(
        kern,
        grid_spec=pltpu.PrefetchScalarGridSpec(
            num_scalar_prefetch=2,
            grid=(batch, hg, qt.shape[0]),
            in_specs=[
                pl.BlockSpec((t, bw), lambda b, h, s, qt, kt: (b * nq + qt[s], h)),
                pl.BlockSpec((t, bw), lambda b, h, s, qt, kt: (b * nq + kt[s], hg + h)),
                pl.BlockSpec((t, bw), lambda b, h, s, qt, kt: (b * nq + kt[s], 2 * hg + h)),
                pl.BlockSpec((chunk, chunk), lambda b, h, s, qt, kt: (0, 0)),
            ],
            out_specs=pl.BlockSpec((t, bw), lambda b, h, s, qt, kt: (b * nq + qt[s], h)),
            scratch_shapes=[pltpu.VMEM((hb, t, 1), F32), pltpu.VMEM((hb, t, LANES), F32)],
        ),
        out_shape=jax.ShapeDtypeStruct((m, hw), BF16),
        compiler_params=_cparams(("parallel", "parallel", "arbitrary")),
        name="sb_attention",
    )(qt, kt, qkv, qkv, qkv, tri)
    return _linear(attn, w_out.astype(BF16), n_out=d, residual=hres, out_dtype=F32, name="sb_out_proj")


def _nsa_compress_kernel(x_ref, w_ref, pe_ref, cos_ref, sin_ref, o_ref, *, rope):
    x = x_ref[0]
    half = w_ref.shape[0] // 2
    y_lo = _dot(x, w_ref[:half, :])
    y_hi = _dot(x, w_ref[half:, :])
    n = y_hi.shape[0]
    y = y_lo + pltpu.roll(y_hi, n - 1, 0) + _dot(pe_ref[...], w_ref[...])[0:1]
    if rope:
        y = y * cos_ref[...] + pltpu.roll(y, HALF_LANES, 1) * sin_ref[...]
    o_ref[0] = y.astype(o_ref.dtype)


def _nsa_compress(x_chunks, w, pe, cos, sin, rope):
    bg, n_chunk, kdim = x_chunks.shape
    dh = w.shape[1]
    return pl.pallas_call(
        functools.partial(_nsa_compress_kernel, rope=rope),
        grid=(bg,),
        in_specs=[pl.BlockSpec((1, n_chunk, kdim), lambda i: (i, 0, 0)),
                  pl.BlockSpec(w.shape, lambda i: (0, 0)),
                  pl.BlockSpec(pe.shape, lambda i: (0, 0)),
                  pl.BlockSpec(cos.shape, lambda i: (0, 0)),
                  pl.BlockSpec(sin.shape, lambda i: (0, 0))],
        out_specs=pl.BlockSpec((1, n_chunk, dh), lambda i: (i, 0, 0)),
        out_shape=jax.ShapeDtypeStruct((bg, n_chunk, dh), BF16),
        compiler_params=_cparams(("parallel",)),
    )(x_chunks, w, pe, cos, sin)


def _head_gate(gates, branch, head):
    lane = lax.broadcasted_iota(jnp.int32, gates.shape, 1)
    return jnp.sum(jnp.where(lane == branch * NSA_HEADS + head, gates, 0.0), axis=-1, keepdims=True)


def _nsa_cmp_kernel(q_ref, kc_ref, vc_ref, ovt_ref, g_ref, o_ref, sel_ref, *, tq, rep, n_sel):
    qi = pl.program_id(2)
    grp = pl.program_id(1)
    qs = jnp.concatenate([q_ref[:, r * LANES:(r + 1) * LANES] for r in range(rep)], axis=0)
    kc = kc_ref[0]
    vc = vc_ref[0]
    n_cmp = kc.shape[0]
    rows = rep * tq

    s = _dot_nt(qs, kc)
    qpos = qi * tq + lax.broadcasted_iota(jnp.int32, (rep, tq, n_cmp), 1).reshape(rows, n_cmp)
    cend = lax.broadcasted_iota(jnp.int32, (rows, n_cmp), 1) * NSA_CMP_STRIDE + (NSA_CMP_BLOCK - 1)
    mask = cend <= qpos
    s = jnp.where(mask, s, NEG)
    p = jnp.where(mask, jnp.exp2(s - jnp.max(s, axis=-1, keepdims=True)), 0.0)
    den = jnp.sum(p, axis=-1, keepdims=True)
    p = p / jnp.where(den > 0.0, den, 1.0)
    o = _dot(p.astype(BF16), vc)
    gates = g_ref[...]
    for r in range(rep):
        gate = _head_gate(gates, 0, grp * rep + r)
        o_ref[:, r * LANES:(r + 1) * LANES] = gate * o[r * tq:(r + 1) * tq, :]

    st = _dot_nt(kc, qs)
    qpos_t = qi * tq + lax.broadcasted_iota(jnp.int32, (n_cmp, rep, tq), 2).reshape(n_cmp, rows)
    cend_t = lax.broadcasted_iota(jnp.int32, (n_cmp, rows), 0) * NSA_CMP_STRIDE + (NSA_CMP_BLOCK - 1)
    mask_t = cend_t <= qpos_t
    st = jnp.where(mask_t, st, NEG)
    pt = jnp.where(mask_t, jnp.exp2(st - jnp.max(st, axis=0, keepdims=True)), 0.0)
    den_t = jnp.sum(pt, axis=0, keepdims=True)
    pt = pt / jnp.where(den_t > 0.0, den_t, 1.0)
    psum = pt[:, 0:tq]
    for r in range(1, rep):
        psum = psum + pt[:, r * tq:(r + 1) * tq]
    hi, lo = _split_bf16(psum)
    ovt = ovt_ref[...]
    imp = _dot(ovt, hi) + _dot(ovt, lo)

    n_pad = imp.shape[0]
    blk = lax.broadcasted_iota(jnp.int32, (n_pad, tq), 0)
    cur = (qi * tq + lax.broadcasted_iota(jnp.int32, (n_pad, tq), 1)) // NSA_SLC_BLOCK
    forced = (blk == 0) | (blk == cur) | (blk == cur - 1)
    val = jnp.where(forced, jnp.inf, jnp.where(blk <= cur, imp, -jnp.inf))
    sel = jnp.zeros((n_pad, tq), F32)
    for _ in range(n_sel):
        top = jnp.max(val, axis=0, keepdims=True)
        first = jnp.min(jnp.where(val == top, blk, n_pad), axis=0, keepdims=True)
        pick = blk == first
        sel = jnp.where(pick, 1.0, sel)
        val = jnp.where(pick, -jnp.inf, val)
    sel_ref[...] = sel.astype(sel_ref.dtype)


def _nsa_win_kernel(q_ref, k_ref, vt_ref, g_ref, prev_ref, o_ref, m_sc, acc_sc, *, t, rep, nwin):
    qi = pl.program_id(2)
    kk = pl.program_id(3)
    grp = pl.program_id(1)
    kb = qi - (nwin - 1) + kk

    @pl.when(kk == 0)
    def _():
        _chain_init(m_sc, acc_sc)

    def update(mask):
        k = k_ref[...]
        v_t1 = _with_ones(vt_ref[...])
        for r in range(rep):
            _chain_update(r, _dot_nt(k, _heads(q_ref, r)), mask, mask is not None, v_t1, m_sc, acc_sc)

    @pl.when((kk == 0) & (kb >= 0))
    def _():
        update(jnp.logical_not(_tile_causal(t)))

    @pl.when((kk > 0) & (kk < nwin - 1) & (kb >= 0))
    def _():
        update(None)

    @pl.when(kk == nwin - 1)
    def _():
        update(_tile_causal(t))
        gates = g_ref[...]
        for r in range(rep):
            gate = _head_gate(gates, 2, grp * rep + r)
            o_ref[:, r * LANES:(r + 1) * LANES] = _heads(prev_ref, r) + gate * _chain_result(r, acc_sc).T


def _nsa_slc_kernel(qt_ref, kt_ref, q_ref, k_ref, vt_ref, sel_ref, g_ref, prev_ref, o_ref,
                    m_sc, acc_sc, *, t, rep):
    step = pl.program_id(2)
    grp = pl.program_id(1)
    qi = qt_ref[step]
    ki = kt_ref[step]

    @pl.when(ki == 0)
    def _():
        _chain_init(m_sc, acc_sc)

    n_pad = sel_ref.shape[0]
    blk = lax.broadcasted_iota(jnp.int32, (t, n_pad), 1)
    kblk = (ki * t + lax.broadcasted_iota(jnp.int32, (t, n_pad), 0)) // NSA_SLC_BLOCK
    expand = jnp.where(blk == kblk, 1.0, 0.0).astype(BF16)
    chosen = _dot(expand, sel_ref[...]) > 0.5

    def update(mask):
        k = k_ref[...]
        v_t1 = _with_ones(vt_ref[...])
        for r in range(rep):
            _chain_update(r, _dot_nt(k, _heads(q_ref, r)), mask, True, v_t1, m_sc, acc_sc)

    @pl.when(ki < qi)
    def _():
        update(chosen)

    @pl.when(ki == qi)
    def _():
        update(chosen & _tile_causal(t))
        gates = g_ref[...]
        for r in range(rep):
            gate = _head_gate(gates, 1, grp * rep + r)
            o = _chain_result(r, acc_sc, guard_empty=True).T
            o_ref[:, r * LANES:(r + 1) * LANES] = (_heads(prev_ref, r) + gate * o).astype(o_ref.dtype)


def _nsa_attention(hres, norm_w, w_in, cmp_pe, cmp_w, w_out, batch, seq):
    m, d = hres.shape
    heads, groups = NSA_HEADS, NSA_GROUPS
    rep = heads // groups
    dh = d // heads
    assert dh == LANES
    kvw = groups * dh
    n_main = heads * dh + 6 * kvw
    gw = rep * LANES

    hq = heads * dh
    cos, sin = _rope_tables(np.arange(seq), dh // 2)
    kv_cols = lambda i: np.arange(hq + i * kvw, hq + (i + 1) * kvw)
    main_cols = np.concatenate([np.arange(hq), kv_cols(0), kv_cols(1), kv_cols(2), kv_cols(4)])
    col_scale = np.ones(main_cols.size, np.float32)
    col_scale[:hq] = dh ** -0.5 * LOG2E
    proj = _linear(hres, (w_in[:, main_cols] * col_scale[None, :]).astype(BF16), n_out=main_cols.size,
                   norm_w=norm_w, rope=(cos, sin, 0, hq // 512, seq), name="nsa_qk_proj")
    v_t = _linear(hres, w_in[:, np.concatenate([kv_cols(3), kv_cols(5)])].astype(BF16), n_out=2 * kvw,
                  norm_w=norm_w, transpose_out=True, name="nsa_vt_proj")
    w_gate = jnp.pad(w_in[:, n_main:], ((0, 0), (0, LANES - 3 * heads))).astype(BF16)
    gates = _linear(hres, w_gate, n_out=LANES, norm_w=norm_w, act="sigmoid", out_dtype=F32,
                    name="nsa_gate_proj")
    k_slc_blk = (hq + 2 * kvw) // LANES
    k_win_blk = (hq + 3 * kvw) // LANES

    n_chunk = seq // NSA_CMP_STRIDE
    assert NSA_CMP_BLOCK == 2 * NSA_CMP_STRIDE
    cmp_end = np.arange(n_chunk) * NSA_CMP_STRIDE + NSA_CMP_BLOCK - 1
    ccos, csin = _rope_tables(cmp_end, dh // 2)

    def chunks(i):
        lo = hq + i * kvw
        x = proj[:, lo:lo + kvw].reshape(batch, n_chunk, NSA_CMP_STRIDE, groups, dh)
        return x.transpose(0, 3, 1, 2, 4).reshape(batch * groups, n_chunk, NSA_CMP_STRIDE * dh)

    def cmp_weights(i):
        return (cmp_w[i].reshape(NSA_CMP_BLOCK * dh, dh).astype(BF16),
                jnp.broadcast_to(cmp_pe[i].reshape(1, NSA_CMP_BLOCK * dh), (8, NSA_CMP_BLOCK * dh)).astype(BF16))

    k_cmp = _nsa_compress(chunks(0), *cmp_weights(0), ccos, csin, True)
    v_cmp = _nsa_compress(chunks(1), *cmp_weights(1), ccos, csin, False)

    n_slc = seq // NSA_SLC_BLOCK
    n_sel = min(NSA_SLC_TOPK, n_slc)
    n_pad = max(LANES, n_slc)
    n_cmp = n_chunk - NSA_CMP_BLOCK // NSA_CMP_STRIDE + 1
    cs = np.arange(n_chunk)[None, :] * NSA_CMP_STRIDE
    js = np.arange(n_pad)[:, None] * NSA_SLC_BLOCK
    ovt = ((cs < js + NSA_SLC_BLOCK) & (cs + NSA_CMP_BLOCK > js)
           & (np.arange(n_chunk)[None, :] < n_cmp) & (np.arange(n_pad)[:, None] < n_slc))
    ovt = jnp.asarray(ovt, BF16)

    tq = min(256, seq)
    nqc = seq // tq
    o_c, sel = pl.pallas_call(
        functools.partial(_nsa_cmp_kernel, tq=tq, rep=rep, n_sel=n_sel),
        grid=(batch, groups, nqc),
        in_specs=[
            pl.BlockSpec((tq, gw), lambda b, g, i: (b * nqc + i, g)),
            pl.BlockSpec((1, n_chunk, dh), lambda b, g, i: (b * groups + g, 0, 0)),
            pl.BlockSpec((1, n_chunk, dh), lambda b, g, i: (b * groups + g, 0, 0)),
            pl.BlockSpec(ovt.shape, lambda b, g, i: (0, 0)),
            pl.BlockSpec((tq, LANES), lambda b, g, i: (b * nqc + i, 0)),
        ],
        out_specs=[
            pl.BlockSpec((tq, gw), lambda b, g, i: (b * nqc + i, g)),
            pl.BlockSpec((n_pad, tq), lambda b, g, i: (b * groups + g, i)),
        ],
        out_shape=[jax.ShapeDtypeStruct((m, heads * dh), F32),
                   jax.ShapeDtypeStruct((batch * groups * n_pad, seq), BF16)],
        compiler_params=_cparams(("parallel", "parallel", "arbitrary")),
        name="nsa_compressed_select",
    )(proj, k_cmp, v_cmp, ovt, gates)

    t = min(512, seq)
    nq = seq // t
    assert NSA_WINDOW % t == 0
    nwin = NSA_WINDOW // t + 1
    key_tile = lambda i, kk: jnp.maximum(i - (nwin - 1) + kk, 0)
    o_cw = pl.pallas_call(
        functools.partial(_nsa_win_kernel, t=t, rep=rep, nwin=nwin),
        grid=(batch, groups, nq, nwin),
        in_specs=[
            pl.BlockSpec((t, gw), lambda b, g, i, kk: (b * nq + i, g)),
            pl.BlockSpec((t, LANES), lambda b, g, i, kk: (b * nq + key_tile(i, kk), k_win_blk + g)),
            pl.BlockSpec((LANES, t), lambda b, g, i, kk: (groups + g, b * nq + key_tile(i, kk))),
            pl.BlockSpec((t, LANES), lambda b, g, i, kk: (b * nq + i, 0)),
            pl.BlockSpec((t, gw), lambda b, g, i, kk: (b * nq + i, g)),
        ],
        out_specs=pl.BlockSpec((t, gw), lambda b, g, i, kk: (b * nq + i, g)),
        out_shape=jax.ShapeDtypeStruct((m, heads * dh), F32),
        scratch_shapes=_chain_scratch(rep, t),
        compiler_params=_cparams(("parallel", "parallel", "parallel", "arbitrary")),
        name="nsa_window",
    )(proj, proj, v_t, gates, o_c)

    qt, kt = _tri_tables(nq)
    attn = pl.pallas_call(
        functools.partial(_nsa_slc_kernel, t=t, rep=rep),
        grid_spec=pltpu.PrefetchScalarGridSpec(
            num_scalar_prefetch=2,
            grid=(batch, groups, qt.shape[0]),
            in_specs=[
                pl.BlockSpec((t, gw), lambda b, g, s, qt, kt: (b * nq + qt[s], g)),
                pl.BlockSpec((t, LANES), lambda b, g, s, qt, kt: (b * nq + kt[s], k_slc_blk + g)),
                pl.BlockSpec((LANES, t), lambda b, g, s, qt, kt: (g, b * nq + kt[s])),
                pl.BlockSpec((n_pad, t), lambda b, g, s, qt, kt: (b * groups + g, qt[s])),
                pl.BlockSpec((t, LANES), lambda b, g, s, qt, kt: (b * nq + qt[s], 0)),
                pl.BlockSpec((t, gw), lambda b, g, s, qt, kt: (b * nq + qt[s], g)),
            ],
            out_specs=pl.BlockSpec((t, gw), lambda b, g, s, qt, kt: (b * nq + qt[s], g)),
            scratch_shapes=_chain_scratch(rep, t),
        ),
        out_shape=jax.ShapeDtypeStruct((m, heads * dh), BF16),
        compiler_params=_cparams(("parallel", "parallel", "arbitrary")),
        name="nsa_selected",
    )(qt, kt, proj, proj, v_t, sel, gates, o_cw)
    return _linear(attn, w_out.astype(BF16), n_out=d, residual=hres, out_dtype=F32, name="nsa_out_proj")


def _swiglu(hres, norm_w, w_gu, w_down):
    d = hres.shape[1]
    ff = w_down.shape[0]
    tn = 512
    hidden = _linear(hres, w_gu.astype(BF16), n_out=ff, w_tile_offsets=(0, ff // tn), norm_w=norm_w,
                     act="swiglu", tn=tn, name="ffn_gate_up")
    return _linear(hidden, w_down.astype(BF16), n_out=d, residual=hres, out_dtype=F32, name="ffn_down")


def _router_kernel(x_ref, nw_ref, w_ref, b_ref, f_ref, idx_ref, gate_ref, *, n_experts):
    xf = x_ref[...]
    ms = jnp.mean(xf * xf, axis=-1, keepdims=True)
    f = xf * lax.rsqrt(ms + NORM_EPS) * nw_ref[...]
    f_ref[...] = f
    fh, fl = _split_bf16(f)
    w = w_ref[...]
    wh, wl = _split_bf16(w)
    logits = _dot(fh, wh) + _dot(fl, wh) + _dot(fh, wl) + b_ref[...]
    lane = lax.broadcasted_iota(jnp.int32, logits.shape, 1)
    logits = jnp.where(lane < n_experts, logits, -jnp.inf)
    v1 = jnp.max(logits, axis=-1, keepdims=True)
    i1 = jnp.min(jnp.where(logits == v1, lane, LANES), axis=-1, keepdims=True)
    rest = jnp.where(lane == i1, -jnp.inf, logits)
    v2 = jnp.max(rest, axis=-1, keepdims=True)
    i2 = jnp.min(jnp.where(rest == v2, lane, LANES), axis=-1, keepdims=True)
    e2 = jnp.exp(v2 - v1)
    g1 = 1.0 / (1.0 + e2)
    g2 = e2 / (1.0 + e2)
    idx_ref[...] = jnp.where(lane == 0, i1, jnp.where(lane == 1, i2, 0))
    gate_ref[...] = jnp.where(lane == 0, g1, jnp.where(lane == 1, g2, 0.0))


def _gather_kernel(src_ref, nvalid_ref, x_hbm, o_ref, buf, sem):
    blk = pl.program_id(0)
    rows = buf.shape[0]
    nvalid = nvalid_ref[blk]

    @pl.when(blk == 0)
    def _():
        buf[...] = jnp.zeros_like(buf)

    def row_copy(r):
        return pltpu.make_async_copy(x_hbm.at[pl.ds(src_ref[blk * rows + r], 1), :],
                                     buf.at[pl.ds(r, 1), :], sem)

    def start(r, c):
        row_copy(r).start()
        return c

    def wait(r, c):
        row_copy(r).wait()
        return c

    lax.fori_loop(0, nvalid, start, 0)
    lax.fori_loop(0, nvalid, wait, 0)
    rid = lax.broadcasted_iota(jnp.int32, (rows, 1), 0)
    o_ref[...] = jnp.where(rid < nvalid, buf[...], 0.0).astype(o_ref.dtype)


def _weight_tile_is_new(be_ref, blk):
    return (blk == 0) | (be_ref[blk] != be_ref[jnp.maximum(blk - 1, 0)])


def _expert_up_kernel(be_ref, nv_ref, x_ref, wg_ref, wu_ref, o_ref, wg_sc, wu_sc):
    blk = pl.program_id(1)

    @pl.when(_weight_tile_is_new(be_ref, blk))
    def _():
        wg_sc[...] = wg_ref[0, 0].astype(BF16)
        wu_sc[...] = wu_ref[0, 0].astype(BF16)

    @pl.when(nv_ref[blk] > 0)
    def _():
        x = x_ref[...]
        g = _dot(x, wg_sc[...])
        u = _dot(x, wu_sc[...])
        o_ref[...] = (g * jax.nn.sigmoid(g) * u).astype(o_ref.dtype)

    @pl.when(nv_ref[blk] == 0)
    def _():
        o_ref[...] = jnp.zeros_like(o_ref)


def _expert_down_kernel(be_ref, nv_ref, h_ref, w_ref, o_ref, w_sc):
    blk = pl.program_id(1)

    @pl.when(_weight_tile_is_new(be_ref, blk))
    def _():
        w_sc[...] = w_ref[0, 0].astype(BF16)

    @pl.when(nv_ref[blk] > 0)
    def _():
        o_ref[...] = _dot(h_ref[...], w_sc[...])

    @pl.when(nv_ref[blk] == 0)
    def _():
        o_ref[...] = jnp.zeros_like(o_ref)


def _combine_kernel(dst_ref, y_hbm, res_ref, gate_ref, o_ref, buf0, buf1, sem):
    i = pl.program_id(0)
    rows = buf0.shape[0]

    def row_copy(r, slot, buf):
        return pltpu.make_async_copy(y_hbm.at[pl.ds(dst_ref[(i * rows + r) * TOP_K + slot], 1), :],
                                     buf.at[pl.ds(r, 1), :], sem.at[slot])

    def start(r, c):
        row_copy(r, 0, buf0).start()
        row_copy(r, 1, buf1).start()
        return c

    def wait(r, c):
        row_copy(r, 0, buf0).wait()
        row_copy(r, 1, buf1).wait()
        return c

    lax.fori_loop(0, rows, start, 0)
    lax.fori_loop(0, rows, wait, 0)
    gate = gate_ref[...]
    o_ref[...] = res_ref[...] + (buf0[...] * gate[:, 0:1] + buf1[...] * gate[:, 1:2])


def _moe(hres, norm_w, w_router, b_router, w_gu, w_down, layer):
    m, d = hres.shape
    n_exp = w_router.shape[1]
    ff = w_down.shape[2]
    rb = MOE_ROW_BLOCK
    assert TOP_K == 2 and m % rb == 0
    tm = min(512, m)

    w_r = jnp.pad(w_router, ((0, 0), (0, LANES - n_exp))).astype(F32)
    b_r = jnp.pad(b_router, (0, LANES - n_exp)).reshape(1, LANES).astype(F32)
    f, idx, gate = pl.pallas_call(
        functools.partial(_router_kernel, n_experts=n_exp),
        grid=(m // tm,),
        in_specs=[pl.BlockSpec((tm, d), lambda i: (i, 0)),
                  pl.BlockSpec((1, d), lambda i: (0, 0)),
                  pl.BlockSpec((d, LANES), lambda i: (0, 0)),
                  pl.BlockSpec((1, LANES), lambda i: (0, 0))],
        out_specs=[pl.BlockSpec((tm, d), lambda i: (i, 0)),
                   pl.BlockSpec((tm, LANES), lambda i: (i, 0)),
                   pl.BlockSpec((tm, LANES), lambda i: (i, 0))],
        out_shape=[jax.ShapeDtypeStruct((m, d), F32),
                   jax.ShapeDtypeStruct((m, LANES), jnp.int32),
                   jax.ShapeDtypeStruct((m, LANES), F32)],
        compiler_params=_cparams(("parallel",)),
        name="moe_router",
    )(hres, norm_w.reshape(1, d).astype(F32), w_r, b_r)

    nk = m * TOP_K
    e_flat = idx[:, :TOP_K].reshape(nk)
    onehot = (e_flat[:, None] == jnp.arange(n_exp, dtype=jnp.int32)[None, :]).astype(jnp.int32)
    csum = jnp.cumsum(onehot, axis=0)
    rank = jnp.sum((csum - onehot) * onehot, axis=1)
    counts = csum[-1]
    padded = (counts + rb - 1) // rb * rb
    pend = jnp.cumsum(padded)
    pstart = pend - padded
    dest = (pstart[e_flat] + rank).astype(jnp.int32)
    n_rows = (-(-nk // rb)) * rb + n_exp * rb
    n_blocks = n_rows // rb
    src_tok = jnp.zeros((n_rows,), jnp.int32).at[dest].set(jnp.arange(nk, dtype=jnp.int32) // TOP_K)
    blk_start = jnp.arange(n_blocks, dtype=jnp.int32) * rb
    blk_expert = jnp.minimum(jnp.searchsorted(pend, blk_start, side="right"), n_exp - 1).astype(jnp.int32)
    nvalid = jnp.clip(pstart[blk_expert] + counts[blk_expert] - blk_start, 0, rb).astype(jnp.int32)

    xbuf = pl.pallas_call(
        _gather_kernel,
        grid_spec=pltpu.PrefetchScalarGridSpec(
            num_scalar_prefetch=2,
            grid=(n_blocks,),
            in_specs=[pl.BlockSpec(memory_space=pl.ANY)],
            out_specs=pl.BlockSpec((rb, d), lambda i, src, nv: (i, 0)),
            scratch_shapes=[pltpu.VMEM((rb, d), F32), pltpu.SemaphoreType.DMA(())],
        ),
        out_shape=jax.ShapeDtypeStruct((n_rows, d), BF16),
        compiler_params=_cparams(("arbitrary",)),
        name="moe_gather",
    )(src_tok, nvalid, f)

    tn = 512
    hidden = pl.pallas_call(
        _expert_up_kernel,
        grid_spec=pltpu.PrefetchScalarGridSpec(
            num_scalar_prefetch=2,
            grid=(ff // tn, n_blocks),
            in_specs=[pl.BlockSpec((rb, d), lambda j, i, be, nv: (i, 0)),
                      pl.BlockSpec((1, 1, d, tn), lambda j, i, be, nv: (layer, be[i], 0, j)),
                      pl.BlockSpec((1, 1, d, tn), lambda j, i, be, nv: (layer, be[i], 0, j + ff // tn))],
            out_specs=pl.BlockSpec((rb, tn), lambda j, i, be, nv: (i, j)),
            scratch_shapes=[pltpu.VMEM((d, tn), BF16), pltpu.VMEM((d, tn), BF16)],
        ),
        out_shape=jax.ShapeDtypeStruct((n_rows, ff), BF16),
        compiler_params=_cparams(("arbitrary", "arbitrary")),
        name="moe_expert_up",
    )(blk_expert, nvalid, xbuf, w_gu, w_gu)

    tnd = min(512, d)
    ybuf = pl.pallas_call(
        _expert_down_kernel,
        grid_spec=pltpu.PrefetchScalarGridSpec(
            num_scalar_prefetch=2,
            grid=(d // tnd, n_blocks),
            in_specs=[pl.BlockSpec((rb, ff), lambda j, i, be, nv: (i, 0)),
                      pl.BlockSpec((1, 1, ff, tnd), lambda j, i, be, nv: (layer, be[i], 0, j))],
            out_specs=pl.BlockSpec((rb, tnd), lambda j, i, be, nv: (i, j)),
            scratch_shapes=[pltpu.VMEM((ff, tnd), BF16)],
        ),
        out_shape=jax.ShapeDtypeStruct((n_rows, d), F32),
        compiler_params=_cparams(("arbitrary", "arbitrary")),
        name="moe_expert_down",
    )(blk_expert, nvalid, hidden, w_down)

    return pl.pallas_call(
        _combine_kernel,
        grid_spec=pltpu.PrefetchScalarGridSpec(
            num_scalar_prefetch=1,
            grid=(m // tm,),
            in_specs=[pl.BlockSpec(memory_space=pl.ANY),
                      pl.BlockSpec((tm, d), lambda i, dst: (i, 0)),
                      pl.BlockSpec((tm, LANES), lambda i, dst: (i, 0))],
            out_specs=pl.BlockSpec((tm, d), lambda i, dst: (i, 0)),
            scratch_shapes=[pltpu.VMEM((tm, d), F32), pltpu.VMEM((tm, d), F32),
                            pltpu.SemaphoreType.DMA((TOP_K,))],
        ),
        out_shape=jax.ShapeDtypeStruct((m, d), F32),
        compiler_params=_cparams(("arbitrary",)),
        name="moe_combine",
    )(dest, ybuf, hres, gate)


def kernel(x, attn_norm_w, ffn_norm_w, final_norm_w, da_w_in, da_lambda, da_subln_w, da_w_out, nsa_w_in, nsa_cmp_pe, nsa_cmp_w, nsa_w_out, mla_w_in, mla_q_norm_w, mla_w_uq, mla_kv_norm_w, mla_w_ukv, mla_w_out, sb_w_in, sb_w_out, ffn_w_gu, ffn_w_down, moe_w_router, moe_b_router, moe_w_gu, moe_w_down):
    batch, seq, d = x.shape
    depth = attn_norm_w.shape[0]
    h = x.reshape(batch * seq, d)
    for i in range(depth):
        kind = i % 4
        j = i // 4
        if kind == 0:
            lam_init = 0.8 - 0.6 * math.exp(-0.3 * i)
            h = _diff_attention(h, attn_norm_w[i], da_w_in[j], da_lambda[j], da_subln_w[j], da_w_out[j],
                                lam_init, batch, seq)
        elif kind == 1:
            h = _nsa_attention(h, attn_norm_w[i], nsa_w_in[j], nsa_cmp_pe[j], nsa_cmp_w[j], nsa_w_out[j],
                               batch, seq)
        elif kind == 2:
            h = _mla_attention(h, attn_norm_w[i], mla_w_in[j], mla_q_norm_w[j], mla_w_uq[j],
                               mla_kv_norm_w[j], mla_w_ukv[j], mla_w_out[j], batch, seq)
        else:
            h = _sb_attention(h, attn_norm_w[i], sb_w_in[j], sb_w_out[j], batch, seq)
        if i % 2 == 0:
            h = _swiglu(h, ffn_norm_w[i], ffn_w_gu[i // 2], ffn_w_down[i // 2])
        else:
            h = _moe(h, ffn_norm_w[i], moe_w_router[i // 2], moe_b_router[i // 2],
                     moe_w_gu, moe_w_down, i // 2)
    return _rmsnorm(h, final_norm_w.astype(F32)).reshape(batch, seq, d)
```

```python
import functools
import math

import numpy as np
import jax
import jax.numpy as jnp
from jax import lax
from jax.experimental import pallas as pl
from jax.experimental.pallas import tpu as pltpu

ROPE_THETA = 10000.0
NORM_EPS = 1e-6

DA_QK_DIM = 64
DA_V_DIM = 2 * DA_QK_DIM

NSA_HEADS = 16
NSA_GROUPS = 4
NSA_CMP_BLOCK = 32
NSA_CMP_STRIDE = 16
NSA_SLC_BLOCK = 64
NSA_SLC_TOPK = 16
NSA_WINDOW = 512

MLA_HEADS = 16
MLA_Q_RANK = 512
MLA_KV_RANK = 512
MLA_NOPE_DIM = 128
MLA_ROPE_DIM = 64
MLA_V_DIM = 128

SB_HEADS = 16

N_EXPERTS = 8
TOP_K = 2
MOE_ROW_BLOCK = 512

LANES = 128
HALF_LANES = LANES // 2
V7X_VMEM_BYTES = 64 * 1024 * 1024
VMEM_LIMIT = V7X_VMEM_BYTES * 7 // 8
NEG = -1e30
LOG2E = math.log2(math.e)
ONES_ROWS = 16
DMA_UNROLL = 8

F32 = jnp.float32
BF16 = jnp.bfloat16


def _cparams(sem):
    return pltpu.CompilerParams(dimension_semantics=sem, vmem_limit_bytes=VMEM_LIMIT)


def _dot(a, b):
    return jnp.dot(a, b, preferred_element_type=F32)


def _dot_nt(a, b):
    return lax.dot_general(a, b, (((1,), (1,)), ((), ())), preferred_element_type=F32)


def _split_bf16(x):
    hi = x.astype(BF16)
    lo = (x - hi.astype(F32)).astype(BF16)
    return hi, lo


def _linear_kernel(*refs, has_norm, prologue, n_w, has_res, rope, act, transpose_out):
    it = iter(refs)
    x_ref = next(it)
    nw_ref = next(it) if has_norm else None
    w_refs = [next(it) for _ in range(n_w)]
    cos_ref = next(it) if rope else None
    sin_ref = next(it) if rope else None
    res_ref = next(it) if has_res else None
    o_ref = next(it)
    xs_ref = next(it) if prologue else None
    j = pl.program_id(1)

    if prologue:
        @pl.when(j == 0)
        def _():
            xf = x_ref[...].astype(F32)
            if has_norm:
                ms = jnp.mean(xf * xf, axis=-1, keepdims=True)
                xf = xf * lax.rsqrt(ms + NORM_EPS) * nw_ref[...]
            xs_ref[...] = xf.astype(BF16)
        xb = xs_ref[...]
    else:
        xb = x_ref[...]

    y = _dot(xb, w_refs[0][...])
    if act == "swiglu":
        u = _dot(xb, w_refs[1][...])
        y = y * jax.nn.sigmoid(y) * u
    elif act == "sigmoid":
        y = jax.nn.sigmoid(y)
    if has_res:
        y = y + res_ref[...]

    if rope:
        lo, hi = rope
        groups = y.shape[1] // LANES
        g0 = j * groups
        tile_has_rope = (g0 < hi) & (g0 + groups > lo)

        @pl.when(tile_has_rope)
        def _():
            cos = cos_ref[...]
            sin = sin_ref[...]
            for c in range(groups):
                yc = y[:, c * LANES:(c + 1) * LANES]
                roped = yc * cos + pltpu.roll(yc, HALF_LANES, 1) * sin
                use = (g0 + c >= lo) & (g0 + c < hi)
                o_ref[:, c * LANES:(c + 1) * LANES] = jnp.where(use, roped, yc).astype(o_ref.dtype)

        @pl.when(jnp.logical_not(tile_has_rope))
        def _():
            o_ref[...] = y.astype(o_ref.dtype)
    elif transpose_out:
        o_ref[...] = y.T.astype(o_ref.dtype)
    else:
        o_ref[...] = y.astype(o_ref.dtype)


def _linear(x, w, *, n_out, k=None, x_col_block=0, w_tile_offsets=(0,), norm_w=None, residual=None,
            rope=None, act=None, out_dtype=None, tm=1024, tn=512, transpose_out=False, name="linear"):
    m = x.shape[0]
    k = x.shape[1] if k is None else k
    tm = min(tm, m)
    tn = min(tn, n_out)
    assert m % tm == 0 and n_out % tn == 0 and w.shape[0] == k
    out_dtype = BF16 if out_dtype is None else out_dtype
    has_norm = norm_w is not None
    prologue = has_norm or x.dtype != BF16
    grid = (m // tm, n_out // tn)

    in_specs = [pl.BlockSpec((tm, k), lambda i, j: (i, x_col_block))]
    args = [x]
    if has_norm:
        in_specs.append(pl.BlockSpec((1, k), lambda i, j: (0, 0)))
        args.append(norm_w.reshape(1, k).astype(F32))
    for off in w_tile_offsets:
        in_specs.append(pl.BlockSpec((k, tn), lambda i, j, off=off: (0, j + off)))
        args.append(w)
    rope_range = None
    if rope is not None:
        cos, sin, lo, hi, seq = rope
        assert seq % tm == 0
        nrep = seq // tm
        for tab in (cos, sin):
            in_specs.append(pl.BlockSpec((tm, LANES), lambda i, j: (i % nrep, 0)))
            args.append(tab)
        rope_range = (lo, hi)
    if residual is not None:
        in_specs.append(pl.BlockSpec((tm, tn), lambda i, j: (i, j)))
        args.append(residual)
    scratch = [pltpu.VMEM((tm, k), BF16)] if prologue else []

    kern = functools.partial(_linear_kernel, has_norm=has_norm, prologue=prologue,
                             n_w=len(w_tile_offsets), has_res=residual is not None,
                             rope=rope_range, act=act, transpose_out=transpose_out)
    if transpose_out:
        assert rope is None
        out_specs = pl.BlockSpec((tn, tm), lambda i, j: (j, i))
        out_shape = jax.ShapeDtypeStruct((n_out, m), out_dtype)
    else:
        out_specs = pl.BlockSpec((tm, tn), lambda i, j: (i, j))
        out_shape = jax.ShapeDtypeStruct((m, n_out), out_dtype)
    return pl.pallas_call(
        kern,
        grid=grid,
        in_specs=in_specs,
        out_specs=out_specs,
        out_shape=out_shape,
        scratch_shapes=scratch,
        compiler_params=_cparams(("parallel", "arbitrary")),
        name=name,
    )(*args)


def _rmsnorm_kernel(x_ref, w_ref, o_ref):
    xf = x_ref[...]
    ms = jnp.mean(xf * xf, axis=-1, keepdims=True)
    o_ref[...] = xf * lax.rsqrt(ms + NORM_EPS) * w_ref[...]


def _rmsnorm(x, w, tm=512):
    m, d = x.shape
    tm = min(tm, m)
    return pl.pallas_call(
        _rmsnorm_kernel,
        grid=(m // tm,),
        in_specs=[pl.BlockSpec((tm, d), lambda i: (i, 0)), pl.BlockSpec((1, d), lambda i: (0, 0))],
        out_specs=pl.BlockSpec((tm, d), lambda i: (i, 0)),
        out_shape=jax.ShapeDtypeStruct((m, d), F32),
        compiler_params=_cparams(("parallel",)),
    )(x, w.reshape(1, d))


def _rope_tables(pos, half):
    inv_freq = ROPE_THETA ** (-np.arange(half, dtype=np.float64) / half)
    ang = np.asarray(pos, np.float64)[:, None] * inv_freq[None, :]
    reps = LANES // half
    cos = np.tile(np.cos(ang), (1, reps))
    sin = np.tile(np.sin(ang), (1, reps))
    sign = np.where(np.arange(LANES) < HALF_LANES, -1.0, 1.0)
    return jnp.asarray(cos, F32), jnp.asarray(sin * sign, F32)


def _tri_tables(nq, reverse=False):
    qi, ki = [], []
    for q in range(nq):
        for kk in (range(q, -1, -1) if reverse else range(q + 1)):
            qi.append(q)
            ki.append(kk)
    return jnp.asarray(qi, jnp.int32), jnp.asarray(ki, jnp.int32)


def _chain_init(m_sc, acc_sc):
    m_sc[...] = jnp.full_like(m_sc, NEG)
    acc_sc[...] = jnp.zeros_like(acc_sc)


def _with_ones(v_t):
    return jnp.concatenate([v_t, jnp.ones((ONES_ROWS, v_t.shape[1]), v_t.dtype)], axis=0)


def _chain_update(c, s, mask, queries_may_be_empty, v_t1, m_sc, acc_sc):
    if mask is not None:
        s = jnp.where(mask, s, NEG)
    m_prev = m_sc[c]
    m_new = jnp.maximum(m_prev, jnp.max(s, axis=0, keepdims=True))
    alpha = jnp.exp2(m_prev - m_new)
    p = jnp.exp2(s - m_new)
    if queries_may_be_empty:
        p = jnp.where(mask, p, 0.0)
    acc_sc[c] = alpha * acc_sc[c] + _dot(v_t1, p.astype(BF16))
    m_sc[c] = m_new


def _chain_result(c, acc_sc, guard_empty=False):
    acc = acc_sc[c]
    den = acc[LANES:LANES + 1]
    if guard_empty:
        den = jnp.where(den > 0.0, den, 1.0)
    return acc[:LANES] * (1.0 / den)


def _chain_scratch(n_chains, t):
    return [pltpu.VMEM((n_chains, 1, t), F32), pltpu.VMEM((n_chains, LANES + ONES_ROWS, t), F32)]


def _tile_causal(t):
    return lax.broadcasted_iota(jnp.int32, (t, t), 0) <= lax.broadcasted_iota(jnp.int32, (t, t), 1)


def _head_rows(ref, h):
    return ref[h * LANES:(h + 1) * LANES, :]


def _heads(ref, h):
    return ref[:, h * LANES:(h + 1) * LANES]


def _da_kernel(qt_ref, kt_ref, q_ref, k_ref, vt_ref, lam_ref, sub_ref, o_ref, m_sc, acc_sc,
               *, lam_init, t, hb):
    step = pl.program_id(2)
    qi = qt_ref[step]
    ki = kt_ref[step]

    @pl.when(ki == 0)
    def _():
        _chain_init(m_sc, acc_sc)

    lane = lax.broadcasted_iota(jnp.int32, (1, LANES), 1)
    first_map = (lane % HALF_LANES) < (HALF_LANES // 2)

    def update(mask):
        for h in range(hb):
            q = _heads(q_ref, h)
            k = _heads(k_ref, h)
            v_t1 = _with_ones(_head_rows(vt_ref, h))
            zero = jnp.zeros_like(q)
            for j, qm in enumerate((jnp.where(first_map, q, zero), jnp.where(first_map, zero, q))):
                _chain_update(2 * h + j, _dot_nt(k, qm), mask, False, v_t1, m_sc, acc_sc)

    @pl.when(ki < qi)
    def _():
        update(None)

    @pl.when(ki == qi)
    def _():
        update(_tile_causal(t))
        lam = lam_ref[...]
        lam_full = (jnp.exp(jnp.sum(lam[0:1] * lam[1:2], axis=-1, keepdims=True))
                    - jnp.exp(jnp.sum(lam[2:3] * lam[3:4], axis=-1, keepdims=True)) + lam_init)
        for h in range(hb):
            o = _chain_result(2 * h, acc_sc) - lam_full * _chain_result(2 * h + 1, acc_sc)
            ms = jnp.mean(o * o, axis=0, keepdims=True)
            o = o * lax.rsqrt(ms + NORM_EPS) * sub_ref[...]
            o_ref[:, h * LANES:(h + 1) * LANES] = (o * (1.0 - lam_init)).T.astype(o_ref.dtype)


def _da_column_order(heads):
    dk = DA_QK_DIM
    hd = dk // 2
    per = 4 * dk + DA_V_DIM
    q_idx, k_idx, v_idx = [], [], []
    for h in range(heads):
        base = h * per
        for out, off in ((q_idx, 0), (k_idx, 2 * dk)):
            a = base + off + np.arange(dk)
            b = base + off + dk + np.arange(dk)
            out += [a[:hd], b[:hd], a[hd:], b[hd:]]
        v_idx.append(base + 4 * dk + np.arange(DA_V_DIM))
    return np.concatenate(q_idx + k_idx + v_idx)


def _diff_attention(hres, norm_w, w_in, lam, subln_w, w_out, lam_init, batch, seq):
    m, d = hres.shape
    heads = w_in.shape[1] // (4 * DA_QK_DIM + DA_V_DIM)
    hw = heads * LANES
    col_scale = np.ones(3 * hw, np.float32)
    col_scale[:hw] = DA_QK_DIM ** -0.5 * LOG2E
    w_perm = (w_in[:, _da_column_order(heads)] * col_scale[None, :]).astype(BF16)
    cos, sin = _rope_tables(np.arange(seq), DA_QK_DIM // 2)
    tn = 512
    proj = _linear(hres, w_perm[:, :2 * hw], n_out=2 * hw, norm_w=norm_w, tn=tn,
                   rope=(cos, sin, 0, 2 * hw // LANES, seq), name="da_qk_proj")
    v_t = _linear(hres, w_perm[:, 2 * hw:], n_out=hw, norm_w=norm_w, tn=tn, transpose_out=True,
                  name="da_vt_proj")

    t = min(512, seq)
    nq = seq // t
    qt, kt = _tri_tables(nq)
    hb = min(8, heads)
    hg = heads // hb
    bw = hb * LANES
    kern = functools.partial(_da_kernel, lam_init=lam_init, t=t, hb=hb)
    attn = pl.pallas_call(
        kern,
        grid_spec=pltpu.PrefetchScalarGridSpec(
            num_scalar_prefetch=2,
            grid=(batch, hg, qt.shape[0]),
            in_specs=[
                pl.BlockSpec((t, bw), lambda b, h, s, qt, kt: (b * nq + qt[s], h)),
                pl.BlockSpec((t, bw), lambda b, h, s, qt, kt: (b * nq + kt[s], hg + h)),
                pl.BlockSpec((bw, t), lambda b, h, s, qt, kt: (h, b * nq + kt[s])),
                pl.BlockSpec(lam.shape, lambda b, h, s, qt, kt: (0, 0)),
                pl.BlockSpec((DA_V_DIM, 1), lambda b, h, s, qt, kt: (0, 0)),
            ],
            out_specs=pl.BlockSpec((t, bw), lambda b, h, s, qt, kt: (b * nq + qt[s], h)),
            scratch_shapes=_chain_scratch(2 * hb, t),
        ),
        out_shape=jax.ShapeDtypeStruct((m, hw), BF16),
        compiler_params=_cparams(("parallel", "parallel", "arbitrary")),
        name="da_attention",
    )(qt, kt, proj, proj, v_t, lam.astype(F32), subln_w.reshape(DA_V_DIM, 1).astype(F32))
    return _linear(attn, w_out.astype(BF16), n_out=d, residual=hres, out_dtype=F32, name="da_out_proj")


def _mla_kernel(qt_ref, kt_ref, qn_ref, qr_ref, kn_ref, kr_ref, vt_ref, o_ref, m_sc, acc_sc, *, t, hb):
    step = pl.program_id(2)
    qi = qt_ref[step]
    ki = kt_ref[step]

    @pl.when(ki == 0)
    def _():
        _chain_init(m_sc, acc_sc)

    kr = kr_ref[...].astype(BF16)

    def update(mask):
        for h in range(hb):
            q = jnp.concatenate([_heads(qn_ref, h), _heads(qr_ref, h)], axis=1)
            k = jnp.concatenate([_heads(kn_ref, h), kr], axis=1)
            _chain_update(h, _dot_nt(k, q), mask, False, _with_ones(_head_rows(vt_ref, h)), m_sc, acc_sc)

    @pl.when(ki < qi)
    def _():
        update(None)

    @pl.when(ki == qi)
    def _():
        update(_tile_causal(t))
        for h in range(hb):
            o_ref[:, h * LANES:(h + 1) * LANES] = _chain_result(h, acc_sc).T.astype(o_ref.dtype)


def _mla_q_columns(heads):
    per = MLA_NOPE_DIM + MLA_ROPE_DIM
    hd = MLA_ROPE_DIM // 2
    idx, keep = [], []
    for h in range(heads):
        idx.append(h * per + np.arange(MLA_NOPE_DIM))
        keep.append(np.ones(MLA_NOPE_DIM))
    for h in range(heads):
        s = h % 2
        g_idx = np.zeros(LANES, np.int64)
        g_keep = np.zeros(LANES)
        r0 = h * per + MLA_NOPE_DIM
        g_idx[s * hd:(s + 1) * hd] = r0 + np.arange(hd)
        g_idx[HALF_LANES + s * hd:HALF_LANES + (s + 1) * hd] = r0 + hd + np.arange(hd)
        g_keep[s * hd:(s + 1) * hd] = 1.0
        g_keep[HALF_LANES + s * hd:HALF_LANES + (s + 1) * hd] = 1.0
        idx.append(g_idx)
        keep.append(g_keep)
    return np.concatenate(idx), np.concatenate(keep)


def _mla_attention(hres, norm_w, w_in, q_norm_w, w_uq, kv_norm_w, w_ukv, w_out, batch, seq):
    m, d = hres.shape
    heads = MLA_HEADS
    hw = heads * LANES
    hd = MLA_ROPE_DIM // 2
    assert MLA_Q_RANK == MLA_KV_RANK and MLA_NOPE_DIM == LANES and MLA_V_DIM == LANES
    rank = MLA_Q_RANK
    r0 = 2 * rank
    kr_cols = np.concatenate([r0 + np.arange(hd), r0 + np.arange(hd),
                              r0 + hd + np.arange(hd), r0 + hd + np.arange(hd)])
    w_in_x = jnp.concatenate([w_in[:, :2 * rank], w_in[:, kr_cols]], axis=1).astype(BF16)
    cos, sin = _rope_tables(np.arange(seq), hd)
    n_in = 2 * rank + LANES
    tn_in = 3 * LANES if n_in % (3 * LANES) == 0 else LANES
    proj = _linear(hres, w_in_x, n_out=n_in, norm_w=norm_w, tn=tn_in, out_dtype=F32,
                   rope=(cos, sin, 2 * rank // LANES, n_in // LANES, seq), name="mla_in_proj")

    q_idx, q_keep = _mla_q_columns(heads)
    c2 = (MLA_NOPE_DIM + MLA_ROPE_DIM) ** -0.5 * LOG2E
    w_uq_x = (w_uq[:, q_idx] * jnp.asarray(q_keep * c2, F32)[None, :]).astype(BF16)
    tn = 512
    qcat = _linear(proj, w_uq_x, n_out=2 * hw, k=rank, x_col_block=0, norm_w=q_norm_w, tn=tn,
                   rope=(cos, sin, hw // LANES, 2 * hw // LANES, seq), name="mla_q_proj")
    k_nope = _linear(proj, w_ukv[:, :, :MLA_NOPE_DIM].reshape(rank, hw).astype(BF16), n_out=hw, k=rank,
                     x_col_block=1, norm_w=kv_norm_w, tn=tn, name="mla_k_proj")
    v_t = _linear(proj, w_ukv[:, :, MLA_NOPE_DIM:].reshape(rank, hw).astype(BF16), n_out=hw, k=rank,
                  x_col_block=1, norm_w=kv_norm_w, tn=tn, transpose_out=True, name="mla_vt_proj")

    t = min(512, seq)
    nq = seq // t
    qt, kt = _tri_tables(nq)
    kr_block = 2 * rank // LANES
    hb = min(8, heads)
    hg = heads // hb
    bw = hb * LANES
    kern = functools.partial(_mla_kernel, t=t, hb=hb)
    attn = pl.pallas_call(
        kern,
        grid_spec=pltpu.PrefetchScalarGridSpec(
            num_scalar_prefetch=2,
            grid=(batch, hg, qt.shape[0]),
            in_specs=[
                pl.BlockSpec((t, bw), lambda b, h, s, qt, kt: (b * nq + qt[s], h)),
                pl.BlockSpec((t, bw), lambda b, h, s, qt, kt: (b * nq + qt[s], hg + h)),
                pl.BlockSpec((t, bw), lambda b, h, s, qt, kt: (b * nq + kt[s], h)),
                pl.BlockSpec((t, LANES), lambda b, h, s, qt, kt: (b * nq + kt[s], kr_block)),
                pl.BlockSpec((bw, t), lambda b, h, s, qt, kt: (h, b * nq + kt[s])),
            ],
            out_specs=pl.BlockSpec((t, bw), lambda b, h, s, qt, kt: (b * nq + qt[s], h)),
            scratch_shapes=_chain_scratch(hb, t),
        ),
        out_shape=jax.ShapeDtypeStruct((m, hw), BF16),
        compiler_params=_cparams(("parallel", "parallel", "arbitrary")),
        name="mla_attention",
    )(qt, kt, qcat, qcat, k_nope, proj, v_t)
    return _linear(attn, w_out.astype(BF16), n_out=d, residual=hres, out_dtype=F32, name="mla_out_proj")


def _sb_kernel(qt_ref, kt_ref, q_ref, k_ref, v_ref, tri_ref, o_ref, carry_sc, acc_sc, *, t, hb):
    step = pl.program_id(2)
    qi = qt_ref[step]
    ki = kt_ref[step]

    @pl.when(ki == qi)
    def _():
        carry_sc[...] = jnp.zeros_like(carry_sc)
        acc_sc[...] = jnp.zeros_like(acc_sc)

    tri = tri_ref[...]
    chunk = tri.shape[0]

    def update(masked):
        if masked:
            mask = lax.broadcasted_iota(jnp.int32, (t, t), 1) < lax.broadcasted_iota(jnp.int32, (t, t), 0)
        for h in range(hb):
            z = _dot_nt(_heads(q_ref, h), _heads(k_ref, h))
            log_beta = jnp.minimum(z, 0.0) - jnp.log2(1.0 + jnp.exp2(-jnp.abs(z)))
            log_rest = log_beta - z
            if masked:
                log_rest = jnp.where(mask, log_rest, 0.0)
            carry = carry_sc[h]
            parts = []
            for c in range(t // chunk - 1, -1, -1):
                cols = slice(c * chunk, (c + 1) * chunk)
                hi, lo = _split_bf16(log_rest[:, cols])
                tail = _dot(hi, tri) + _dot(lo, tri) + carry
                parts.append(jnp.exp2(log_beta[:, cols] + tail))
                carry = carry + jnp.sum(log_rest[:, cols], axis=-1, keepdims=True)
            a = jnp.concatenate(parts[::-1], axis=1)
            if masked:
                a = jnp.where(mask, a, 0.0)
            acc_sc[h] += _dot(a.astype(BF16), _heads(v_ref, h))
            carry_sc[h] = carry

    @pl.when(ki == qi)
    def _():
        update(True)

    @pl.when(ki < qi)
    def _():
        update(False)

    @pl.when(ki == 0)
    def _():
        for h in range(hb):
            o_ref[:, h * LANES:(h + 1) * LANES] = acc_sc[h].astype(o_ref.dtype)


def _sb_attention(hres, norm_w, w_in, w_out, batch, seq):
    m, d = hres.shape
    heads = SB_HEADS
    hw = heads * LANES
    assert w_in.shape[1] == 3 * hw
    col_scale = np.ones(3 * hw, np.float32)
    col_scale[:hw] = LANES ** -0.5 * LOG2E
    qkv = _linear(hres, (w_in * col_scale[None, :]).astype(BF16), n_out=3 * hw, norm_w=norm_w,
                  name="sb_qkv_proj")
    t = min(512, seq)
    nq = seq // t
    qt, kt = _tri_tables(nq, reverse=True)
    chunk = min(2 * LANES, t)
    tri = jnp.asarray(np.arange(chunk)[:, None] > np.arange(chunk)[None, :], BF16)
    hb = min(8, heads)
    hg = heads // hb
    bw = hb * LANES
    kern = functools.partial(_sb_kernel, t=t, hb=hb)
    attn = pl.pallas_call(
        kern,
        grid_spec=pltpu.PrefetchScalarGridSpec(
            num_scalar_prefetch=2,
            grid=(batch, hg, qt.shape[0]),
            in_specs=[
                pl.BlockSpec((t, bw), lambda b, h, s, qt, kt: (b * nq + qt[s], h)),
                pl.BlockSpec((t, bw), lambda b, h, s, qt, kt: (b * nq + kt[s], hg + h)),
                pl.BlockSpec((t, bw), lambda b, h, s, qt, kt: (b * nq + kt[s], 2 * hg + h)),
                pl.BlockSpec((chunk, chunk), lambda b, h, s, qt, kt: (0, 0)),
            ],
            out_specs=pl.BlockSpec((t, bw), lambda b, h, s, qt, kt: (b * nq + qt[s], h)),
            scratch_shapes=[pltpu.VMEM((hb, t, 1), F32), pltpu.VMEM((hb, t, LANES), F32)],
        ),
        out_shape=jax.ShapeDtypeStruct((m, hw), BF16),
        compiler_params=_cparams(("parallel", "parallel", "arbitrary")),
        name="sb_attention",
    )(qt, kt, qkv, qkv, qkv, tri)
    return _linear(attn, w_out.astype(BF16), n_out=d, residual=hres, out_dtype=F32, name="sb_out_proj")


def _nsa_compress_kernel(x_ref, w_ref, pe_ref, cos_ref, sin_ref, o_ref, *, rope, transpose_out):
    x = x_ref[0]
    half = w_ref.shape[0] // 2
    y_lo = _dot(x, w_ref[:half, :])
    y_hi = _dot(x, w_ref[half:, :])
    n = y_hi.shape[0]
    y = y_lo + pltpu.roll(y_hi, n - 1, 0) + _dot(pe_ref[...], w_ref[...])[0:1]
    if rope:
        y = y * cos_ref[...] + pltpu.roll(y, HALF_LANES, 1) * sin_ref[...]
    o_ref[0] = (y.T if transpose_out else y).astype(o_ref.dtype)


def _nsa_compress(x_chunks, w, pe, cos, sin, rope, transpose_out):
    bg, n_chunk, kdim = x_chunks.shape
    dh = w.shape[1]
    out_block = (1, dh, n_chunk) if transpose_out else (1, n_chunk, dh)
    return pl.pallas_call(
        functools.partial(_nsa_compress_kernel, rope=rope, transpose_out=transpose_out),
        grid=(bg,),
        in_specs=[pl.BlockSpec((1, n_chunk, kdim), lambda i: (i, 0, 0)),
                  pl.BlockSpec(w.shape, lambda i: (0, 0)),
                  pl.BlockSpec(pe.shape, lambda i: (0, 0)),
                  pl.BlockSpec(cos.shape, lambda i: (0, 0)),
                  pl.BlockSpec(sin.shape, lambda i: (0, 0))],
        out_specs=pl.BlockSpec(out_block, lambda i: (i, 0, 0)),
        out_shape=jax.ShapeDtypeStruct((bg,) + out_block[1:], BF16),
        compiler_params=_cparams(("parallel",)),
        name="nsa_compress",
    )(x_chunks, w, pe, cos, sin)


def _head_gate(gates, branch, head):
    lane = lax.broadcasted_iota(jnp.int32, gates.shape, 1)
    return jnp.sum(jnp.where(lane == branch * NSA_HEADS + head, gates, 0.0), axis=-1, keepdims=True)


def _nsa_cmp_kernel(q_ref, kc_ref, vct_ref, ovt_ref, g_ref, o_ref, sel_ref, *, tq, rep, n_sel):
    qi = pl.program_id(2)
    grp = pl.program_id(1)
    qs = jnp.concatenate([q_ref[:, r * LANES:(r + 1) * LANES] for r in range(rep)], axis=0)
    kc = kc_ref[0]
    n_cmp = kc.shape[0]
    rows = rep * tq

    st = _dot_nt(kc, qs)
    qpos_t = qi * tq + lax.broadcasted_iota(jnp.int32, (n_cmp, rep, tq), 2).reshape(n_cmp, rows)
    cend_t = lax.broadcasted_iota(jnp.int32, (n_cmp, rows), 0) * NSA_CMP_STRIDE + (NSA_CMP_BLOCK - 1)
    mask_t = cend_t <= qpos_t
    st = jnp.where(mask_t, st, NEG)
    pt = jnp.where(mask_t, jnp.exp2(st - jnp.max(st, axis=0, keepdims=True)), 0.0)
    den_t = jnp.sum(pt, axis=0, keepdims=True)
    pt = pt / jnp.where(den_t > 0.0, den_t, 1.0)
    o_t = _dot(vct_ref[0], pt.astype(BF16))
    gates = g_ref[...]
    for r in range(rep):
        gate = _head_gate(gates, 0, grp * rep + r)
        o_ref[:, r * LANES:(r + 1) * LANES] = gate * o_t[:, r * tq:(r + 1) * tq].T

    psum = pt[:, 0:tq]
    for r in range(1, rep):
        psum = psum + pt[:, r * tq:(r + 1) * tq]
    hi, lo = _split_bf16(psum)
    ovt = ovt_ref[...]
    imp = _dot(ovt, hi) + _dot(ovt, lo)

    n_pad = imp.shape[0]
    blk = lax.broadcasted_iota(jnp.int32, (n_pad, tq), 0)
    cur = (qi * tq + lax.broadcasted_iota(jnp.int32, (n_pad, tq), 1)) // NSA_SLC_BLOCK
    forced = (blk == 0) | (blk == cur) | (blk == cur - 1)
    val = jnp.where(forced, jnp.inf, jnp.where(blk <= cur, imp, -jnp.inf))
    sel = jnp.zeros((n_pad, tq), F32)
    for _ in range(n_sel):
        top = jnp.max(val, axis=0, keepdims=True)
        first = jnp.min(jnp.where(val == top, blk, n_pad), axis=0, keepdims=True)
        pick = blk == first
        sel = jnp.where(pick, 1.0, sel)
        val = jnp.where(pick, -jnp.inf, val)
    sel_ref[...] = sel.astype(sel_ref.dtype)


def _nsa_win_kernel(q_ref, k_ref, vt_ref, g_ref, prev_ref, o_ref, m_sc, acc_sc, *, t, rep, nwin):
    qi = pl.program_id(2)
    kk = pl.program_id(3)
    grp = pl.program_id(1)
    kb = qi - (nwin - 1) + kk

    @pl.when(kk == 0)
    def _():
        _chain_init(m_sc, acc_sc)

    def update(mask):
        k = k_ref[...]
        v_t1 = _with_ones(vt_ref[...])
        for r in range(rep):
            _chain_update(r, _dot_nt(k, _heads(q_ref, r)), mask, mask is not None, v_t1, m_sc, acc_sc)

    @pl.when((kk == 0) & (kb >= 0))
    def _():
        update(jnp.logical_not(_tile_causal(t)))

    @pl.when((kk > 0) & (kk < nwin - 1) & (kb >= 0))
    def _():
        update(None)

    @pl.when(kk == nwin - 1)
    def _():
        update(_tile_causal(t))
        gates = g_ref[...]
        for r in range(rep):
            gate = _head_gate(gates, 2, grp * rep + r)
            o_ref[:, r * LANES:(r + 1) * LANES] = _heads(prev_ref, r) + gate * _chain_result(r, acc_sc).T


def _nsa_slc_kernel(qt_ref, kt_ref, q_ref, k_ref, vt_ref, sel_ref, g_ref, prev_ref, o_ref,
                    m_sc, acc_sc, *, t, rep):
    step = pl.program_id(2)
    grp = pl.program_id(1)
    qi = qt_ref[step]
    ki = kt_ref[step]

    @pl.when(ki == 0)
    def _():
        _chain_init(m_sc, acc_sc)

    n_pad = sel_ref.shape[0]
    blk = lax.broadcasted_iota(jnp.int32, (t, n_pad), 1)
    kblk = (ki * t + lax.broadcasted_iota(jnp.int32, (t, n_pad), 0)) // NSA_SLC_BLOCK
    expand = jnp.where(blk == kblk, 1.0, 0.0).astype(BF16)
    chosen = _dot(expand, sel_ref[...]) > 0.5

    def update(mask):
        k = k_ref[...]
        v_t1 = _with_ones(vt_ref[...])
        for r in range(rep):
            _chain_update(r, _dot_nt(k, _heads(q_ref, r)), mask, True, v_t1, m_sc, acc_sc)

    @pl.when(ki < qi)
    def _():
        update(chosen)

    @pl.when(ki == qi)
    def _():
        update(chosen & _tile_causal(t))
        gates = g_ref[...]
        for r in range(rep):
            gate = _head_gate(gates, 1, grp * rep + r)
            o = _chain_result(r, acc_sc, guard_empty=True).T
            o_ref[:, r * LANES:(r + 1) * LANES] = (_heads(prev_ref, r) + gate * o).astype(o_ref.dtype)


def _nsa_attention(hres, norm_w, w_in, cmp_pe, cmp_w, w_out, batch, seq):
    m, d = hres.shape
    heads, groups = NSA_HEADS, NSA_GROUPS
    rep = heads // groups
    dh = d // heads
    assert dh == LANES
    kvw = groups * dh
    n_main = heads * dh + 6 * kvw
    gw = rep * LANES

    hq = heads * dh
    cos, sin = _rope_tables(np.arange(seq), dh // 2)
    kv_cols = lambda i: np.arange(hq + i * kvw, hq + (i + 1) * kvw)
    main_cols = np.concatenate([np.arange(hq), kv_cols(0), kv_cols(1), kv_cols(2), kv_cols(4)])
    col_scale = np.ones(main_cols.size, np.float32)
    col_scale[:hq] = dh ** -0.5 * LOG2E
    proj = _linear(hres, (w_in[:, main_cols] * col_scale[None, :]).astype(BF16), n_out=main_cols.size,
                   norm_w=norm_w, rope=(cos, sin, 0, hq // LANES, seq), name="nsa_qk_proj")
    v_t = _linear(hres, w_in[:, np.concatenate([kv_cols(3), kv_cols(5)])].astype(BF16), n_out=2 * kvw,
                  norm_w=norm_w, transpose_out=True, name="nsa_vt_proj")
    w_gate = jnp.pad(w_in[:, n_main:], ((0, 0), (0, LANES - 3 * heads))).astype(BF16)
    gates = _linear(hres, w_gate, n_out=LANES, norm_w=norm_w, act="sigmoid", out_dtype=F32,
                    name="nsa_gate_proj")
    k_slc_blk = (hq + 2 * kvw) // LANES
    k_win_blk = (hq + 3 * kvw) // LANES

    n_chunk = seq // NSA_CMP_STRIDE
    assert NSA_CMP_BLOCK == 2 * NSA_CMP_STRIDE
    cmp_end = np.arange(n_chunk) * NSA_CMP_STRIDE + NSA_CMP_BLOCK - 1
    ccos, csin = _rope_tables(cmp_end, dh // 2)

    def chunks(i):
        lo = hq + i * kvw
        x = proj[:, lo:lo + kvw].reshape(batch, n_chunk, NSA_CMP_STRIDE, groups, dh)
        return x.transpose(0, 3, 1, 2, 4).reshape(batch * groups, n_chunk, NSA_CMP_STRIDE * dh)

    def cmp_weights(i):
        return (cmp_w[i].reshape(NSA_CMP_BLOCK * dh, dh).astype(BF16),
                jnp.broadcast_to(cmp_pe[i].reshape(1, NSA_CMP_BLOCK * dh), (8, NSA_CMP_BLOCK * dh)).astype(BF16))

    k_cmp = _nsa_compress(chunks(0), *cmp_weights(0), ccos, csin, True, False)
    v_cmp_t = _nsa_compress(chunks(1), *cmp_weights(1), ccos, csin, False, True)

    n_slc = seq // NSA_SLC_BLOCK
    n_sel = min(NSA_SLC_TOPK, n_slc)
    n_pad = max(LANES, n_slc)
    n_cmp = n_chunk - NSA_CMP_BLOCK // NSA_CMP_STRIDE + 1
    cs = np.arange(n_chunk)[None, :] * NSA_CMP_STRIDE
    js = np.arange(n_pad)[:, None] * NSA_SLC_BLOCK
    ovt = ((cs < js + NSA_SLC_BLOCK) & (cs + NSA_CMP_BLOCK > js)
           & (np.arange(n_chunk)[None, :] < n_cmp) & (np.arange(n_pad)[:, None] < n_slc))
    ovt = jnp.asarray(ovt, BF16)

    tq = min(256, seq)
    nqc = seq // tq
    o_c, sel = pl.pallas_call(
        functools.partial(_nsa_cmp_kernel, tq=tq, rep=rep, n_sel=n_sel),
        grid=(batch, groups, nqc),
        in_specs=[
            pl.BlockSpec((tq, gw), lambda b, g, i: (b * nqc + i, g)),
            pl.BlockSpec((1, n_chunk, dh), lambda b, g, i: (b * groups + g, 0, 0)),
            pl.BlockSpec((1, dh, n_chunk), lambda b, g, i: (b * groups + g, 0, 0)),
            pl.BlockSpec(ovt.shape, lambda b, g, i: (0, 0)),
            pl.BlockSpec((tq, LANES), lambda b, g, i: (b * nqc + i, 0)),
        ],
        out_specs=[
            pl.BlockSpec((tq, gw), lambda b, g, i: (b * nqc + i, g)),
            pl.BlockSpec((n_pad, tq), lambda b, g, i: (b * groups + g, i)),
        ],
        out_shape=[jax.ShapeDtypeStruct((m, heads * dh), F32),
                   jax.ShapeDtypeStruct((batch * groups * n_pad, seq), BF16)],
        compiler_params=_cparams(("parallel", "parallel", "arbitrary")),
        name="nsa_compressed_select",
    )(proj, k_cmp, v_cmp_t, ovt, gates)

    t = min(512, seq)
    nq = seq // t
    assert NSA_WINDOW % t == 0
    nwin = NSA_WINDOW // t + 1
    key_tile = lambda i, kk: jnp.maximum(i - (nwin - 1) + kk, 0)
    o_cw = pl.pallas_call(
        functools.partial(_nsa_win_kernel, t=t, rep=rep, nwin=nwin),
        grid=(batch, groups, nq, nwin),
        in_specs=[
            pl.BlockSpec((t, gw), lambda b, g, i, kk: (b * nq + i, g)),
            pl.BlockSpec((t, LANES), lambda b, g, i, kk: (b * nq + key_tile(i, kk), k_win_blk + g)),
            pl.BlockSpec((LANES, t), lambda b, g, i, kk: (groups + g, b * nq + key_tile(i, kk))),
            pl.BlockSpec((t, LANES), lambda b, g, i, kk: (b * nq + i, 0)),
            pl.BlockSpec((t, gw), lambda b, g, i, kk: (b * nq + i, g)),
        ],
        out_specs=pl.BlockSpec((t, gw), lambda b, g, i, kk: (b * nq + i, g)),
        out_shape=jax.ShapeDtypeStruct((m, heads * dh), F32),
        scratch_shapes=_chain_scratch(rep, t),
        compiler_params=_cparams(("parallel", "parallel", "parallel", "arbitrary")),
        name="nsa_window",
    )(proj, proj, v_t, gates, o_c)

    qt, kt = _tri_tables(nq)
    attn = pl.pallas_call(
        functools.partial(_nsa_slc_kernel, t=t, rep=rep),
        grid_spec=pltpu.PrefetchScalarGridSpec(
            num_scalar_prefetch=2,
            grid=(batch, groups, qt.shape[0]),
            in_specs=[
                pl.BlockSpec((t, gw), lambda b, g, s, qt, kt: (b * nq + qt[s], g)),
                pl.BlockSpec((t, LANES), lambda b, g, s, qt, kt: (b * nq + kt[s], k_slc_blk + g)),
                pl.BlockSpec((LANES, t), lambda b, g, s, qt, kt: (g, b * nq + kt[s])),
                pl.BlockSpec((n_pad, t), lambda b, g, s, qt, kt: (b * groups + g, qt[s])),
                pl.BlockSpec((t, LANES), lambda b, g, s, qt, kt: (b * nq + qt[s], 0)),
                pl.BlockSpec((t, gw), lambda b, g, s, qt, kt: (b * nq + qt[s], g)),
            ],
            out_specs=pl.BlockSpec((t, gw), lambda b, g, s, qt, kt: (b * nq + qt[s], g)),
            scratch_shapes=_chain_scratch(rep, t),
        ),
        out_shape=jax.ShapeDtypeStruct((m, heads * dh), BF16),
        compiler_params=_cparams(("parallel", "parallel", "arbitrary")),
        name="nsa_selected",
    )(qt, kt, proj, proj, v_t, sel, gates, o_cw)
    return _linear(attn, w_out.astype(BF16), n_out=d, residual=hres, out_dtype=F32, name="nsa_out_proj")


def _swiglu(hres, norm_w, w_gu, w_down):
    d = hres.shape[1]
    ff = w_down.shape[0]
    tn = 512
    hidden = _linear(hres, w_gu.astype(BF16), n_out=ff, w_tile_offsets=(0, ff // tn), norm_w=norm_w,
                     act="swiglu", tn=tn, name="ffn_gate_up")
    return _linear(hidden, w_down.astype(BF16), n_out=d, residual=hres, out_dtype=F32, name="ffn_down")


def _router_kernel(x_ref, nw_ref, w_ref, b_ref, f_ref, idx_ref, gate_ref, *, n_experts):
    xf = x_ref[...]
    ms = jnp.mean(xf * xf, axis=-1, keepdims=True)
    f = xf * lax.rsqrt(ms + NORM_EPS) * nw_ref[...]
    f_ref[...] = f
    fh, fl = _split_bf16(f)
    w = w_ref[...]
    wh, wl = _split_bf16(w)
    logits = _dot(fh, wh) + _dot(fl, wh) + _dot(fh, wl) + b_ref[...]
    lane = lax.broadcasted_iota(jnp.int32, logits.shape, 1)
    logits = jnp.where(lane < n_experts, logits, -jnp.inf)
    v1 = jnp.max(logits, axis=-1, keepdims=True)
    i1 = jnp.min(jnp.where(logits == v1, lane, LANES), axis=-1, keepdims=True)
    rest = jnp.where(lane == i1, -jnp.inf, logits)
    v2 = jnp.max(rest, axis=-1, keepdims=True)
    i2 = jnp.min(jnp.where(rest == v2, lane, LANES), axis=-1, keepdims=True)
    e2 = jnp.exp(v2 - v1)
    g1 = 1.0 / (1.0 + e2)
    g2 = e2 / (1.0 + e2)
    idx_ref[...] = jnp.where(lane == 0, i1, jnp.where(lane == 1, i2, 0))
    gate_ref[...] = jnp.where(lane == 0, g1, jnp.where(lane == 1, g2, 0.0))


def _gather_kernel(src_ref, nvalid_ref, x_hbm, o_ref, buf, sem):
    blk = pl.program_id(0)
    rows = buf.shape[0]
    nvalid = nvalid_ref[blk]

    def row_copy(r):
        return pltpu.make_async_copy(x_hbm.at[pl.ds(src_ref[blk * rows + r], 1), :],
                                     buf.at[pl.ds(r, 1), :], sem)

    def start(r, c):
        row_copy(r).start()
        return c

    def wait(r, c):
        row_copy(r).wait()
        return c

    @pl.when(nvalid > 0)
    def _():
        lax.fori_loop(0, rows, start, 0, unroll=DMA_UNROLL)
        lax.fori_loop(0, rows, wait, 0, unroll=DMA_UNROLL)
        rid = lax.broadcasted_iota(jnp.int32, (rows, 1), 0)
        o_ref[...] = jnp.where(rid < nvalid, buf[...], 0.0).astype(o_ref.dtype)

    @pl.when(nvalid == 0)
    def _():
        o_ref[...] = jnp.zeros_like(o_ref)


def _weight_tile_is_new(be_ref, blk):
    return (blk == 0) | (be_ref[blk] != be_ref[jnp.maximum(blk - 1, 0)])


def _expert_up_kernel(be_ref, nv_ref, x_ref, wg_ref, wu_ref, o_ref, wg_sc, wu_sc):
    blk = pl.program_id(1)

    @pl.when(_weight_tile_is_new(be_ref, blk))
    def _():
        wg_sc[...] = wg_ref[0, 0].astype(BF16)
        wu_sc[...] = wu_ref[0, 0].astype(BF16)

    @pl.when(nv_ref[blk] > 0)
    def _():
        x = x_ref[...]
        g = _dot(x, wg_sc[...])
        u = _dot(x, wu_sc[...])
        o_ref[...] = (g * jax.nn.sigmoid(g) * u).astype(o_ref.dtype)

    @pl.when(nv_ref[blk] == 0)
    def _():
        o_ref[...] = jnp.zeros_like(o_ref)


def _expert_down_kernel(be_ref, nv_ref, h_ref, w_ref, o_ref, w_sc):
    blk = pl.program_id(1)

    @pl.when(_weight_tile_is_new(be_ref, blk))
    def _():
        w_sc[...] = w_ref[0, 0].astype(BF16)

    @pl.when(nv_ref[blk] > 0)
    def _():
        o_ref[...] = _dot(h_ref[...], w_sc[...])

    @pl.when(nv_ref[blk] == 0)
    def _():
        o_ref[...] = jnp.zeros_like(o_ref)


def _combine_kernel(dst_ref, y_hbm, res_ref, gate_ref, o_ref, buf0, buf1, sem):
    i = pl.program_id(0)
    rows = buf0.shape[0]

    def row_copy(r, slot, buf):
        return pltpu.make_async_copy(y_hbm.at[pl.ds(dst_ref[(i * rows + r) * TOP_K + slot], 1), :],
                                     buf.at[pl.ds(r, 1), :], sem.at[slot])

    def start(r, c):
        row_copy(r, 0, buf0).start()
        row_copy(r, 1, buf1).start()
        return c

    def wait(r, c):
        row_copy(r, 0, buf0).wait()
        row_copy(r, 1, buf1).wait()
        return c

    lax.fori_loop(0, rows, start, 0, unroll=DMA_UNROLL)
    lax.fori_loop(0, rows, wait, 0, unroll=DMA_UNROLL)
    gate = gate_ref[...]
    o_ref[...] = res_ref[...] + (buf0[...] * gate[:, 0:1] + buf1[...] * gate[:, 1:2])


def _moe(hres, norm_w, w_router, b_router, w_gu, w_down, layer):
    m, d = hres.shape
    n_exp = w_router.shape[1]
    ff = w_down.shape[2]
    rb = MOE_ROW_BLOCK
    assert TOP_K == 2 and m % rb == 0
    tm = min(512, m)

    w_r = jnp.pad(w_router, ((0, 0), (0, LANES - n_exp))).astype(F32)
    b_r = jnp.pad(b_router, (0, LANES - n_exp)).reshape(1, LANES).astype(F32)
    f, idx, gate = pl.pallas_call(
        functools.partial(_router_kernel, n_experts=n_exp),
        grid=(m // tm,),
        in_specs=[pl.BlockSpec((tm, d), lambda i: (i, 0)),
                  pl.BlockSpec((1, d), lambda i: (0, 0)),
                  pl.BlockSpec((d, LANES), lambda i: (0, 0)),
                  pl.BlockSpec((1, LANES), lambda i: (0, 0))],
        out_specs=[pl.BlockSpec((tm, d), lambda i: (i, 0)),
                   pl.BlockSpec((tm, LANES), lambda i: (i, 0)),
                   pl.BlockSpec((tm, LANES), lambda i: (i, 0))],
        out_shape=[jax.ShapeDtypeStruct((m, d), F32),
                   jax.ShapeDtypeStruct((m, LANES), jnp.int32),
                   jax.ShapeDtypeStruct((m, LANES), F32)],
        compiler_params=_cparams(("parallel",)),
        name="moe_router",
    )(hres, norm_w.reshape(1, d).astype(F32), w_r, b_r)

    nk = m * TOP_K
    e_flat = idx[:, :TOP_K].reshape(nk)
    onehot = (e_flat[:, None] == jnp.arange(n_exp, dtype=jnp.int32)[None, :]).astype(jnp.int32)
    csum = jnp.cumsum(onehot, axis=0)
    rank = jnp.sum((csum - onehot) * onehot, axis=1)
    counts = csum[-1]
    padded = (counts + rb - 1) // rb * rb
    pend = jnp.cumsum(padded)
    pstart = pend - padded
    dest = (pstart[e_flat] + rank).astype(jnp.int32)
    n_rows = (-(-nk // rb)) * rb + n_exp * rb
    n_blocks = n_rows // rb
    src_tok = jnp.zeros((n_rows,), jnp.int32).at[dest].set(jnp.arange(nk, dtype=jnp.int32) // TOP_K)
    blk_start = jnp.arange(n_blocks, dtype=jnp.int32) * rb
    blk_expert = jnp.minimum(jnp.searchsorted(pend, blk_start, side="right"), n_exp - 1).astype(jnp.int32)
    nvalid = jnp.clip(pstart[blk_expert] + counts[blk_expert] - blk_start, 0, rb).astype(jnp.int32)

    xbuf = pl.pallas_call(
        _gather_kernel,
        grid_spec=pltpu.PrefetchScalarGridSpec(
            num_scalar_prefetch=2,
            grid=(n_blocks,),
            in_specs=[pl.BlockSpec(memory_space=pl.ANY)],
            out_specs=pl.BlockSpec((rb, d), lambda i, src, nv: (i, 0)),
            scratch_shapes=[pltpu.VMEM((rb, d), F32), pltpu.SemaphoreType.DMA(())],
        ),
        out_shape=jax.ShapeDtypeStruct((n_rows, d), BF16),
        compiler_params=_cparams(("arbitrary",)),
        name="moe_gather",
    )(src_tok, nvalid, f)

    tn = 512
    hidden = pl.pallas_call(
        _expert_up_kernel,
        grid_spec=pltpu.PrefetchScalarGridSpec(
            num_scalar_prefetch=2,
            grid=(ff // tn, n_blocks),
            in_specs=[pl.BlockSpec((rb, d), lambda j, i, be, nv: (i, 0)),
                      pl.BlockSpec((1, 1, d, tn), lambda j, i, be, nv: (layer, be[i], 0, j)),
                      pl.BlockSpec((1, 1, d, tn), lambda j, i, be, nv: (layer, be[i], 0, j + ff // tn))],
            out_specs=pl.BlockSpec((rb, tn), lambda j, i, be, nv: (i, j)),
            scratch_shapes=[pltpu.VMEM((d, tn), BF16), pltpu.VMEM((d, tn), BF16)],
        ),
        out_shape=jax.ShapeDtypeStruct((n_rows, ff), BF16),
        compiler_params=_cparams(("arbitrary", "arbitrary")),
        name="moe_expert_up",
    )(blk_expert, nvalid, xbuf, w_gu, w_gu)

    tnd = min(512, d)
    ybuf = pl.pallas_call(
        _expert_down_kernel,
        grid_spec=pltpu.PrefetchScalarGridSpec(
            num_scalar_prefetch=2,
            grid=(d // tnd, n_blocks),
            in_specs=[pl.BlockSpec((rb, ff), lambda j, i, be, nv: (i, 0)),
                      pl.BlockSpec((1, 1, ff, tnd), lambda j, i, be, nv: (layer, be[i], 0, j))],
            out_specs=pl.BlockSpec((rb, tnd), lambda j, i, be, nv: (i, j)),
            scratch_shapes=[pltpu.VMEM((ff, tnd), BF16)],
        ),
        out_shape=jax.ShapeDtypeStruct((n_rows, d), F32),
        compiler_params=_cparams(("arbitrary", "arbitrary")),
        name="moe_expert_down",
    )(blk_expert, nvalid, hidden, w_down)

    return pl.pallas_call(
        _combine_kernel,
        grid_spec=pltpu.PrefetchScalarGridSpec(
            num_scalar_prefetch=1,
            grid=(m // tm,),
            in_specs=[pl.BlockSpec(memory_space=pl.ANY),
                      pl.BlockSpec((tm, d), lambda i, dst: (i, 0)),
                      pl.BlockSpec((tm, LANES), lambda i, dst: (i, 0))],
            out_specs=pl.BlockSpec((tm, d), lambda i, dst: (i, 0)),
            scratch_shapes=[pltpu.VMEM((tm, d), F32), pltpu.VMEM((tm, d), F32),
                            pltpu.SemaphoreType.DMA((TOP_K,))],
        ),
        out_shape=jax.ShapeDtypeStruct((m, d), F32),
        compiler_params=_cparams(("arbitrary",)),
        name="moe_combine",
    )(dest, ybuf, hres, gate)


def kernel(x, attn_norm_w, ffn_norm_w, final_norm_w, da_w_in, da_lambda, da_subln_w, da_w_out, nsa_w_in, nsa_cmp_pe, nsa_cmp_w, nsa_w_out, mla_w_in, mla_q_norm_w, mla_w_uq, mla_kv_norm_w, mla_w_ukv, mla_w_out, sb_w_in, sb_w_out, ffn_w_gu, ffn_w_down, moe_w_router, moe_b_router, moe_w_gu, moe_w_down):
    batch, seq, d = x.shape
    depth = attn_norm_w.shape[0]
    h = x.reshape(batch * seq, d)
    for i in range(depth):
        kind = i % 4
        j = i // 4
        if kind == 0:
            lam_init = 0.8 - 0.6 * math.exp(-0.3 * i)
            h = _diff_attention(h, attn_norm_w[i], da_w_in[j], da_lambda[j], da_subln_w[j], da_w_out[j],
                                lam_init, batch, seq)
        elif kind == 1:
            h = _nsa_attention(h, attn_norm_w[i], nsa_w_in[j], nsa_cmp_pe[j], nsa_cmp_w[j], nsa_w_out[j],
                               batch, seq)
        elif kind == 2:
            h = _mla_attention(h, attn_norm_w[i], mla_w_in[j], mla_q_norm_w[j], mla_w_uq[j],
                               mla_kv_norm_w[j], mla_w_ukv[j], mla_w_out[j], batch, seq)
        else:
            h = _sb_attention(h, attn_norm_w[i], sb_w_in[j], sb_w_out[j], batch, seq)
        if i % 2 == 0:
            h = _swiglu(h, ffn_norm_w[i], ffn_w_gu[i // 2], ffn_w_down[i // 2])
        else:
            h = _moe(h, ffn_norm_w[i], moe_w_router[i // 2], moe_b_router[i // 2],
                     moe_w_gu, moe_w_down, i // 2)
    return _rmsnorm(h, final_norm_w.astype(F32)).reshape(batch, seq, d)
```

```python
import functools
import math

import numpy as np
import jax
import jax.numpy as jnp
from jax import lax
from jax.experimental import pallas as pl
from jax.experimental.pallas import tpu as pltpu

ROPE_THETA = 10000.0
NORM_EPS = 1e-6

DA_QK_DIM = 64
DA_V_DIM = 2 * DA_QK_DIM

NSA_HEADS = 16
NSA_GROUPS = 4
NSA_CMP_BLOCK = 32
NSA_CMP_STRIDE = 16
NSA_SLC_BLOCK = 64
NSA_SLC_TOPK = 16
NSA_WINDOW = 512

MLA_HEADS = 16
MLA_Q_RANK = 512
MLA_KV_RANK = 512
MLA_NOPE_DIM = 128
MLA_ROPE_DIM = 64
MLA_V_DIM = 128

SB_HEADS = 16

N_EXPERTS = 8
TOP_K = 2
MOE_ROW_BLOCK = 512

LANES = 128
HALF_LANES = LANES // 2
V7X_VMEM_BYTES = 64 * 1024 * 1024
VMEM_LIMIT = V7X_VMEM_BYTES * 7 // 8
NEG = -1e30
LOG2E = math.log2(math.e)
ONES_ROWS = 16
DMA_UNROLL = 8
ATTN_TILE = 1024
ATTN_CHAINS = 4

F32 = jnp.float32
BF16 = jnp.bfloat16


def _cparams(sem):
    return pltpu.CompilerParams(dimension_semantics=sem, vmem_limit_bytes=VMEM_LIMIT)


def _dot(a, b):
    return jnp.dot(a, b, preferred_element_type=F32)


def _dot_nt(a, b):
    return lax.dot_general(a, b, (((1,), (1,)), ((), ())), preferred_element_type=F32)


def _split_bf16(x):
    hi = x.astype(BF16)
    lo = (x - hi.astype(F32)).astype(BF16)
    return hi, lo


def _linear_kernel(*refs, has_norm, prologue, n_w, has_res, rope, act, transpose_out):
    it = iter(refs)
    x_ref = next(it)
    nw_ref = next(it) if has_norm else None
    w_refs = [next(it) for _ in range(n_w)]
    cos_ref = next(it) if rope else None
    sin_ref = next(it) if rope else None
    res_ref = next(it) if has_res else None
    o_ref = next(it)
    xs_ref = next(it) if prologue else None
    j = pl.program_id(1)

    if prologue:
        @pl.when(j == 0)
        def _():
            xf = x_ref[...].astype(F32)
            if has_norm:
                ms = jnp.mean(xf * xf, axis=-1, keepdims=True)
                xf = xf * lax.rsqrt(ms + NORM_EPS) * nw_ref[...]
            xs_ref[...] = xf.astype(BF16)
        xb = xs_ref[...]
    else:
        xb = x_ref[...]

    y = _dot(xb, w_refs[0][...])
    if act == "swiglu":
        u = _dot(xb, w_refs[1][...])
        y = y * jax.nn.sigmoid(y) * u
    elif act == "sigmoid":
        y = jax.nn.sigmoid(y)
    if has_res:
        y = y + res_ref[...]

    if rope:
        lo, hi = rope
        groups = y.shape[1] // LANES
        g0 = j * groups
        tile_has_rope = (g0 < hi) & (g0 + groups > lo)

        @pl.when(tile_has_rope)
        def _():
            cos = cos_ref[...]
            sin = sin_ref[...]
            for c in range(groups):
                yc = y[:, c * LANES:(c + 1) * LANES]
                roped = yc * cos + pltpu.roll(yc, HALF_LANES, 1) * sin
                use = (g0 + c >= lo) & (g0 + c < hi)
                o_ref[:, c * LANES:(c + 1) * LANES] = jnp.where(use, roped, yc).astype(o_ref.dtype)

        @pl.when(jnp.logical_not(tile_has_rope))
        def _():
            o_ref[...] = y.astype(o_ref.dtype)
    elif transpose_out:
        o_ref[...] = y.T.astype(o_ref.dtype)
    else:
        o_ref[...] = y.astype(o_ref.dtype)


def _linear(x, w, *, n_out, k=None, x_col_block=0, w_tile_offsets=(0,), norm_w=None, residual=None,
            rope=None, act=None, out_dtype=None, tm=1024, tn=512, transpose_out=False, name="linear"):
    m = x.shape[0]
    k = x.shape[1] if k is None else k
    tm = min(tm, m)
    tn = min(tn, n_out)
    assert m % tm == 0 and n_out % tn == 0 and w.shape[0] == k
    out_dtype = BF16 if out_dtype is None else out_dtype
    has_norm = norm_w is not None
    prologue = has_norm or x.dtype != BF16
    grid = (m // tm, n_out // tn)

    in_specs = [pl.BlockSpec((tm, k), lambda i, j: (i, x_col_block))]
    args = [x]
    if has_norm:
        in_specs.append(pl.BlockSpec((1, k), lambda i, j: (0, 0)))
        args.append(norm_w.reshape(1, k).astype(F32))
    for off in w_tile_offsets:
        in_specs.append(pl.BlockSpec((k, tn), lambda i, j, off=off: (0, j + off)))
        args.append(w)
    rope_range = None
    if rope is not None:
        cos, sin, lo, hi, seq = rope
        assert seq % tm == 0
        nrep = seq // tm
        for tab in (cos, sin):
            in_specs.append(pl.BlockSpec((tm, LANES), lambda i, j: (i % nrep, 0)))
            args.append(tab)
        rope_range = (lo, hi)
    if residual is not None:
        in_specs.append(pl.BlockSpec((tm, tn), lambda i, j: (i, j)))
        args.append(residual)
    scratch = [pltpu.VMEM((tm, k), BF16)] if prologue else []

    kern = functools.partial(_linear_kernel, has_norm=has_norm, prologue=prologue,
                             n_w=len(w_tile_offsets), has_res=residual is not None,
                             rope=rope_range, act=act, transpose_out=transpose_out)
    if transpose_out:
        assert rope is None
        out_specs = pl.BlockSpec((tn, tm), lambda i, j: (j, i))
        out_shape = jax.ShapeDtypeStruct((n_out, m), out_dtype)
    else:
        out_specs = pl.BlockSpec((tm, tn), lambda i, j: (i, j))
        out_shape = jax.ShapeDtypeStruct((m, n_out), out_dtype)
    return pl.pallas_call(
        kern,
        grid=grid,
        in_specs=in_specs,
        out_specs=out_specs,
        out_shape=out_shape,
        scratch_shapes=scratch,
        compiler_params=_cparams(("parallel", "arbitrary")),
        name=name,
    )(*args)


def _rmsnorm_kernel(x_ref, w_ref, o_ref):
    xf = x_ref[...]
    ms = jnp.mean(xf * xf, axis=-1, keepdims=True)
    o_ref[...] = xf * lax.rsqrt(ms + NORM_EPS) * w_ref[...]


def _rmsnorm(x, w, tm=512):
    m, d = x.shape
    tm = min(tm, m)
    return pl.pallas_call(
        _rmsnorm_kernel,
        grid=(m // tm,),
        in_specs=[pl.BlockSpec((tm, d), lambda i: (i, 0)), pl.BlockSpec((1, d), lambda i: (0, 0))],
        out_specs=pl.BlockSpec((tm, d), lambda i: (i, 0)),
        out_shape=jax.ShapeDtypeStruct((m, d), F32),
        compiler_params=_cparams(("parallel",)),
    )(x, w.reshape(1, d))


def _rope_tables(pos, half):
    inv_freq = ROPE_THETA ** (-np.arange(half, dtype=np.float64) / half)
    ang = np.asarray(pos, np.float64)[:, None] * inv_freq[None, :]
    reps = LANES // half
    cos = np.tile(np.cos(ang), (1, reps))
    sin = np.tile(np.sin(ang), (1, reps))
    sign = np.where(np.arange(LANES) < HALF_LANES, -1.0, 1.0)
    return jnp.asarray(cos, F32), jnp.asarray(sin * sign, F32)


def _tri_tables(nq, reverse=False):
    qi, ki = [], []
    for q in range(nq):
        for kk in (range(q, -1, -1) if reverse else range(q + 1)):
            qi.append(q)
            ki.append(kk)
    return jnp.asarray(qi, jnp.int32), jnp.asarray(ki, jnp.int32)


def _chain_init(m_sc, acc_sc):
    m_sc[...] = jnp.full_like(m_sc, NEG)
    acc_sc[...] = jnp.zeros_like(acc_sc)


def _with_ones(v_t):
    return jnp.concatenate([v_t, jnp.ones((ONES_ROWS, v_t.shape[1]), v_t.dtype)], axis=0)


def _chain_update(c, s, mask, v_t1, m_sc, acc_sc):
    if mask is not None:
        s = jnp.where(mask, s, NEG)
    m_prev = m_sc[c]
    m_new = jnp.maximum(m_prev, jnp.max(s, axis=0, keepdims=True))
    alpha = jnp.exp2(m_prev - m_new)
    p = jnp.exp2(s - m_new)
    acc_sc[c] = alpha * acc_sc[c] + _dot(v_t1, p.astype(BF16))
    m_sc[c] = m_new


def _chain_result(c, acc_sc):
    acc = acc_sc[c]
    return acc[:LANES] * (1.0 / acc[LANES:LANES + 1])


def _chain_scratch(n_chains, t):
    return [pltpu.VMEM((n_chains, 1, t), F32), pltpu.VMEM((n_chains, LANES + ONES_ROWS, t), F32)]


def _tile_causal(t):
    return lax.broadcasted_iota(jnp.int32, (t, t), 0) <= lax.broadcasted_iota(jnp.int32, (t, t), 1)


def _head_rows(ref, h):
    return ref[h * LANES:(h + 1) * LANES, :]


def _heads(ref, h):
    return ref[:, h * LANES:(h + 1) * LANES]


def _da_kernel(qt_ref, kt_ref, q_ref, k_ref, vt_ref, lam_ref, sub_ref, o_ref, m_sc, acc_sc,
               *, lam_init, t, hb):
    step = pl.program_id(2)
    qi = qt_ref[step]
    ki = kt_ref[step]

    @pl.when(ki == 0)
    def _():
        _chain_init(m_sc, acc_sc)

    lane = lax.broadcasted_iota(jnp.int32, (1, LANES), 1)
    first_map = (lane % HALF_LANES) < (HALF_LANES // 2)

    def update(mask):
        for h in range(hb):
            q = _heads(q_ref, h)
            k = _heads(k_ref, h)
            v_t1 = _with_ones(_head_rows(vt_ref, h))
            zero = jnp.zeros_like(q)
            for j, qm in enumerate((jnp.where(first_map, q, zero), jnp.where(first_map, zero, q))):
                _chain_update(2 * h + j, _dot_nt(k, qm), mask, v_t1, m_sc, acc_sc)

    @pl.when(ki < qi)
    def _():
        update(None)

    @pl.when(ki == qi)
    def _():
        update(_tile_causal(t))
        lam = lam_ref[...]
        lam_full = (jnp.exp(jnp.sum(lam[0:1] * lam[1:2], axis=-1, keepdims=True))
                    - jnp.exp(jnp.sum(lam[2:3] * lam[3:4], axis=-1, keepdims=True)) + lam_init)
        for h in range(hb):
            o = _chain_result(2 * h, acc_sc) - lam_full * _chain_result(2 * h + 1, acc_sc)
            ms = jnp.mean(o * o, axis=0, keepdims=True)
            o = o * lax.rsqrt(ms + NORM_EPS) * sub_ref[...]
            o_ref[:, h * LANES:(h + 1) * LANES] = (o * (1.0 - lam_init)).T.astype(o_ref.dtype)


def _da_column_order(heads):
    dk = DA_QK_DIM
    hd = dk // 2
    per = 4 * dk + DA_V_DIM
    q_idx, k_idx, v_idx = [], [], []
    for h in range(heads):
        base = h * per
        for out, off in ((q_idx, 0), (k_idx, 2 * dk)):
            a = base + off + np.arange(dk)
            b = base + off + dk + np.arange(dk)
            out += [a[:hd], b[:hd], a[hd:], b[hd:]]
        v_idx.append(base + 4 * dk + np.arange(DA_V_DIM))
    return np.concatenate(q_idx + k_idx + v_idx)


def _diff_attention(hres, norm_w, w_in, lam, subln_w, w_out, lam_init, batch, seq):
    m, d = hres.shape
    heads = w_in.shape[1] // (4 * DA_QK_DIM + DA_V_DIM)
    hw = heads * LANES
    col_scale = np.ones(3 * hw, np.float32)
    col_scale[:hw] = DA_QK_DIM ** -0.5 * LOG2E
    w_perm = (w_in[:, _da_column_order(heads)] * col_scale[None, :]).astype(BF16)
    cos, sin = _rope_tables(np.arange(seq), DA_QK_DIM // 2)
    tn = 512
    proj = _linear(hres, w_perm[:, :2 * hw], n_out=2 * hw, norm_w=norm_w, tn=tn,
                   rope=(cos, sin, 0, 2 * hw // LANES, seq), name="da_qk_proj")
    v_t = _linear(hres, w_perm[:, 2 * hw:], n_out=hw, norm_w=norm_w, tn=tn, transpose_out=True,
                  name="da_vt_proj")

    t = min(ATTN_TILE, seq)
    nq = seq // t
    qt, kt = _tri_tables(nq)
    hb = min(ATTN_CHAINS // 2, heads)
    hg = heads // hb
    bw = hb * LANES
    kern = functools.partial(_da_kernel, lam_init=lam_init, t=t, hb=hb)
    attn = pl.pallas_call(
        kern,
        grid_spec=pltpu.PrefetchScalarGridSpec(
            num_scalar_prefetch=2,
            grid=(batch, hg, qt.shape[0]),
            in_specs=[
                pl.BlockSpec((t, bw), lambda b, h, s, qt, kt: (b * nq + qt[s], h)),
                pl.BlockSpec((t, bw), lambda b, h, s, qt, kt: (b * nq + kt[s], hg + h)),
                pl.BlockSpec((bw, t), lambda b, h, s, qt, kt: (h, b * nq + kt[s])),
                pl.BlockSpec(lam.shape, lambda b, h, s, qt, kt: (0, 0)),
                pl.BlockSpec((DA_V_DIM, 1), lambda b, h, s, qt, kt: (0, 0)),
            ],
            out_specs=pl.BlockSpec((t, bw), lambda b, h, s, qt, kt: (b * nq + qt[s], h)),
            scratch_shapes=_chain_scratch(2 * hb, t),
        ),
        out_shape=jax.ShapeDtypeStruct((m, hw), BF16),
        compiler_params=_cparams(("parallel", "parallel", "arbitrary")),
        name="da_attention",
    )(qt, kt, proj, proj, v_t, lam.astype(F32), subln_w.reshape(DA_V_DIM, 1).astype(F32))
    return _linear(attn, w_out.astype(BF16), n_out=d, residual=hres, out_dtype=F32, name="da_out_proj")


def _mla_kernel(qt_ref, kt_ref, qn_ref, qr_ref, kn_ref, kr_ref, vt_ref, o_ref, m_sc, acc_sc, *, t, hb):
    step = pl.program_id(2)
    qi = qt_ref[step]
    ki = kt_ref[step]

    @pl.when(ki == 0)
    def _():
        _chain_init(m_sc, acc_sc)

    kr = kr_ref[...].astype(BF16)

    def update(mask):
        for h in range(hb):
            q = jnp.concatenate([_heads(qn_ref, h), _heads(qr_ref, h)], axis=1)
            k = jnp.concatenate([_heads(kn_ref, h), kr], axis=1)
            _chain_update(h, _dot_nt(k, q), mask, _with_ones(_head_rows(vt_ref, h)), m_sc, acc_sc)

    @pl.when(ki < qi)
    def _():
        update(None)

    @pl.when(ki == qi)
    def _():
        update(_tile_causal(t))
        for h in range(hb):
            o_ref[:, h * LANES:(h + 1) * LANES] = _chain_result(h, acc_sc).T.astype(o_ref.dtype)


def _mla_q_columns(heads):
    per = MLA_NOPE_DIM + MLA_ROPE_DIM
    hd = MLA_ROPE_DIM // 2
    idx, keep = [], []
    for h in range(heads):
        idx.append(h * per + np.arange(MLA_NOPE_DIM))
        keep.append(np.ones(MLA_NOPE_DIM))
    for h in range(heads):
        s = h % 2
        g_idx = np.zeros(LANES, np.int64)
        g_keep = np.zeros(LANES)
        r0 = h * per + MLA_NOPE_DIM
        g_idx[s * hd:(s + 1) * hd] = r0 + np.arange(hd)
        g_idx[HALF_LANES + s * hd:HALF_LANES + (s + 1) * hd] = r0 + hd + np.arange(hd)
        g_keep[s * hd:(s + 1) * hd] = 1.0
        g_keep[HALF_LANES + s * hd:HALF_LANES + (s + 1) * hd] = 1.0
        idx.append(g_idx)
        keep.append(g_keep)
    return np.concatenate(idx), np.concatenate(keep)


def _mla_attention(hres, norm_w, w_in, q_norm_w, w_uq, kv_norm_w, w_ukv, w_out, batch, seq):
    m, d = hres.shape
    heads = MLA_HEADS
    hw = heads * LANES
    hd = MLA_ROPE_DIM // 2
    assert MLA_Q_RANK == MLA_KV_RANK and MLA_NOPE_DIM == LANES and MLA_V_DIM == LANES
    rank = MLA_Q_RANK
    r0 = 2 * rank
    kr_cols = np.concatenate([r0 + np.arange(hd), r0 + np.arange(hd),
                              r0 + hd + np.arange(hd), r0 + hd + np.arange(hd)])
    w_in_x = jnp.concatenate([w_in[:, :2 * rank], w_in[:, kr_cols]], axis=1).astype(BF16)
    cos, sin = _rope_tables(np.arange(seq), hd)
    n_in = 2 * rank + LANES
    tn_in = 3 * LANES if n_in % (3 * LANES) == 0 else LANES
    proj = _linear(hres, w_in_x, n_out=n_in, norm_w=norm_w, tn=tn_in, out_dtype=F32,
                   rope=(cos, sin, 2 * rank // LANES, n_in // LANES, seq), name="mla_in_proj")

    q_idx, q_keep = _mla_q_columns(heads)
    c2 = (MLA_NOPE_DIM + MLA_ROPE_DIM) ** -0.5 * LOG2E
    w_uq_x = (w_uq[:, q_idx] * jnp.asarray(q_keep * c2, F32)[None, :]).astype(BF16)
    tn = 512
    qcat = _linear(proj, w_uq_x, n_out=2 * hw, k=rank, x_col_block=0, norm_w=q_norm_w, tn=tn,
                   rope=(cos, sin, hw // LANES, 2 * hw // LANES, seq), name="mla_q_proj")
    k_nope = _linear(proj, w_ukv[:, :, :MLA_NOPE_DIM].reshape(rank, hw).astype(BF16), n_out=hw, k=rank,
                     x_col_block=1, norm_w=kv_norm_w, tn=tn, name="mla_k_proj")
    v_t = _linear(proj, w_ukv[:, :, MLA_NOPE_DIM:].reshape(rank, hw).astype(BF16), n_out=hw, k=rank,
                  x_col_block=1, norm_w=kv_norm_w, tn=tn, transpose_out=True, name="mla_vt_proj")

    t = min(ATTN_TILE, seq)
    nq = seq // t
    qt, kt = _tri_tables(nq)
    kr_block = 2 * rank // LANES
    hb = min(ATTN_CHAINS, heads)
    hg = heads // hb
    bw = hb * LANES
    kern = functools.partial(_mla_kernel, t=t, hb=hb)
    attn = pl.pallas_call(
        kern,
        grid_spec=pltpu.PrefetchScalarGridSpec(
            num_scalar_prefetch=2,
            grid=(batch, hg, qt.shape[0]),
            in_specs=[
                pl.BlockSpec((t, bw), lambda b, h, s, qt, kt: (b * nq + qt[s], h)),
                pl.BlockSpec((t, bw), lambda b, h, s, qt, kt: (b * nq + qt[s], hg + h)),
                pl.BlockSpec((t, bw), lambda b, h, s, qt, kt: (b * nq + kt[s], h)),
                pl.BlockSpec((t, LANES), lambda b, h, s, qt, kt: (b * nq + kt[s], kr_block)),
                pl.BlockSpec((bw, t), lambda b, h, s, qt, kt: (h, b * nq + kt[s])),
            ],
            out_specs=pl.BlockSpec((t, bw), lambda b, h, s, qt, kt: (b * nq + qt[s], h)),
            scratch_shapes=_chain_scratch(hb, t),
        ),
        out_shape=jax.ShapeDtypeStruct((m, hw), BF16),
        compiler_params=_cparams(("parallel", "parallel", "arbitrary")),
        name="mla_attention",
    )(qt, kt, qcat, qcat, k_nope, proj, v_t)
    return _linear(attn, w_out.astype(BF16), n_out=d, residual=hres, out_dtype=F32, name="mla_out_proj")


def _sb_kernel(qt_ref, kt_ref, q_ref, k_ref, v_ref, tri_ref, o_ref, carry_sc, acc_sc, *, t, hb):
    step = pl.program_id(2)
    qi = qt_ref[step]
    ki = kt_ref[step]

    @pl.when(ki == qi)
    def _():
        carry_sc[...] = jnp.zeros_like(carry_sc)
        acc_sc[...] = jnp.zeros_like(acc_sc)

    tri = tri_ref[...]
    chunk = tri.shape[0]

    def update(masked):
        if masked:
            mask = lax.broadcasted_iota(jnp.int32, (t, t), 1) < lax.broadcasted_iota(jnp.int32, (t, t), 0)
        for h in range(hb):
            z = _dot_nt(_heads(q_ref, h), _heads(k_ref, h))
            log_beta = jnp.minimum(z, 0.0) - jnp.log2(1.0 + jnp.exp2(-jnp.abs(z)))
            log_rest = log_beta - z
            if masked:
                log_rest = jnp.where(mask, log_rest, 0.0)
            carry = carry_sc[h]
            parts = []
            for c in range(t // chunk - 1, -1, -1):
                cols = slice(c * chunk, (c + 1) * chunk)
                hi, lo = _split_bf16(log_rest[:, cols])
                tail = _dot(hi, tri) + _dot(lo, tri) + carry
                parts.append(jnp.exp2(log_beta[:, cols] + tail))
                carry = carry + jnp.sum(log_rest[:, cols], axis=-1, keepdims=True)
            a = jnp.concatenate(parts[::-1], axis=1)
            if masked:
                a = jnp.where(mask, a, 0.0)
            acc_sc[h] += _dot(a.astype(BF16), _heads(v_ref, h))
            carry_sc[h] = carry

    @pl.when(ki == qi)
    def _():
        update(True)

    @pl.when(ki < qi)
    def _():
        update(False)

    @pl.when(ki == 0)
    def _():
        for h in range(hb):
            o_ref[:, h * LANES:(h + 1) * LANES] = acc_sc[h].astype(o_ref.dtype)


def _sb_attention(hres, norm_w, w_in, w_out, batch, seq):
    m, d = hres.shape
    heads = SB_HEADS
    hw = heads * LANES
    assert w_in.shape[1] == 3 * hw
    col_scale = np.ones(3 * hw, np.float32)
    col_scale[:hw] = LANES ** -0.5 * LOG2E
    qkv = _linear(hres, (w_in * col_scale[None, :]).astype(BF16), n_out=3 * hw, norm_w=norm_w,
                  name="sb_qkv_proj")
    t = min(ATTN_TILE, seq)
    nq = seq // t
    qt, kt = _tri_tables(nq, reverse=True)
    chunk = min(2 * LANES, t)
    tri = jnp.asarray(np.arange(chunk)[:, None] > np.arange(chunk)[None, :], BF16)
    hb = min(ATTN_CHAINS, heads)
    hg = heads // hb
    bw = hb * LANES
    kern = functools.partial(_sb_kernel, t=t, hb=hb)
    attn = pl.pallas_call(
        kern,
        grid_spec=pltpu.PrefetchScalarGridSpec(
            num_scalar_prefetch=2,
            grid=(batch, hg, qt.shape[0]),
            in_specs=[
                pl.BlockSpec((t, bw), lambda b, h, s, qt, kt: (b * nq + qt[s], h)),
                pl.BlockSpec((t, bw), lambda b, h, s, qt, kt: (b * nq + kt[s], hg + h)),
                pl.BlockSpec((t, bw), lambda b, h, s, qt, kt: (b * nq + kt[s], 2 * hg + h)),
                pl.BlockSpec((chunk, chunk), lambda b, h, s, qt, kt: (0, 0)),
            ],
            out_specs=pl.BlockSpec((t, bw), lambda b, h, s, qt, kt: (b * nq + qt[s], h)),
            scratch_shapes=[pltpu.VMEM((hb, t, 1), F32), pltpu.VMEM((hb, t, LANES), F32)],
        ),
        out_shape=jax.ShapeDtypeStruct((m, hw), BF16),
        compiler_params=_cparams(("parallel", "parallel", "arbitrary")),
        name="sb_attention",
    )(qt, kt, qkv, qkv, qkv, tri)
    return _linear(attn, w_out.astype(BF16), n_out=d, residual=hres, out_dtype=F32, name="sb_out_proj")


def _nsa_compress_kernel(x_ref, w_ref, pe_ref, cos_ref, sin_ref, o_ref, *, rope, transpose_out):
    x = x_ref[0]
    half = w_ref.shape[0] // 2
    y_lo = _dot(x, w_ref[:half, :])
    y_hi = _dot(x, w_ref[half:, :])
    n = y_hi.shape[0]
    y = y_lo + pltpu.roll(y_hi, n - 1, 0) + _dot(pe_ref[...], w_ref[...])[0:1]
    if rope:
        y = y * cos_ref[...] + pltpu.roll(y, HALF_LANES, 1) * sin_ref[...]
    o_ref[0] = (y.T if transpose_out else y).astype(o_ref.dtype)


def _nsa_compress(x_chunks, w, pe, cos, sin, rope, transpose_out):
    bg, n_chunk, kdim = x_chunks.shape
    dh = w.shape[1]
    out_block = (1, dh, n_chunk) if transpose_out else (1, n_chunk, dh)
    return pl.pallas_call(
        functools.partial(_nsa_compress_kernel, rope=rope, transpose_out=transpose_out),
        grid=(bg,),
        in_specs=[pl.BlockSpec((1, n_chunk, kdim), lambda i: (i, 0, 0)),
                  pl.BlockSpec(w.shape, lambda i: (0, 0)),
                  pl.BlockSpec(pe.shape, lambda i: (0, 0)),
                  pl.BlockSpec(cos.shape, lambda i: (0, 0)),
                  pl.BlockSpec(sin.shape, lambda i: (0, 0))],
        out_specs=pl.BlockSpec(out_block, lambda i: (i, 0, 0)),
        out_shape=jax.ShapeDtypeStruct((bg,) + out_block[1:], BF16),
        compiler_params=_cparams(("parallel",)),
        name="nsa_compress",
    )(x_chunks, w, pe, cos, sin)


def _head_gate(gates, branch, head):
    lane = lax.broadcasted_iota(jnp.int32, gates.shape, 1)
    return jnp.sum(jnp.where(lane == branch * NSA_HEADS + head, gates, 0.0), axis=-1, keepdims=True)


def _nsa_cmp_kernel(q_ref, kc_ref, vct_ref, ovt_ref, g_ref, o_ref, sel_ref, *, tq, rep, n_sel):
    qi = pl.program_id(2)
    grp = pl.program_id(1)
    qs = jnp.concatenate([q_ref[:, r * LANES:(r + 1) * LANES] for r in range(rep)], axis=0)
    kc = kc_ref[0]
    n_cmp = kc.shape[0]
    rows = rep * tq

    st = _dot_nt(kc, qs)
    qpos_t = qi * tq + lax.broadcasted_iota(jnp.int32, (n_cmp, rep, tq), 2).reshape(n_cmp, rows)
    cend_t = lax.broadcasted_iota(jnp.int32, (n_cmp, rows), 0) * NSA_CMP_STRIDE + (NSA_CMP_BLOCK - 1)
    mask_t = cend_t <= qpos_t
    st = jnp.where(mask_t, st, NEG)
    pt = jnp.where(mask_t, jnp.exp2(st - jnp.max(st, axis=0, keepdims=True)), 0.0)
    den_t = jnp.sum(pt, axis=0, keepdims=True)
    pt = pt / jnp.where(den_t > 0.0, den_t, 1.0)
    o_t = _dot(vct_ref[0], pt.astype(BF16))
    gates = g_ref[...]
    for r in range(rep):
        gate = _head_gate(gates, 0, grp * rep + r)
        o_ref[:, r * LANES:(r + 1) * LANES] = gate * o_t[:, r * tq:(r + 1) * tq].T

    psum = pt[:, 0:tq]
    for r in range(1, rep):
        psum = psum + pt[:, r * tq:(r + 1) * tq]
    hi, lo = _split_bf16(psum)
    ovt = ovt_ref[...]
    imp = _dot(ovt, hi) + _dot(ovt, lo)

    n_pad = imp.shape[0]
    blk = lax.broadcasted_iota(jnp.int32, (n_pad, tq), 0)
    cur = (qi * tq + lax.broadcasted_iota(jnp.int32, (n_pad, tq), 1)) // NSA_SLC_BLOCK
    forced = (blk == 0) | (blk == cur) | (blk == cur - 1)
    val = jnp.where(forced, jnp.inf, jnp.where(blk <= cur, imp, -jnp.inf))
    sel = jnp.zeros((n_pad, tq), F32)
    for _ in range(n_sel):
        top = jnp.max(val, axis=0, keepdims=True)
        first = jnp.min(jnp.where(val == top, blk, n_pad), axis=0, keepdims=True)
        pick = blk == first
        sel = jnp.where(pick, 1.0, sel)
        val = jnp.where(pick, -jnp.inf, val)
    sel_ref[...] = sel.astype(sel_ref.dtype)


def _nsa_win_kernel(q_ref, k_ref, vt_ref, g_ref, prev_ref, o_ref, m_sc, acc_sc, *, t, rep, nwin):
    qi = pl.program_id(2)
    kk = pl.program_id(3)
    grp = pl.program_id(1)
    kb = qi - (nwin - 1) + kk

    @pl.when(kk == 0)
    def _():
        _chain_init(m_sc, acc_sc)

    def update(mask):
        k = k_ref[...]
        v_t1 = _with_ones(vt_ref[...])
        for r in range(rep):
            _chain_update(r, _dot_nt(k, _heads(q_ref, r)), mask, v_t1, m_sc, acc_sc)

    @pl.when((kk == 0) & (kb >= 0))
    def _():
        update(jnp.logical_not(_tile_causal(t)))

    @pl.when((kk > 0) & (kk < nwin - 1) & (kb >= 0))
    def _():
        update(None)

    @pl.when(kk == nwin - 1)
    def _():
        update(_tile_causal(t))
        gates = g_ref[...]
        for r in range(rep):
            gate = _head_gate(gates, 2, grp * rep + r)
            o_ref[:, r * LANES:(r + 1) * LANES] = _heads(prev_ref, r) + gate * _chain_result(r, acc_sc).T


def _nsa_slc_kernel(qt_ref, kt_ref, q_ref, k_ref, vt_ref, sel_ref, g_ref, prev_ref, o_ref,
                    m_sc, acc_sc, *, t, rep, gb):
    step = pl.program_id(2)
    grp0 = pl.program_id(1) * gb
    qi = qt_ref[step]
    ki = kt_ref[step]

    @pl.when(ki == 0)
    def _():
        _chain_init(m_sc, acc_sc)

    n_pad = sel_ref.shape[0] // gb
    blk = lax.broadcasted_iota(jnp.int32, (t, n_pad), 1)
    kblk = (ki * t + lax.broadcasted_iota(jnp.int32, (t, n_pad), 0)) // NSA_SLC_BLOCK
    expand = jnp.where(blk == kblk, 1.0, 0.0).astype(BF16)

    def update(diagonal):
        for g in range(gb):
            mask = _dot(expand, sel_ref[g * n_pad:(g + 1) * n_pad, :]) > 0.5
            if diagonal:
                mask = mask & _tile_causal(t)
            k = _heads(k_ref, g)
            v_t1 = _with_ones(_head_rows(vt_ref, g))
            for r in range(rep):
                c = g * rep + r
                _chain_update(c, _dot_nt(k, _heads(q_ref, c)), mask, v_t1, m_sc, acc_sc)

    @pl.when(ki < qi)
    def _():
        update(False)

    @pl.when(ki == qi)
    def _():
        update(True)
        gates = g_ref[...]
        for c in range(gb * rep):
            gate = _head_gate(gates, 1, grp0 * rep + c)
            o = _chain_result(c, acc_sc).T
            o_ref[:, c * LANES:(c + 1) * LANES] = (_heads(prev_ref, c) + gate * o).astype(o_ref.dtype)


def _nsa_attention(hres, norm_w, w_in, cmp_pe, cmp_w, w_out, batch, seq):
    m, d = hres.shape
    heads, groups = NSA_HEADS, NSA_GROUPS
    rep = heads // groups
    dh = d // heads
    assert dh == LANES
    kvw = groups * dh
    n_main = heads * dh + 6 * kvw
    gw = rep * LANES

    hq = heads * dh
    cos, sin = _rope_tables(np.arange(seq), dh // 2)
    kv_cols = lambda i: np.arange(hq + i * kvw, hq + (i + 1) * kvw)
    main_cols = np.concatenate([np.arange(hq), kv_cols(0), kv_cols(1), kv_cols(2), kv_cols(4)])
    col_scale = np.ones(main_cols.size, np.float32)
    col_scale[:hq] = dh ** -0.5 * LOG2E
    proj = _linear(hres, (w_in[:, main_cols] * col_scale[None, :]).astype(BF16), n_out=main_cols.size,
                   norm_w=norm_w, rope=(cos, sin, 0, hq // LANES, seq), name="nsa_qk_proj")
    v_t = _linear(hres, w_in[:, np.concatenate([kv_cols(3), kv_cols(5)])].astype(BF16), n_out=2 * kvw,
                  norm_w=norm_w, transpose_out=True, name="nsa_vt_proj")
    w_gate = jnp.pad(w_in[:, n_main:], ((0, 0), (0, LANES - 3 * heads))).astype(BF16)
    gates = _linear(hres, w_gate, n_out=LANES, norm_w=norm_w, act="sigmoid", out_dtype=F32,
                    name="nsa_gate_proj")
    k_slc_blk = (hq + 2 * kvw) // LANES
    k_win_blk = (hq + 3 * kvw) // LANES

    n_chunk = seq // NSA_CMP_STRIDE
    assert NSA_CMP_BLOCK == 2 * NSA_CMP_STRIDE
    cmp_end = np.arange(n_chunk) * NSA_CMP_STRIDE + NSA_CMP_BLOCK - 1
    ccos, csin = _rope_tables(cmp_end, dh // 2)

    def chunks(i):
        lo = hq + i * kvw
        x = proj[:, lo:lo + kvw].reshape(batch, n_chunk, NSA_CMP_STRIDE, groups, dh)
        return x.transpose(0, 3, 1, 2, 4).reshape(batch * groups, n_chunk, NSA_CMP_STRIDE * dh)

    def cmp_weights(i):
        return (cmp_w[i].reshape(NSA_CMP_BLOCK * dh, dh).astype(BF16),
                jnp.broadcast_to(cmp_pe[i].reshape(1, NSA_CMP_BLOCK * dh), (8, NSA_CMP_BLOCK * dh)).astype(BF16))

    k_cmp = _nsa_compress(chunks(0), *cmp_weights(0), ccos, csin, True, False)
    v_cmp_t = _nsa_compress(chunks(1), *cmp_weights(1), ccos, csin, False, True)

    n_slc = seq // NSA_SLC_BLOCK
    n_sel = min(NSA_SLC_TOPK, n_slc)
    n_pad = max(LANES, n_slc)
    n_cmp = n_chunk - NSA_CMP_BLOCK // NSA_CMP_STRIDE + 1
    cs = np.arange(n_chunk)[None, :] * NSA_CMP_STRIDE
    js = np.arange(n_pad)[:, None] * NSA_SLC_BLOCK
    ovt = ((cs < js + NSA_SLC_BLOCK) & (cs + NSA_CMP_BLOCK > js)
           & (np.arange(n_chunk)[None, :] < n_cmp) & (np.arange(n_pad)[:, None] < n_slc))
    ovt = jnp.asarray(ovt, BF16)

    tq = min(256, seq)
    nqc = seq // tq
    o_c, sel = pl.pallas_call(
        functools.partial(_nsa_cmp_kernel, tq=tq, rep=rep, n_sel=n_sel),
        grid=(batch, groups, nqc),
        in_specs=[
            pl.BlockSpec((tq, gw), lambda b, g, i: (b * nqc + i, g)),
            pl.BlockSpec((1, n_chunk, dh), lambda b, g, i: (b * groups + g, 0, 0)),
            pl.BlockSpec((1, dh, n_chunk), lambda b, g, i: (b * groups + g, 0, 0)),
            pl.BlockSpec(ovt.shape, lambda b, g, i: (0, 0)),
            pl.BlockSpec((tq, LANES), lambda b, g, i: (b * nqc + i, 0)),
        ],
        out_specs=[
            pl.BlockSpec((tq, gw), lambda b, g, i: (b * nqc + i, g)),
            pl.BlockSpec((n_pad, tq), lambda b, g, i: (b * groups + g, i)),
        ],
        out_shape=[jax.ShapeDtypeStruct((m, heads * dh), F32),
                   jax.ShapeDtypeStruct((batch * groups * n_pad, seq), BF16)],
        compiler_params=_cparams(("parallel", "parallel", "arbitrary")),
        name="nsa_compressed_select",
    )(proj, k_cmp, v_cmp_t, ovt, gates)

    t = min(512, seq)
    nq = seq // t
    assert NSA_WINDOW % t == 0
    nwin = NSA_WINDOW // t + 1
    key_tile = lambda i, kk: jnp.maximum(i - (nwin - 1) + kk, 0)
    o_cw = pl.pallas_call(
        functools.partial(_nsa_win_kernel, t=t, rep=rep, nwin=nwin),
        grid=(batch, groups, nq, nwin),
        in_specs=[
            pl.BlockSpec((t, gw), lambda b, g, i, kk: (b * nq + i, g)),
            pl.BlockSpec((t, LANES), lambda b, g, i, kk: (b * nq + key_tile(i, kk), k_win_blk + g)),
            pl.BlockSpec((LANES, t), lambda b, g, i, kk: (groups + g, b * nq + key_tile(i, kk))),
            pl.BlockSpec((t, LANES), lambda b, g, i, kk: (b * nq + i, 0)),
            pl.BlockSpec((t, gw), lambda b, g, i, kk: (b * nq + i, g)),
        ],
        out_specs=pl.BlockSpec((t, gw), lambda b, g, i, kk: (b * nq + i, g)),
        out_shape=jax.ShapeDtypeStruct((m, heads * dh), F32),
        scratch_shapes=_chain_scratch(rep, t),
        compiler_params=_cparams(("parallel", "parallel", "parallel", "arbitrary")),
        name="nsa_window",
    )(proj, proj, v_t, gates, o_c)

    ts = min(ATTN_TILE, seq)
    nqs = seq // ts
    qt, kt = _tri_tables(nqs)
    gb = max(1, ATTN_CHAINS // rep)
    assert groups % gb == 0 and k_slc_blk % gb == 0
    gsteps = groups // gb
    attn = pl.pallas_call(
        functools.partial(_nsa_slc_kernel, t=ts, rep=rep, gb=gb),
        grid_spec=pltpu.PrefetchScalarGridSpec(
            num_scalar_prefetch=2,
            grid=(batch, gsteps, qt.shape[0]),
            in_specs=[
                pl.BlockSpec((ts, gb * gw), lambda b, g, s, qt, kt: (b * nqs + qt[s], g)),
                pl.BlockSpec((ts, gb * LANES), lambda b, g, s, qt, kt: (b * nqs + kt[s], k_slc_blk // gb + g)),
                pl.BlockSpec((gb * LANES, ts), lambda b, g, s, qt, kt: (g, b * nqs + kt[s])),
                pl.BlockSpec((gb * n_pad, ts), lambda b, g, s, qt, kt: (b * gsteps + g, qt[s])),
                pl.BlockSpec((ts, LANES), lambda b, g, s, qt, kt: (b * nqs + qt[s], 0)),
                pl.BlockSpec((ts, gb * gw), lambda b, g, s, qt, kt: (b * nqs + qt[s], g)),
            ],
            out_specs=pl.BlockSpec((ts, gb * gw), lambda b, g, s, qt, kt: (b * nqs + qt[s], g)),
            scratch_shapes=_chain_scratch(gb * rep, ts),
        ),
        out_shape=jax.ShapeDtypeStruct((m, heads * dh), BF16),
        compiler_params=_cparams(("parallel", "parallel", "arbitrary")),
        name="nsa_selected",
    )(qt, kt, proj, proj, v_t, sel, gates, o_cw)
    return _linear(attn, w_out.astype(BF16), n_out=d, residual=hres, out_dtype=F32, name="nsa_out_proj")


def _swiglu(hres, norm_w, w_gu, w_down):
    d = hres.shape[1]
    ff = w_down.shape[0]
    tn = 512
    hidden = _linear(hres, w_gu.astype(BF16), n_out=ff, w_tile_offsets=(0, ff // tn), norm_w=norm_w,
                     act="swiglu", tn=tn, name="ffn_gate_up")
    return _linear(hidden, w_down.astype(BF16), n_out=d, residual=hres, out_dtype=F32, name="ffn_down")


def _router_kernel(x_ref, nw_ref, w_ref, b_ref, f_ref, idx_ref, gate_ref, *, n_experts):
    xf = x_ref[...]
    ms = jnp.mean(xf * xf, axis=-1, keepdims=True)
    f = xf * lax.rsqrt(ms + NORM_EPS) * nw_ref[...]
    f_ref[...] = f
    fh, fl = _split_bf16(f)
    w = w_ref[...]
    wh, wl = _split_bf16(w)
    logits = _dot(fh, wh) + _dot(fl, wh) + _dot(fh, wl) + b_ref[...]
    lane = lax.broadcasted_iota(jnp.int32, logits.shape, 1)
    logits = jnp.where(lane < n_experts, logits, -jnp.inf)
    v1 = jnp.max(logits, axis=-1, keepdims=True)
    i1 = jnp.min(jnp.where(logits == v1, lane, LANES), axis=-1, keepdims=True)
    rest = jnp.where(lane == i1, -jnp.inf, logits)
    v2 = jnp.max(rest, axis=-1, keepdims=True)
    i2 = jnp.min(jnp.where(rest == v2, lane, LANES), axis=-1, keepdims=True)
    e2 = jnp.exp(v2 - v1)
    g1 = 1.0 / (1.0 + e2)
    g2 = e2 / (1.0 + e2)
    idx_ref[...] = jnp.where(lane == 0, i1, jnp.where(lane == 1, i2, 0))
    gate_ref[...] = jnp.where(lane == 0, g1, jnp.where(lane == 1, g2, 0.0))


def _gather_kernel(src_ref, nvalid_ref, x_hbm, o_ref, buf, sem):
    blk = pl.program_id(0)
    rows = buf.shape[0]
    nvalid = nvalid_ref[blk]

    def row_copy(r):
        return pltpu.make_async_copy(x_hbm.at[pl.ds(src_ref[blk * rows + r], 1), :],
                                     buf.at[pl.ds(r, 1), :], sem)

    def start(r, c):
        row_copy(r).start()
        return c

    def wait(r, c):
        row_copy(r).wait()
        return c

    @pl.when(nvalid > 0)
    def _():
        lax.fori_loop(0, rows, start, 0, unroll=DMA_UNROLL)
        lax.fori_loop(0, rows, wait, 0, unroll=DMA_UNROLL)
        rid = lax.broadcasted_iota(jnp.int32, (rows, 1), 0)
        o_ref[...] = jnp.where(rid < nvalid, buf[...], 0.0).astype(o_ref.dtype)

    @pl.when(nvalid == 0)
    def _():
        o_ref[...] = jnp.zeros_like(o_ref)


def _weight_tile_is_new(be_ref, blk):
    return (blk == 0) | (be_ref[blk] != be_ref[jnp.maximum(blk - 1, 0)])


def _expert_up_kernel(be_ref, nv_ref, x_ref, wg_ref, wu_ref, o_ref, wg_sc, wu_sc):
    blk = pl.program_id(1)

    @pl.when(_weight_tile_is_new(be_ref, blk))
    def _():
        wg_sc[...] = wg_ref[0, 0].astype(BF16)
        wu_sc[...] = wu_ref[0, 0].astype(BF16)

    @pl.when(nv_ref[blk] > 0)
    def _():
        x = x_ref[...]
        g = _dot(x, wg_sc[...])
        u = _dot(x, wu_sc[...])
        o_ref[...] = (g * jax.nn.sigmoid(g) * u).astype(o_ref.dtype)

    @pl.when(nv_ref[blk] == 0)
    def _():
        o_ref[...] = jnp.zeros_like(o_ref)


def _expert_down_kernel(be_ref, nv_ref, h_ref, w_ref, o_ref, w_sc):
    blk = pl.program_id(1)

    @pl.when(_weight_tile_is_new(be_ref, blk))
    def _():
        w_sc[...] = w_ref[0, 0].astype(BF16)

    @pl.when(nv_ref[blk] > 0)
    def _():
        o_ref[...] = _dot(h_ref[...], w_sc[...])

    @pl.when(nv_ref[blk] == 0)
    def _():
        o_ref[...] = jnp.zeros_like(o_ref)


def _combine_kernel(dst_ref, y_hbm, res_ref, gate_ref, o_ref, buf0, buf1, sem):
    i = pl.program_id(0)
    rows = buf0.shape[0]

    def row_copy(r, slot, buf):
        return pltpu.make_async_copy(y_hbm.at[pl.ds(dst_ref[(i * rows + r) * TOP_K + slot], 1), :],
                                     buf.at[pl.ds(r, 1), :], sem.at[slot])

    def start(r, c):
        row_copy(r, 0, buf0).start()
        row_copy(r, 1, buf1).start()
        return c

    def wait(r, c):
        row_copy(r, 0, buf0).wait()
        row_copy(r, 1, buf1).wait()
        return c

    lax.fori_loop(0, rows, start, 0, unroll=DMA_UNROLL)
    lax.fori_loop(0, rows, wait, 0, unroll=DMA_UNROLL)
    gate = gate_ref[...]
    o_ref[...] = res_ref[...] + (buf0[...] * gate[:, 0:1] + buf1[...] * gate[:, 1:2])


def _moe(hres, norm_w, w_router, b_router, w_gu, w_down, layer):
    m, d = hres.shape
    n_exp = w_router.shape[1]
    ff = w_down.shape[2]
    rb = MOE_ROW_BLOCK
    assert TOP_K == 2 and m % rb == 0
    tm = min(512, m)

    w_r = jnp.pad(w_router, ((0, 0), (0, LANES - n_exp))).astype(F32)
    b_r = jnp.pad(b_router, (0, LANES - n_exp)).reshape(1, LANES).astype(F32)
    f, idx, gate = pl.pallas_call(
        functools.partial(_router_kernel, n_experts=n_exp),
        grid=(m // tm,),
        in_specs=[pl.BlockSpec((tm, d), lambda i: (i, 0)),
                  pl.BlockSpec((1, d), lambda i: (0, 0)),
                  pl.BlockSpec((d, LANES), lambda i: (0, 0)),
                  pl.BlockSpec((1, LANES), lambda i: (0, 0))],
        out_specs=[pl.BlockSpec((tm, d), lambda i: (i, 0)),
                   pl.BlockSpec((tm, LANES), lambda i: (i, 0)),
                   pl.BlockSpec((tm, LANES), lambda i: (i, 0))],
        out_shape=[jax.ShapeDtypeStruct((m, d), F32),
                   jax.ShapeDtypeStruct((m, LANES), jnp.int32),
                   jax.ShapeDtypeStruct((m, LANES), F32)],
        compiler_params=_cparams(("parallel",)),
        name="moe_router",
    )(hres, norm_w.reshape(1, d).astype(F32), w_r, b_r)

    nk = m * TOP_K
    e_flat = idx[:, :TOP_K].reshape(nk)
    onehot = (e_flat[:, None] == jnp.arange(n_exp, dtype=jnp.int32)[None, :]).astype(jnp.int32)
    csum = jnp.cumsum(onehot, axis=0)
    rank = jnp.sum((csum - onehot) * onehot, axis=1)
    counts = csum[-1]
    padded = (counts + rb - 1) // rb * rb
    pend = jnp.cumsum(padded)
    pstart = pend - padded
    dest = (pstart[e_flat] + rank).astype(jnp.int32)
    n_rows = (-(-nk // rb)) * rb + n_exp * rb
    n_blocks = n_rows // rb
    src_tok = jnp.zeros((n_rows,), jnp.int32).at[dest].set(jnp.arange(nk, dtype=jnp.int32) // TOP_K)
    blk_start = jnp.arange(n_blocks, dtype=jnp.int32) * rb
    blk_expert = jnp.minimum(jnp.searchsorted(pend, blk_start, side="right"), n_exp - 1).astype(jnp.int32)
    nvalid = jnp.clip(pstart[blk_expert] + counts[blk_expert] - blk_start, 0, rb).astype(jnp.int32)

    xbuf = pl.pallas_call(
        _gather_kernel,
        grid_spec=pltpu.PrefetchScalarGridSpec(
            num_scalar_prefetch=2,
            grid=(n_blocks,),
            in_specs=[pl.BlockSpec(memory_space=pl.ANY)],
            out_specs=pl.BlockSpec((rb, d), lambda i, src, nv: (i, 0)),
            scratch_shapes=[pltpu.VMEM((rb, d), F32), pltpu.SemaphoreType.DMA(())],
        ),
        out_shape=jax.ShapeDtypeStruct((n_rows, d), BF16),
        compiler_params=_cparams(("arbitrary",)),
        name="moe_gather",
    )(src_tok, nvalid, f)

    tn = 512
    hidden = pl.pallas_call(
        _expert_up_kernel,
        grid_spec=pltpu.PrefetchScalarGridSpec(
            num_scalar_prefetch=2,
            grid=(ff // tn, n_blocks),
            in_specs=[pl.BlockSpec((rb, d), lambda j, i, be, nv: (i, 0)),
                      pl.BlockSpec((1, 1, d, tn), lambda j, i, be, nv: (layer, be[i], 0, j)),
                      pl.BlockSpec((1, 1, d, tn), lambda j, i, be, nv: (layer, be[i], 0, j + ff // tn))],
            out_specs=pl.BlockSpec((rb, tn), lambda j, i, be, nv: (i, j)),
            scratch_shapes=[pltpu.VMEM((d, tn), BF16), pltpu.VMEM((d, tn), BF16)],
        ),
        out_shape=jax.ShapeDtypeStruct((n_rows, ff), BF16),
        compiler_params=_cparams(("arbitrary", "arbitrary")),
        name="moe_expert_up",
    )(blk_expert, nvalid, xbuf, w_gu, w_gu)

    tnd = min(512, d)
    ybuf = pl.pallas_call(
        _expert_down_kernel,
        grid_spec=pltpu.PrefetchScalarGridSpec(
            num_scalar_prefetch=2,
            grid=(d // tnd, n_blocks),
            in_specs=[pl.BlockSpec((rb, ff), lambda j, i, be, nv: (i, 0)),
                      pl.BlockSpec((1, 1, ff, tnd), lambda j, i, be, nv: (layer, be[i], 0, j))],
            out_specs=pl.BlockSpec((rb, tnd), lambda j, i, be, nv: (i, j)),
            scratch_shapes=[pltpu.VMEM((ff, tnd), BF16)],
        ),
        out_shape=jax.ShapeDtypeStruct((n_rows, d), F32),
        compiler_params=_cparams(("arbitrary", "arbitrary")),
        name="moe_expert_down",
    )(blk_expert, nvalid, hidden, w_down)

    return pl.pallas_call(
        _combine_kernel,
        grid_spec=pltpu.PrefetchScalarGridSpec(
            num_scalar_prefetch=1,
            grid=(m // tm,),
            in_specs=[pl.BlockSpec(memory_space=pl.ANY),
                      pl.BlockSpec((tm, d), lambda i, dst: (i, 0)),
                      pl.BlockSpec((tm, LANES), lambda i, dst: (i, 0))],
            out_specs=pl.BlockSpec((tm, d), lambda i, dst: (i, 0)),
            scratch_shapes=[pltpu.VMEM((tm, d), F32), pltpu.VMEM((tm, d), F32),
                            pltpu.SemaphoreType.DMA((TOP_K,))],
        ),
        out_shape=jax.ShapeDtypeStruct((m, d), F32),
        compiler_params=_cparams(("arbitrary",)),
        name="moe_combine",
    )(dest, ybuf, hres, gate)


def kernel(x, attn_norm_w, ffn_norm_w, final_norm_w, da_w_in, da_lambda, da_subln_w, da_w_out, nsa_w_in, nsa_cmp_pe, nsa_cmp_w, nsa_w_out, mla_w_in, mla_q_norm_w, mla_w_uq, mla_kv_norm_w, mla_w_ukv, mla_w_out, sb_w_in, sb_w_out, ffn_w_gu, ffn_w_down, moe_w_router, moe_b_router, moe_w_gu, moe_w_down):
    batch, seq, d = x.shape
    depth = attn_norm_w.shape[0]
    h = x.reshape(batch * seq, d)
    for i in range(depth):
        kind = i % 4
        j = i // 4
        if kind == 0:
            lam_init = 0.8 - 0.6 * math.exp(-0.3 * i)
            h = _diff_attention(h, attn_norm_w[i], da_w_in[j], da_lambda[j], da_subln_w[j], da_w_out[j],
                                lam_init, batch, seq)
        elif kind == 1:
            h = _nsa_attention(h, attn_norm_w[i], nsa_w_in[j], nsa_cmp_pe[j], nsa_cmp_w[j], nsa_w_out[j],
                               batch, seq)
        elif kind == 2:
            h = _mla_attention(h, attn_norm_w[i], mla_w_in[j], mla_q_norm_w[j], mla_w_uq[j],
                               mla_kv_norm_w[j], mla_w_ukv[j], mla_w_out[j], batch, seq)
        else:
            h = _sb_attention(h, attn_norm_w[i], sb_w_in[j], sb_w_out[j], batch, seq)
        if i % 2 == 0:
            h = _swiglu(h, ffn_norm_w[i], ffn_w_gu[i // 2], ffn_w_down[i // 2])
        else:
            h = _moe(h, ffn_norm_w[i], moe_w_router[i // 2], moe_b_router[i // 2],
                     moe_w_gu, moe_w_down, i // 2)
    return _rmsnorm(h, final_norm_w.astype(F32)).reshape(batch, seq, d)
```

```python
import functools
import math

import numpy as np
import jax
import jax.numpy as jnp
from jax import lax
from jax.experimental import pallas as pl
from jax.experimental.pallas import tpu as pltpu

ROPE_THETA = 10000.0
NORM_EPS = 1e-6

DA_QK_DIM = 64
DA_V_DIM = 2 * DA_QK_DIM

NSA_HEADS = 16
NSA_GROUPS = 4
NSA_CMP_BLOCK = 32
NSA_CMP_STRIDE = 16
NSA_SLC_BLOCK = 64
NSA_SLC_TOPK = 16
NSA_WINDOW = 512

MLA_HEADS = 16
MLA_Q_RANK = 512
MLA_KV_RANK = 512
MLA_NOPE_DIM = 128
MLA_ROPE_DIM = 64
MLA_V_DIM = 128

SB_HEADS = 16

N_EXPERTS = 8
TOP_K = 2
MOE_ROW_BLOCK = 512

LANES = 128
HALF_LANES = LANES // 2
V7X_VMEM_BYTES = 64 * 1024 * 1024
VMEM_LIMIT = V7X_VMEM_BYTES * 7 // 8
NEG = -1e30
LOG2E = math.log2(math.e)
ONES_ROWS = 16
DMA_UNROLL = 8
ATTN_TILE = 1024
ATTN_CHAINS = 8
SB_HEADS_PER_STEP = 4

F32 = jnp.float32
BF16 = jnp.bfloat16


def _cparams(sem):
    return pltpu.CompilerParams(dimension_semantics=sem, vmem_limit_bytes=VMEM_LIMIT)


def _dot(a, b):
    return jnp.dot(a, b, preferred_element_type=F32)


def _dot_nt(a, b):
    return lax.dot_general(a, b, (((1,), (1,)), ((), ())), preferred_element_type=F32)


def _split_bf16(x):
    hi = x.astype(BF16)
    lo = (x - hi.astype(F32)).astype(BF16)
    return hi, lo


def _linear_kernel(*refs, has_norm, prologue, n_w, has_res, rope, act, transpose_out):
    it = iter(refs)
    x_ref = next(it)
    nw_ref = next(it) if has_norm else None
    w_refs = [next(it) for _ in range(n_w)]
    cos_ref = next(it) if rope else None
    sin_ref = next(it) if rope else None
    res_ref = next(it) if has_res else None
    o_ref = next(it)
    xs_ref = next(it) if prologue else None
    j = pl.program_id(1)

    if prologue:
        @pl.when(j == 0)
        def _():
            xf = x_ref[...].astype(F32)
            if has_norm:
                ms = jnp.mean(xf * xf, axis=-1, keepdims=True)
                xf = xf * lax.rsqrt(ms + NORM_EPS) * nw_ref[...]
            xs_ref[...] = xf.astype(BF16)
        xb = xs_ref[...]
    else:
        xb = x_ref[...]

    y = _dot(xb, w_refs[0][...])
    if act == "swiglu":
        u = _dot(xb, w_refs[1][...])
        y = y * jax.nn.sigmoid(y) * u
    elif act == "sigmoid":
        y = jax.nn.sigmoid(y)
    if has_res:
        y = y + res_ref[...]

    if rope:
        lo, hi = rope
        groups = y.shape[1] // LANES
        g0 = j * groups
        tile_has_rope = (g0 < hi) & (g0 + groups > lo)

        @pl.when(tile_has_rope)
        def _():
            cos = cos_ref[...]
            sin = sin_ref[...]
            for c in range(groups):
                yc = y[:, c * LANES:(c + 1) * LANES]
                roped = yc * cos + pltpu.roll(yc, HALF_LANES, 1) * sin
                use = (g0 + c >= lo) & (g0 + c < hi)
                o_ref[:, c * LANES:(c + 1) * LANES] = jnp.where(use, roped, yc).astype(o_ref.dtype)

        @pl.when(jnp.logical_not(tile_has_rope))
        def _():
            o_ref[...] = y.astype(o_ref.dtype)
    elif transpose_out:
        o_ref[...] = y.T.astype(o_ref.dtype)
    else:
        o_ref[...] = y.astype(o_ref.dtype)


def _linear(x, w, *, n_out, k=None, x_col_block=0, w_tile_offsets=(0,), norm_w=None, residual=None,
            rope=None, act=None, out_dtype=None, tm=1024, tn=512, transpose_out=False, name="linear"):
    m = x.shape[0]
    k = x.shape[1] if k is None else k
    tm = min(tm, m)
    tn = min(tn, n_out)
    assert m % tm == 0 and n_out % tn == 0 and w.shape[0] == k
    out_dtype = BF16 if out_dtype is None else out_dtype
    has_norm = norm_w is not None
    prologue = has_norm or x.dtype != BF16
    grid = (m // tm, n_out // tn)

    in_specs = [pl.BlockSpec((tm, k), lambda i, j: (i, x_col_block))]
    args = [x]
    if has_norm:
        in_specs.append(pl.BlockSpec((1, k), lambda i, j: (0, 0)))
        args.append(norm_w.reshape(1, k).astype(F32))
    for off in w_tile_offsets:
        in_specs.append(pl.BlockSpec((k, tn), lambda i, j, off=off: (0, j + off)))
        args.append(w)
    rope_range = None
    if rope is not None:
        cos, sin, lo, hi, seq = rope
        assert seq % tm == 0
        nrep = seq // tm
        for tab in (cos, sin):
            in_specs.append(pl.BlockSpec((tm, LANES), lambda i, j: (i % nrep, 0)))
            args.append(tab)
        rope_range = (lo, hi)
    if residual is not None:
        in_specs.append(pl.BlockSpec((tm, tn), lambda i, j: (i, j)))
        args.append(residual)
    scratch = [pltpu.VMEM((tm, k), BF16)] if prologue else []

    kern = functools.partial(_linear_kernel, has_norm=has_norm, prologue=prologue,
                             n_w=len(w_tile_offsets), has_res=residual is not None,
                             rope=rope_range, act=act, transpose_out=transpose_out)
    if transpose_out:
        assert rope is None
        out_specs = pl.BlockSpec((tn, tm), lambda i, j: (j, i))
        out_shape = jax.ShapeDtypeStruct((n_out, m), out_dtype)
    else:
        out_specs = pl.BlockSpec((tm, tn), lambda i, j: (i, j))
        out_shape = jax.ShapeDtypeStruct((m, n_out), out_dtype)
    return pl.pallas_call(
        kern,
        grid=grid,
        in_specs=in_specs,
        out_specs=out_specs,
        out_shape=out_shape,
        scratch_shapes=scratch,
        compiler_params=_cparams(("parallel", "arbitrary")),
        name=name,
    )(*args)


def _rmsnorm_kernel(x_ref, w_ref, o_ref):
    xf = x_ref[...]
    ms = jnp.mean(xf * xf, axis=-1, keepdims=True)
    o_ref[...] = xf * lax.rsqrt(ms + NORM_EPS) * w_ref[...]


def _rmsnorm(x, w, tm=512):
    m, d = x.shape
    tm = min(tm, m)
    return pl.pallas_call(
        _rmsnorm_kernel,
        grid=(m // tm,),
        in_specs=[pl.BlockSpec((tm, d), lambda i: (i, 0)), pl.BlockSpec((1, d), lambda i: (0, 0))],
        out_specs=pl.BlockSpec((tm, d), lambda i: (i, 0)),
        out_shape=jax.ShapeDtypeStruct((m, d), F32),
        compiler_params=_cparams(("parallel",)),
    )(x, w.reshape(1, d))


def _rope_tables(pos, half):
    inv_freq = ROPE_THETA ** (-np.arange(half, dtype=np.float64) / half)
    ang = np.asarray(pos, np.float64)[:, None] * inv_freq[None, :]
    reps = LANES // half
    cos = np.tile(np.cos(ang), (1, reps))
    sin = np.tile(np.sin(ang), (1, reps))
    sign = np.where(np.arange(LANES) < HALF_LANES, -1.0, 1.0)
    return jnp.asarray(cos, F32), jnp.asarray(sin * sign, F32)


def _tri_tables(nq, reverse=False):
    qi, ki = [], []
    for q in range(nq):
        for kk in (range(q, -1, -1) if reverse else range(q + 1)):
            qi.append(q)
            ki.append(kk)
    return jnp.asarray(qi, jnp.int32), jnp.asarray(ki, jnp.int32)


def _chain_init(m_sc, acc_sc):
    m_sc[...] = jnp.full_like(m_sc, NEG)
    acc_sc[...] = jnp.zeros_like(acc_sc)


def _with_ones(v_t):
    return jnp.concatenate([v_t, jnp.ones((ONES_ROWS, v_t.shape[1]), v_t.dtype)], axis=0)


def _chain_update(c, s, mask, v_t1, m_sc, acc_sc):
    if mask is not None:
        s = jnp.where(mask, s, NEG)
    m_prev = m_sc[c]
    m_new = jnp.maximum(m_prev, jnp.max(s, axis=0, keepdims=True))
    alpha = jnp.exp2(m_prev - m_new)
    p = jnp.exp2(s - m_new)
    acc_sc[c] = alpha * acc_sc[c] + _dot(v_t1, p.astype(BF16))
    m_sc[c] = m_new


def _chain_result(c, acc_sc):
    acc = acc_sc[c]
    return acc[:LANES] * (1.0 / acc[LANES:LANES + 1])


def _chain_scratch(n_chains, t):
    return [pltpu.VMEM((n_chains, 1, t), F32), pltpu.VMEM((n_chains, LANES + ONES_ROWS, t), F32)]


def _tile_causal(t):
    return lax.broadcasted_iota(jnp.int32, (t, t), 0) <= lax.broadcasted_iota(jnp.int32, (t, t), 1)


def _head_rows(ref, h):
    return ref[h * LANES:(h + 1) * LANES, :]


def _heads(ref, h):
    return ref[:, h * LANES:(h + 1) * LANES]


def _da_kernel(qt_ref, kt_ref, q_ref, k_ref, vt_ref, lam_ref, sub_ref, o_ref, m_sc, acc_sc,
               *, lam_init, t, hb):
    step = pl.program_id(2)
    qi = qt_ref[step]
    ki = kt_ref[step]

    @pl.when(ki == 0)
    def _():
        _chain_init(m_sc, acc_sc)

    lane = lax.broadcasted_iota(jnp.int32, (1, LANES), 1)
    first_map = (lane % HALF_LANES) < (HALF_LANES // 2)

    def update(mask):
        for h in range(hb):
            q = _heads(q_ref, h)
            k = _heads(k_ref, h)
            v_t1 = _with_ones(_head_rows(vt_ref, h))
            zero = jnp.zeros_like(q)
            for j, qm in enumerate((jnp.where(first_map, q, zero), jnp.where(first_map, zero, q))):
                _chain_update(2 * h + j, _dot_nt(k, qm), mask, v_t1, m_sc, acc_sc)

    @pl.when(ki < qi)
    def _():
        update(None)

    @pl.when(ki == qi)
    def _():
        update(_tile_causal(t))
        lam = lam_ref[...]
        lam_full = (jnp.exp(jnp.sum(lam[0:1] * lam[1:2], axis=-1, keepdims=True))
                    - jnp.exp(jnp.sum(lam[2:3] * lam[3:4], axis=-1, keepdims=True)) + lam_init)
        for h in range(hb):
            o = _chain_result(2 * h, acc_sc) - lam_full * _chain_result(2 * h + 1, acc_sc)
            ms = jnp.mean(o * o, axis=0, keepdims=True)
            o = o * lax.rsqrt(ms + NORM_EPS) * sub_ref[...]
            o_ref[:, h * LANES:(h + 1) * LANES] = (o * (1.0 - lam_init)).T.astype(o_ref.dtype)


def _da_column_order(heads):
    dk = DA_QK_DIM
    hd = dk // 2
    per = 4 * dk + DA_V_DIM
    q_idx, k_idx, v_idx = [], [], []
    for h in range(heads):
        base = h * per
        for out, off in ((q_idx, 0), (k_idx, 2 * dk)):
            a = base + off + np.arange(dk)
            b = base + off + dk + np.arange(dk)
            out += [a[:hd], b[:hd], a[hd:], b[hd:]]
        v_idx.append(base + 4 * dk + np.arange(DA_V_DIM))
    return np.concatenate(q_idx + k_idx + v_idx)


def _diff_attention(hres, norm_w, w_in, lam, subln_w, w_out, lam_init, batch, seq):
    m, d = hres.shape
    heads = w_in.shape[1] // (4 * DA_QK_DIM + DA_V_DIM)
    hw = heads * LANES
    col_scale = np.ones(3 * hw, np.float32)
    col_scale[:hw] = DA_QK_DIM ** -0.5 * LOG2E
    w_perm = (w_in[:, _da_column_order(heads)] * col_scale[None, :]).astype(BF16)
    cos, sin = _rope_tables(np.arange(seq), DA_QK_DIM // 2)
    tn = 512
    proj = _linear(hres, w_perm[:, :2 * hw], n_out=2 * hw, norm_w=norm_w, tn=tn,
                   rope=(cos, sin, 0, 2 * hw // LANES, seq), name="da_qk_proj")
    v_t = _linear(hres, w_perm[:, 2 * hw:], n_out=hw, norm_w=norm_w, tn=tn, transpose_out=True,
                  name="da_vt_proj")

    t = min(ATTN_TILE, seq)
    nq = seq // t
    qt, kt = _tri_tables(nq)
    hb = min(ATTN_CHAINS // 2, heads)
    hg = heads // hb
    bw = hb * LANES
    kern = functools.partial(_da_kernel, lam_init=lam_init, t=t, hb=hb)
    attn = pl.pallas_call(
        kern,
        grid_spec=pltpu.PrefetchScalarGridSpec(
            num_scalar_prefetch=2,
            grid=(batch, hg, qt.shape[0]),
            in_specs=[
                pl.BlockSpec((t, bw), lambda b, h, s, qt, kt: (b * nq + qt[s], h)),
                pl.BlockSpec((t, bw), lambda b, h, s, qt, kt: (b * nq + kt[s], hg + h)),
                pl.BlockSpec((bw, t), lambda b, h, s, qt, kt: (h, b * nq + kt[s])),
                pl.BlockSpec(lam.shape, lambda b, h, s, qt, kt: (0, 0)),
                pl.BlockSpec((DA_V_DIM, 1), lambda b, h, s, qt, kt: (0, 0)),
            ],
            out_specs=pl.BlockSpec((t, bw), lambda b, h, s, qt, kt: (b * nq + qt[s], h)),
            scratch_shapes=_chain_scratch(2 * hb, t),
        ),
        out_shape=jax.ShapeDtypeStruct((m, hw), BF16),
        compiler_params=_cparams(("parallel", "parallel", "arbitrary")),
        name="da_attention",
    )(qt, kt, proj, proj, v_t, lam.astype(F32), subln_w.reshape(DA_V_DIM, 1).astype(F32))
    return _linear(attn, w_out.astype(BF16), n_out=d, residual=hres, out_dtype=F32, name="da_out_proj")


def _mla_kernel(qt_ref, kt_ref, qn_ref, qr_ref, kn_ref, kr_ref, vt_ref, o_ref, m_sc, acc_sc, *, t, hb):
    step = pl.program_id(2)
    qi = qt_ref[step]
    ki = kt_ref[step]

    @pl.when(ki == 0)
    def _():
        _chain_init(m_sc, acc_sc)

    kr = kr_ref[...].astype(BF16)

    def update(mask):
        for h in range(hb):
            q = jnp.concatenate([_heads(qn_ref, h), _heads(qr_ref, h)], axis=1)
            k = jnp.concatenate([_heads(kn_ref, h), kr], axis=1)
            _chain_update(h, _dot_nt(k, q), mask, _with_ones(_head_rows(vt_ref, h)), m_sc, acc_sc)

    @pl.when(ki < qi)
    def _():
        update(None)

    @pl.when(ki == qi)
    def _():
        update(_tile_causal(t))
        for h in range(hb):
            o_ref[:, h * LANES:(h + 1) * LANES] = _chain_result(h, acc_sc).T.astype(o_ref.dtype)


def _mla_q_columns(heads):
    per = MLA_NOPE_DIM + MLA_ROPE_DIM
    hd = MLA_ROPE_DIM // 2
    idx, keep = [], []
    for h in range(heads):
        idx.append(h * per + np.arange(MLA_NOPE_DIM))
        keep.append(np.ones(MLA_NOPE_DIM))
    for h in range(heads):
        s = h % 2
        g_idx = np.zeros(LANES, np.int64)
        g_keep = np.zeros(LANES)
        r0 = h * per + MLA_NOPE_DIM
        g_idx[s * hd:(s + 1) * hd] = r0 + np.arange(hd)
        g_idx[HALF_LANES + s * hd:HALF_LANES + (s + 1) * hd] = r0 + hd + np.arange(hd)
        g_keep[s * hd:(s + 1) * hd] = 1.0
        g_keep[HALF_LANES + s * hd:HALF_LANES + (s + 1) * hd] = 1.0
        idx.append(g_idx)
        keep.append(g_keep)
    return np.concatenate(idx), np.concatenate(keep)


def _mla_attention(hres, norm_w, w_in, q_norm_w, w_uq, kv_norm_w, w_ukv, w_out, batch, seq):
    m, d = hres.shape
    heads = MLA_HEADS
    hw = heads * LANES
    hd = MLA_ROPE_DIM // 2
    assert MLA_Q_RANK == MLA_KV_RANK and MLA_NOPE_DIM == LANES and MLA_V_DIM == LANES
    rank = MLA_Q_RANK
    r0 = 2 * rank
    kr_cols = np.concatenate([r0 + np.arange(hd), r0 + np.arange(hd),
                              r0 + hd + np.arange(hd), r0 + hd + np.arange(hd)])
    w_in_x = jnp.concatenate([w_in[:, :2 * rank], w_in[:, kr_cols]], axis=1).astype(BF16)
    cos, sin = _rope_tables(np.arange(seq), hd)
    n_in = 2 * rank + LANES
    tn_in = 3 * LANES if n_in % (3 * LANES) == 0 else LANES
    proj = _linear(hres, w_in_x, n_out=n_in, norm_w=norm_w, tn=tn_in, out_dtype=F32,
                   rope=(cos, sin, 2 * rank // LANES, n_in // LANES, seq), name="mla_in_proj")

    q_idx, q_keep = _mla_q_columns(heads)
    c2 = (MLA_NOPE_DIM + MLA_ROPE_DIM) ** -0.5 * LOG2E
    w_uq_x = (w_uq[:, q_idx] * jnp.asarray(q_keep * c2, F32)[None, :]).astype(BF16)
    tn = 512
    qcat = _linear(proj, w_uq_x, n_out=2 * hw, k=rank, x_col_block=0, norm_w=q_norm_w, tn=tn,
                   rope=(cos, sin, hw // LANES, 2 * hw // LANES, seq), name="mla_q_proj")
    k_nope = _linear(proj, w_ukv[:, :, :MLA_NOPE_DIM].reshape(rank, hw).astype(BF16), n_out=hw, k=rank,
                     x_col_block=1, norm_w=kv_norm_w, tn=tn, name="mla_k_proj")
    v_t = _linear(proj, w_ukv[:, :, MLA_NOPE_DIM:].reshape(rank, hw).astype(BF16), n_out=hw, k=rank,
                  x_col_block=1, norm_w=kv_norm_w, tn=tn, transpose_out=True, name="mla_vt_proj")

    t = min(ATTN_TILE, seq)
    nq = seq // t
    qt, kt = _tri_tables(nq)
    kr_block = 2 * rank // LANES
    hb = min(ATTN_CHAINS, heads)
    hg = heads // hb
    bw = hb * LANES
    kern = functools.partial(_mla_kernel, t=t, hb=hb)
    attn = pl.pallas_call(
        kern,
        grid_spec=pltpu.PrefetchScalarGridSpec(
            num_scalar_prefetch=2,
            grid=(batch, hg, qt.shape[0]),
            in_specs=[
                pl.BlockSpec((t, bw), lambda b, h, s, qt, kt: (b * nq + qt[s], h)),
                pl.BlockSpec((t, bw), lambda b, h, s, qt, kt: (b * nq + qt[s], hg + h)),
                pl.BlockSpec((t, bw), lambda b, h, s, qt, kt: (b * nq + kt[s], h)),
                pl.BlockSpec((t, LANES), lambda b, h, s, qt, kt: (b * nq + kt[s], kr_block)),
                pl.BlockSpec((bw, t), lambda b, h, s, qt, kt: (h, b * nq + kt[s])),
            ],
            out_specs=pl.BlockSpec((t, bw), lambda b, h, s, qt, kt: (b * nq + qt[s], h)),
            scratch_shapes=_chain_scratch(hb, t),
        ),
        out_shape=jax.ShapeDtypeStruct((m, hw), BF16),
        compiler_params=_cparams(("parallel", "parallel", "arbitrary")),
        name="mla_attention",
    )(qt, kt, qcat, qcat, k_nope, proj, v_t)
    return _linear(attn, w_out.astype(BF16), n_out=d, residual=hres, out_dtype=F32, name="mla_out_proj")


def _sb_kernel(qt_ref, kt_ref, q_ref, k_ref, v_ref, tri_ref, o_ref, carry_sc, acc_sc, *, t, hb):
    step = pl.program_id(2)
    qi = qt_ref[step]
    ki = kt_ref[step]

    @pl.when(ki == qi)
    def _():
        carry_sc[...] = jnp.zeros_like(carry_sc)
        acc_sc[...] = jnp.zeros_like(acc_sc)

    tri = tri_ref[...]
    chunk = tri.shape[0]

    def update(masked):
        if masked:
            mask = lax.broadcasted_iota(jnp.int32, (t, t), 1) < lax.broadcasted_iota(jnp.int32, (t, t), 0)
        for h in range(hb):
            z = _dot_nt(_heads(q_ref, h), _heads(k_ref, h))
            log_beta = jnp.minimum(z, 0.0) - jnp.log2(1.0 + jnp.exp2(-jnp.abs(z)))
            log_rest = log_beta - z
            if masked:
                log_rest = jnp.where(mask, log_rest, 0.0)
            carry = carry_sc[h]
            parts = []
            for c in range(t // chunk - 1, -1, -1):
                cols = slice(c * chunk, (c + 1) * chunk)
                hi, lo = _split_bf16(log_rest[:, cols])
                tail = _dot(hi, tri) + _dot(lo, tri) + carry
                parts.append(jnp.exp2(log_beta[:, cols] + tail))
                carry = carry + jnp.sum(log_rest[:, cols], axis=-1, keepdims=True)
            a = jnp.concatenate(parts[::-1], axis=1)
            if masked:
                a = jnp.where(mask, a, 0.0)
            acc_sc[h] += _dot(a.astype(BF16), _heads(v_ref, h))
            carry_sc[h] = carry

    @pl.when(ki == qi)
    def _():
        update(True)

    @pl.when(ki < qi)
    def _():
        update(False)

    @pl.when(ki == 0)
    def _():
        for h in range(hb):
            o_ref[:, h * LANES:(h + 1) * LANES] = acc_sc[h].astype(o_ref.dtype)


def _sb_attention(hres, norm_w, w_in, w_out, batch, seq):
    m, d = hres.shape
    heads = SB_HEADS
    hw = heads * LANES
    assert w_in.shape[1] == 3 * hw
    col_scale = np.ones(3 * hw, np.float32)
    col_scale[:hw] = LANES ** -0.5 * LOG2E
    qkv = _linear(hres, (w_in * col_scale[None, :]).astype(BF16), n_out=3 * hw, norm_w=norm_w,
                  name="sb_qkv_proj")
    t = min(ATTN_TILE, seq)
    nq = seq // t
    qt, kt = _tri_tables(nq, reverse=True)
    chunk = min(2 * LANES, t)
    tri = jnp.asarray(np.arange(chunk)[:, None] > np.arange(chunk)[None, :], BF16)
    hb = min(SB_HEADS_PER_STEP, heads)
    hg = heads // hb
    bw = hb * LANES
    kern = functools.partial(_sb_kernel, t=t, hb=hb)
    attn = pl.pallas_call(
        kern,
        grid_spec=pltpu.PrefetchScalarGridSpec(
            num_scalar_prefetch=2,
            grid=(batch, hg, qt.shape[0]),
            in_specs=[
                pl.BlockSpec((t, bw), lambda b, h, s, qt, kt: (b * nq + qt[s], h)),
                pl.BlockSpec((t, bw), lambda b, h, s, qt, kt: (b * nq + kt[s], hg + h)),
                pl.BlockSpec((t, bw), lambda b, h, s, qt, kt: (b * nq + kt[s], 2 * hg + h)),
                pl.BlockSpec((chunk, chunk), lambda b, h, s, qt, kt: (0, 0)),
            ],
            out_specs=pl.BlockSpec((t, bw), lambda b, h, s, qt, kt: (b * nq + qt[s], h)),
            scratch_shapes=[pltpu.VMEM((hb, t, 1), F32), pltpu.VMEM((hb, t, LANES), F32)],
        ),
        out_shape=jax.ShapeDtypeStruct((m, hw), BF16),
        compiler_params=_cparams(("parallel", "parallel", "arbitrary")),
        name="sb_attention",
    )(qt, kt, qkv, qkv, qkv, tri)
    return _linear(attn, w_out.astype(BF16), n_out=d, residual=hres, out_dtype=F32, name="sb_out_proj")


def _nsa_compress_kernel(x_ref, w_ref, pe_ref, cos_ref, sin_ref, o_ref, *, rope, transpose_out):
    x = x_ref[0]
    half = w_ref.shape[0] // 2
    y_lo = _dot(x, w_ref[:half, :])
    y_hi = _dot(x, w_ref[half:, :])
    n = y_hi.shape[0]
    y = y_lo + pltpu.roll(y_hi, n - 1, 0) + _dot(pe_ref[...], w_ref[...])[0:1]
    if rope:
        y = y * cos_ref[...] + pltpu.roll(y, HALF_LANES, 1) * sin_ref[...]
    o_ref[0] = (y.T if transpose_out else y).astype(o_ref.dtype)


def _nsa_compress(x_chunks, w, pe, cos, sin, rope, transpose_out):
    bg, n_chunk, kdim = x_chunks.shape
    dh = w.shape[1]
    out_block = (1, dh, n_chunk) if transpose_out else (1, n_chunk, dh)
    return pl.pallas_call(
        functools.partial(_nsa_compress_kernel, rope=rope, transpose_out=transpose_out),
        grid=(bg,),
        in_specs=[pl.BlockSpec((1, n_chunk, kdim), lambda i: (i, 0, 0)),
                  pl.BlockSpec(w.shape, lambda i: (0, 0)),
                  pl.BlockSpec(pe.shape, lambda i: (0, 0)),
                  pl.BlockSpec(cos.shape, lambda i: (0, 0)),
                  pl.BlockSpec(sin.shape, lambda i: (0, 0))],
        out_specs=pl.BlockSpec(out_block, lambda i: (i, 0, 0)),
        out_shape=jax.ShapeDtypeStruct((bg,) + out_block[1:], BF16),
        compiler_params=_cparams(("parallel",)),
        name="nsa_compress",
    )(x_chunks, w, pe, cos, sin)


def _head_gate(gates, branch, head):
    lane = lax.broadcasted_iota(jnp.int32, gates.shape, 1)
    return jnp.sum(jnp.where(lane == branch * NSA_HEADS + head, gates, 0.0), axis=-1, keepdims=True)


def _nsa_cmp_kernel(q_ref, kc_ref, vct_ref, ovt_ref, g_ref, o_ref, sel_ref, *, tq, rep, n_sel):
    qi = pl.program_id(2)
    grp = pl.program_id(1)
    qs = jnp.concatenate([q_ref[:, r * LANES:(r + 1) * LANES] for r in range(rep)], axis=0)
    kc = kc_ref[0]
    n_cmp = kc.shape[0]
    rows = rep * tq

    st = _dot_nt(kc, qs)
    qpos_t = qi * tq + lax.broadcasted_iota(jnp.int32, (n_cmp, rep, tq), 2).reshape(n_cmp, rows)
    cend_t = lax.broadcasted_iota(jnp.int32, (n_cmp, rows), 0) * NSA_CMP_STRIDE + (NSA_CMP_BLOCK - 1)
    mask_t = cend_t <= qpos_t
    st = jnp.where(mask_t, st, NEG)
    pt = jnp.where(mask_t, jnp.exp2(st - jnp.max(st, axis=0, keepdims=True)), 0.0)
    den_t = jnp.sum(pt, axis=0, keepdims=True)
    pt = pt / jnp.where(den_t > 0.0, den_t, 1.0)
    o_t = _dot(vct_ref[0], pt.astype(BF16))
    gates = g_ref[...]
    for r in range(rep):
        gate = _head_gate(gates, 0, grp * rep + r)
        o_ref[:, r * LANES:(r + 1) * LANES] = gate * o_t[:, r * tq:(r + 1) * tq].T

    psum = pt[:, 0:tq]
    for r in range(1, rep):
        psum = psum + pt[:, r * tq:(r + 1) * tq]
    hi, lo = _split_bf16(psum)
    ovt = ovt_ref[...]
    imp = _dot(ovt, hi) + _dot(ovt, lo)

    n_pad = imp.shape[0]
    blk = lax.broadcasted_iota(jnp.int32, (n_pad, tq), 0)
    cur = (qi * tq + lax.broadcasted_iota(jnp.int32, (n_pad, tq), 1)) // NSA_SLC_BLOCK
    forced = (blk == 0) | (blk == cur) | (blk == cur - 1)
    val = jnp.where(forced, jnp.inf, jnp.where(blk <= cur, imp, -jnp.inf))
    sel = jnp.zeros((n_pad, tq), F32)
    for _ in range(n_sel):
        top = jnp.max(val, axis=0, keepdims=True)
        first = jnp.min(jnp.where(val == top, blk, n_pad), axis=0, keepdims=True)
        pick = blk == first
        sel = jnp.where(pick, 1.0, sel)
        val = jnp.where(pick, -jnp.inf, val)
    sel_ref[...] = sel.astype(sel_ref.dtype)


def _nsa_win_kernel(q_ref, k_ref, vt_ref, g_ref, prev_ref, o_ref, m_sc, acc_sc, *, t, rep, nwin):
    qi = pl.program_id(2)
    kk = pl.program_id(3)
    grp = pl.program_id(1)
    kb = qi - (nwin - 1) + kk

    @pl.when(kk == 0)
    def _():
        _chain_init(m_sc, acc_sc)

    def update(mask):
        k = k_ref[...]
        v_t1 = _with_ones(vt_ref[...])
        for r in range(rep):
            _chain_update(r, _dot_nt(k, _heads(q_ref, r)), mask, v_t1, m_sc, acc_sc)

    @pl.when((kk == 0) & (kb >= 0))
    def _():
        update(jnp.logical_not(_tile_causal(t)))

    @pl.when((kk > 0) & (kk < nwin - 1) & (kb >= 0))
    def _():
        update(None)

    @pl.when(kk == nwin - 1)
    def _():
        update(_tile_causal(t))
        gates = g_ref[...]
        for r in range(rep):
            gate = _head_gate(gates, 2, grp * rep + r)
            o_ref[:, r * LANES:(r + 1) * LANES] = _heads(prev_ref, r) + gate * _chain_result(r, acc_sc).T


def _nsa_slc_kernel(qt_ref, kt_ref, q_ref, k_ref, vt_ref, sel_ref, g_ref, prev_ref, o_ref,
                    m_sc, acc_sc, *, t, rep, gb):
    step = pl.program_id(2)
    grp0 = pl.program_id(1) * gb
    qi = qt_ref[step]
    ki = kt_ref[step]

    @pl.when(ki == 0)
    def _():
        _chain_init(m_sc, acc_sc)

    n_pad = sel_ref.shape[0] // gb
    blk = lax.broadcasted_iota(jnp.int32, (t, n_pad), 1)
    kblk = (ki * t + lax.broadcasted_iota(jnp.int32, (t, n_pad), 0)) // NSA_SLC_BLOCK
    expand = jnp.where(blk == kblk, 1.0, 0.0).astype(BF16)

    def update(diagonal):
        for g in range(gb):
            mask = _dot(expand, sel_ref[g * n_pad:(g + 1) * n_pad, :]) > 0.5
            if diagonal:
                mask = mask & _tile_causal(t)
            k = _heads(k_ref, g)
            v_t1 = _with_ones(_head_rows(vt_ref, g))
            for r in range(rep):
                c = g * rep + r
                _chain_update(c, _dot_nt(k, _heads(q_ref, c)), mask, v_t1, m_sc, acc_sc)

    @pl.when(ki < qi)
    def _():
        update(False)

    @pl.when(ki == qi)
    def _():
        update(True)
        gates = g_ref[...]
        for c in range(gb * rep):
            gate = _head_gate(gates, 1, grp0 * rep + c)
            o = _chain_result(c, acc_sc).T
            o_ref[:, c * LANES:(c + 1) * LANES] = (_heads(prev_ref, c) + gate * o).astype(o_ref.dtype)


def _nsa_attention(hres, norm_w, w_in, cmp_pe, cmp_w, w_out, batch, seq):
    m, d = hres.shape
    heads, groups = NSA_HEADS, NSA_GROUPS
    rep = heads // groups
    dh = d // heads
    assert dh == LANES
    kvw = groups * dh
    n_main = heads * dh + 6 * kvw
    gw = rep * LANES

    hq = heads * dh
    cos, sin = _rope_tables(np.arange(seq), dh // 2)
    kv_cols = lambda i: np.arange(hq + i * kvw, hq + (i + 1) * kvw)
    main_cols = np.concatenate([np.arange(hq), kv_cols(0), kv_cols(1), kv_cols(2), kv_cols(4)])
    col_scale = np.ones(main_cols.size, np.float32)
    col_scale[:hq] = dh ** -0.5 * LOG2E
    proj = _linear(hres, (w_in[:, main_cols] * col_scale[None, :]).astype(BF16), n_out=main_cols.size,
                   norm_w=norm_w, rope=(cos, sin, 0, hq // LANES, seq), name="nsa_qk_proj")
    v_t = _linear(hres, w_in[:, np.concatenate([kv_cols(3), kv_cols(5)])].astype(BF16), n_out=2 * kvw,
                  norm_w=norm_w, transpose_out=True, name="nsa_vt_proj")
    w_gate = jnp.pad(w_in[:, n_main:], ((0, 0), (0, LANES - 3 * heads))).astype(BF16)
    gates = _linear(hres, w_gate, n_out=LANES, norm_w=norm_w, act="sigmoid", out_dtype=F32,
                    name="nsa_gate_proj")
    k_slc_blk = (hq + 2 * kvw) // LANES
    k_win_blk = (hq + 3 * kvw) // LANES

    n_chunk = seq // NSA_CMP_STRIDE
    assert NSA_CMP_BLOCK == 2 * NSA_CMP_STRIDE
    cmp_end = np.arange(n_chunk) * NSA_CMP_STRIDE + NSA_CMP_BLOCK - 1
    ccos, csin = _rope_tables(cmp_end, dh // 2)

    def chunks(i):
        lo = hq + i * kvw
        x = proj[:, lo:lo + kvw].reshape(batch, n_chunk, NSA_CMP_STRIDE, groups, dh)
        return x.transpose(0, 3, 1, 2, 4).reshape(batch * groups, n_chunk, NSA_CMP_STRIDE * dh)

    def cmp_weights(i):
        return (cmp_w[i].reshape(NSA_CMP_BLOCK * dh, dh).astype(BF16),
                jnp.broadcast_to(cmp_pe[i].reshape(1, NSA_CMP_BLOCK * dh), (8, NSA_CMP_BLOCK * dh)).astype(BF16))

    k_cmp = _nsa_compress(chunks(0), *cmp_weights(0), ccos, csin, True, False)
    v_cmp_t = _nsa_compress(chunks(1), *cmp_weights(1), ccos, csin, False, True)

    n_slc = seq // NSA_SLC_BLOCK
    n_sel = min(NSA_SLC_TOPK, n_slc)
    n_pad = max(LANES, n_slc)
    n_cmp = n_chunk - NSA_CMP_BLOCK // NSA_CMP_STRIDE + 1
    cs = np.arange(n_chunk)[None, :] * NSA_CMP_STRIDE
    js = np.arange(n_pad)[:, None] * NSA_SLC_BLOCK
    ovt = ((cs < js + NSA_SLC_BLOCK) & (cs + NSA_CMP_BLOCK > js)
           & (np.arange(n_chunk)[None, :] < n_cmp) & (np.arange(n_pad)[:, None] < n_slc))
    ovt = jnp.asarray(ovt, BF16)

    tq = min(256, seq)
    nqc = seq // tq
    o_c, sel = pl.pallas_call(
        functools.partial(_nsa_cmp_kernel, tq=tq, rep=rep, n_sel=n_sel),
        grid=(batch, groups, nqc),
        in_specs=[
            pl.BlockSpec((tq, gw), lambda b, g, i: (b * nqc + i, g)),
            pl.BlockSpec((1, n_chunk, dh), lambda b, g, i: (b * groups + g, 0, 0)),
            pl.BlockSpec((1, dh, n_chunk), lambda b, g, i: (b * groups + g, 0, 0)),
            pl.BlockSpec(ovt.shape, lambda b, g, i: (0, 0)),
            pl.BlockSpec((tq, LANES), lambda b, g, i: (b * nqc + i, 0)),
        ],
        out_specs=[
            pl.BlockSpec((tq, gw), lambda b, g, i: (b * nqc + i, g)),
            pl.BlockSpec((n_pad, tq), lambda b, g, i: (b * groups + g, i)),
        ],
        out_shape=[jax.ShapeDtypeStruct((m, heads * dh), F32),
                   jax.ShapeDtypeStruct((batch * groups * n_pad, seq), BF16)],
        compiler_params=_cparams(("parallel", "parallel", "arbitrary")),
        name="nsa_compressed_select",
    )(proj, k_cmp, v_cmp_t, ovt, gates)

    t = min(512, seq)
    nq = seq // t
    assert NSA_WINDOW % t == 0
    nwin = NSA_WINDOW // t + 1
    key_tile = lambda i, kk: jnp.maximum(i - (nwin - 1) + kk, 0)
    o_cw = pl.pallas_call(
        functools.partial(_nsa_win_kernel, t=t, rep=rep, nwin=nwin),
        grid=(batch, groups, nq, nwin),
        in_specs=[
            pl.BlockSpec((t, gw), lambda b, g, i, kk: (b * nq + i, g)),
            pl.BlockSpec((t, LANES), lambda b, g, i, kk: (b * nq + key_tile(i, kk), k_win_blk + g)),
            pl.BlockSpec((LANES, t), lambda b, g, i, kk: (groups + g, b * nq + key_tile(i, kk))),
            pl.BlockSpec((t, LANES), lambda b, g, i, kk: (b * nq + i, 0)),
            pl.BlockSpec((t, gw), lambda b, g, i, kk: (b * nq + i, g)),
        ],
        out_specs=pl.BlockSpec((t, gw), lambda b, g, i, kk: (b * nq + i, g)),
        out_shape=jax.ShapeDtypeStruct((m, heads * dh), F32),
        scratch_shapes=_chain_scratch(rep, t),
        compiler_params=_cparams(("parallel", "parallel", "parallel", "arbitrary")),
        name="nsa_window",
    )(proj, proj, v_t, gates, o_c)

    ts = min(ATTN_TILE, seq)
    nqs = seq // ts
    qt, kt = _tri_tables(nqs)
    gb = min(groups, max(1, ATTN_CHAINS // rep))
    assert groups % gb == 0 and k_slc_blk % gb == 0
    gsteps = groups // gb
    attn = pl.pallas_call(
        functools.partial(_nsa_slc_kernel, t=ts, rep=rep, gb=gb),
        grid_spec=pltpu.PrefetchScalarGridSpec(
            num_scalar_prefetch=2,
            grid=(batch, gsteps, qt.shape[0]),
            in_specs=[
                pl.BlockSpec((ts, gb * gw), lambda b, g, s, qt, kt: (b * nqs + qt[s], g)),
                pl.BlockSpec((ts, gb * LANES), lambda b, g, s, qt, kt: (b * nqs + kt[s], k_slc_blk // gb + g)),
                pl.BlockSpec((gb * LANES, ts), lambda b, g, s, qt, kt: (g, b * nqs + kt[s])),
                pl.BlockSpec((gb * n_pad, ts), lambda b, g, s, qt, kt: (b * gsteps + g, qt[s])),
                pl.BlockSpec((ts, LANES), lambda b, g, s, qt, kt: (b * nqs + qt[s], 0)),
                pl.BlockSpec((ts, gb * gw), lambda b, g, s, qt, kt: (b * nqs + qt[s], g)),
            ],
            out_specs=pl.BlockSpec((ts, gb * gw), lambda b, g, s, qt, kt: (b * nqs + qt[s], g)),
            scratch_shapes=_chain_scratch(gb * rep, ts),
        ),
        out_shape=jax.ShapeDtypeStruct((m, heads * dh), BF16),
        compiler_params=_cparams(("parallel", "parallel", "arbitrary")),
        name="nsa_selected",
    )(qt, kt, proj, proj, v_t, sel, gates, o_cw)
    return _linear(attn, w_out.astype(BF16), n_out=d, residual=hres, out_dtype=F32, name="nsa_out_proj")


def _swiglu(hres, norm_w, w_gu, w_down):
    d = hres.shape[1]
    ff = w_down.shape[0]
    tn = 512
    hidden = _linear(hres, w_gu.astype(BF16), n_out=ff, w_tile_offsets=(0, ff // tn), norm_w=norm_w,
                     act="swiglu", tn=tn, name="ffn_gate_up")
    return _linear(hidden, w_down.astype(BF16), n_out=d, residual=hres, out_dtype=F32, name="ffn_down")


def _router_kernel(x_ref, nw_ref, w_ref, b_ref, f_ref, idx_ref, gate_ref, *, n_experts):
    xf = x_ref[...]
    ms = jnp.mean(xf * xf, axis=-1, keepdims=True)
    f = xf * lax.rsqrt(ms + NORM_EPS) * nw_ref[...]
    f_ref[...] = f
    fh, fl = _split_bf16(f)
    w = w_ref[...]
    wh, wl = _split_bf16(w)
    logits = _dot(fh, wh) + _dot(fl, wh) + _dot(fh, wl) + b_ref[...]
    lane = lax.broadcasted_iota(jnp.int32, logits.shape, 1)
    logits = jnp.where(lane < n_experts, logits, -jnp.inf)
    v1 = jnp.max(logits, axis=-1, keepdims=True)
    i1 = jnp.min(jnp.where(logits == v1, lane, LANES), axis=-1, keepdims=True)
    rest = jnp.where(lane == i1, -jnp.inf, logits)
    v2 = jnp.max(rest, axis=-1, keepdims=True)
    i2 = jnp.min(jnp.where(rest == v2, lane, LANES), axis=-1, keepdims=True)
    e2 = jnp.exp(v2 - v1)
    g1 = 1.0 / (1.0 + e2)
    g2 = e2 / (1.0 + e2)
    idx_ref[...] = jnp.where(lane == 0, i1, jnp.where(lane == 1, i2, 0))
    gate_ref[...] = jnp.where(lane == 0, g1, jnp.where(lane == 1, g2, 0.0))


def _gather_kernel(src_ref, nvalid_ref, x_hbm, o_ref, buf, sem):
    blk = pl.program_id(0)
    rows = buf.shape[0]
    nvalid = nvalid_ref[blk]

    def row_copy(r):
        return pltpu.make_async_copy(x_hbm.at[pl.ds(src_ref[blk * rows + r], 1), :],
                                     buf.at[pl.ds(r, 1), :], sem)

    def start(r, c):
        row_copy(r).start()
        return c

    def wait(r, c):
        row_copy(r).wait()
        return c

    @pl.when(nvalid > 0)
    def _():
        lax.fori_loop(0, rows, start, 0, unroll=DMA_UNROLL)
        lax.fori_loop(0, rows, wait, 0, unroll=DMA_UNROLL)
        rid = lax.broadcasted_iota(jnp.int32, (rows, 1), 0)
        o_ref[...] = jnp.where(rid < nvalid, buf[...], 0.0).astype(o_ref.dtype)

    @pl.when(nvalid == 0)
    def _():
        o_ref[...] = jnp.zeros_like(o_ref)


def _weight_tile_is_new(be_ref, blk):
    return (blk == 0) | (be_ref[blk] != be_ref[jnp.maximum(blk - 1, 0)])


def _expert_up_kernel(be_ref, nv_ref, x_ref, wg_ref, wu_ref, o_ref, wg_sc, wu_sc):
    blk = pl.program_id(1)

    @pl.when(_weight_tile_is_new(be_ref, blk))
    def _():
        wg_sc[...] = wg_ref[0, 0].astype(BF16)
        wu_sc[...] = wu_ref[0, 0].astype(BF16)

    @pl.when(nv_ref[blk] > 0)
    def _():
        x = x_ref[...]
        g = _dot(x, wg_sc[...])
        u = _dot(x, wu_sc[...])
        o_ref[...] = (g * jax.nn.sigmoid(g) * u).astype(o_ref.dtype)

    @pl.when(nv_ref[blk] == 0)
    def _():
        o_ref[...] = jnp.zeros_like(o_ref)


def _expert_down_kernel(be_ref, nv_ref, h_ref, w_ref, o_ref, w_sc):
    blk = pl.program_id(1)

    @pl.when(_weight_tile_is_new(be_ref, blk))
    def _():
        w_sc[...] = w_ref[0, 0].astype(BF16)

    @pl.when(nv_ref[blk] > 0)
    def _():
        o_ref[...] = _dot(h_ref[...], w_sc[...])

    @pl.when(nv_ref[blk] == 0)
    def _():
        o_ref[...] = jnp.zeros_like(o_ref)


def _combine_kernel(dst_ref, y_hbm, res_ref, gate_ref, o_ref, buf0, buf1, sem):
    i = pl.program_id(0)
    rows = buf0.shape[0]

    def row_copy(r, slot, buf):
        return pltpu.make_async_copy(y_hbm.at[pl.ds(dst_ref[(i * rows + r) * TOP_K + slot], 1), :],
                                     buf.at[pl.ds(r, 1), :], sem.at[slot])

    def start(r, c):
        row_copy(r, 0, buf0).start()
        row_copy(r, 1, buf1).start()
        return c

    def wait(r, c):
        row_copy(r, 0, buf0).wait()
        row_copy(r, 1, buf1).wait()
        return c

    lax.fori_loop(0, rows, start, 0, unroll=DMA_UNROLL)
    lax.fori_loop(0, rows, wait, 0, unroll=DMA_UNROLL)
    gate = gate_ref[...]
    o_ref[...] = res_ref[...] + (buf0[...] * gate[:, 0:1] + buf1[...] * gate[:, 1:2])


def _moe(hres, norm_w, w_router, b_router, w_gu, w_down, layer):
    m, d = hres.shape
    n_exp = w_router.shape[1]
    ff = w_down.shape[2]
    rb = MOE_ROW_BLOCK
    assert TOP_K == 2 and m % rb == 0
    tm = min(512, m)

    w_r = jnp.pad(w_router, ((0, 0), (0, LANES - n_exp))).astype(F32)
    b_r = jnp.pad(b_router, (0, LANES - n_exp)).reshape(1, LANES).astype(F32)
    f, idx, gate = pl.pallas_call(
        functools.partial(_router_kernel, n_experts=n_exp),
        grid=(m // tm,),
        in_specs=[pl.BlockSpec((tm, d), lambda i: (i, 0)),
                  pl.BlockSpec((1, d), lambda i: (0, 0)),
                  pl.BlockSpec((d, LANES), lambda i: (0, 0)),
                  pl.BlockSpec((1, LANES), lambda i: (0, 0))],
        out_specs=[pl.BlockSpec((tm, d), lambda i: (i, 0)),
                   pl.BlockSpec((tm, LANES), lambda i: (i, 0)),
                   pl.BlockSpec((tm, LANES), lambda i: (i, 0))],
        out_shape=[jax.ShapeDtypeStruct((m, d), F32),
                   jax.ShapeDtypeStruct((m, LANES), jnp.int32),
                   jax.ShapeDtypeStruct((m, LANES), F32)],
        compiler_params=_cparams(("parallel",)),
        name="moe_router",
    )(hres, norm_w.reshape(1, d).astype(F32), w_r, b_r)

    nk = m * TOP_K
    e_flat = idx[:, :TOP_K].reshape(nk)
    onehot = (e_flat[:, None] == jnp.arange(n_exp, dtype=jnp.int32)[None, :]).astype(jnp.int32)
    csum = jnp.cumsum(onehot, axis=0)
    rank = jnp.sum((csum - onehot) * onehot, axis=1)
    counts = csum[-1]
    padded = (counts + rb - 1) // rb * rb
    pend = jnp.cumsum(padded)
    pstart = pend - padded
    dest = (pstart[e_flat] + rank).astype(jnp.int32)
    n_rows = (-(-nk // rb)) * rb + n_exp * rb
    n_blocks = n_rows // rb
    src_tok = jnp.zeros((n_rows,), jnp.int32).at[dest].set(jnp.arange(nk, dtype=jnp.int32) // TOP_K)
    blk_start = jnp.arange(n_blocks, dtype=jnp.int32) * rb
    blk_expert = jnp.minimum(jnp.searchsorted(pend, blk_start, side="right"), n_exp - 1).astype(jnp.int32)
    nvalid = jnp.clip(pstart[blk_expert] + counts[blk_expert] - blk_start, 0, rb).astype(jnp.int32)

    xbuf = pl.pallas_call(
        _gather_kernel,
        grid_spec=pltpu.PrefetchScalarGridSpec(
            num_scalar_prefetch=2,
            grid=(n_blocks,),
            in_specs=[pl.BlockSpec(memory_space=pl.ANY)],
            out_specs=pl.BlockSpec((rb, d), lambda i, src, nv: (i, 0)),
            scratch_shapes=[pltpu.VMEM((rb, d), F32), pltpu.SemaphoreType.DMA(())],
        ),
        out_shape=jax.ShapeDtypeStruct((n_rows, d), BF16),
        compiler_params=_cparams(("arbitrary",)),
        name="moe_gather",
    )(src_tok, nvalid, f)

    tn = 512
    hidden = pl.pallas_call(
        _expert_up_kernel,
        grid_spec=pltpu.PrefetchScalarGridSpec(
            num_scalar_prefetch=2,
            grid=(ff // tn, n_blocks),
            in_specs=[pl.BlockSpec((rb, d), lambda j, i, be, nv: (i, 0)),
                      pl.BlockSpec((1, 1, d, tn), lambda j, i, be, nv: (layer, be[i], 0, j)),
                      pl.BlockSpec((1, 1, d, tn), lambda j, i, be, nv: (layer, be[i], 0, j + ff // tn))],
            out_specs=pl.BlockSpec((rb, tn), lambda j, i, be, nv: (i, j)),
            scratch_shapes=[pltpu.VMEM((d, tn), BF16), pltpu.VMEM((d, tn), BF16)],
        ),
        out_shape=jax.ShapeDtypeStruct((n_rows, ff), BF16),
        compiler_params=_cparams(("arbitrary", "arbitrary")),
        name="moe_expert_up",
    )(blk_expert, nvalid, xbuf, w_gu, w_gu)

    tnd = min(512, d)
    ybuf = pl.pallas_call(
        _expert_down_kernel,
        grid_spec=pltpu.PrefetchScalarGridSpec(
            num_scalar_prefetch=2,
            grid=(d // tnd, n_blocks),
            in_specs=[pl.BlockSpec((rb, ff), lambda j, i, be, nv: (i, 0)),
                      pl.BlockSpec((1, 1, ff, tnd), lambda j, i, be, nv: (layer, be[i], 0, j))],
            out_specs=pl.BlockSpec((rb, tnd), lambda j, i, be, nv: (i, j)),
            scratch_shapes=[pltpu.VMEM((ff, tnd), BF16)],
        ),
        out_shape=jax.ShapeDtypeStruct((n_rows, d), F32),
        compiler_params=_cparams(("arbitrary", "arbitrary")),
        name="moe_expert_down",
    )(blk_expert, nvalid, hidden, w_down)

    return pl.pallas_call(
        _combine_kernel,
        grid_spec=pltpu.PrefetchScalarGridSpec(
            num_scalar_prefetch=1,
            grid=(m // tm,),
            in_specs=[pl.BlockSpec(memory_space=pl.ANY),
                      pl.BlockSpec((tm, d), lambda i, dst: (i, 0)),
                      pl.BlockSpec((tm, LANES), lambda i, dst: (i, 0))],
            out_specs=pl.BlockSpec((tm, d), lambda i, dst: (i, 0)),
            scratch_shapes=[pltpu.VMEM((tm, d), F32), pltpu.VMEM((tm, d), F32),
                            pltpu.SemaphoreType.DMA((TOP_K,))],
        ),
        out_shape=jax.ShapeDtypeStruct((m, d), F32),
        compiler_params=_cparams(("arbitrary",)),
        name="moe_combine",
    )(dest, ybuf, hres, gate)


def kernel(x, attn_norm_w, ffn_norm_w, final_norm_w, da_w_in, da_lambda, da_subln_w, da_w_out, nsa_w_in, nsa_cmp_pe, nsa_cmp_w, nsa_w_out, mla_w_in, mla_q_norm_w, mla_w_uq, mla_kv_norm_w, mla_w_ukv, mla_w_out, sb_w_in, sb_w_out, ffn_w_gu, ffn_w_down, moe_w_router, moe_b_router, moe_w_gu, moe_w_down):
    batch, seq, d = x.shape
    depth = attn_norm_w.shape[0]
    h = x.reshape(batch * seq, d)
    for i in range(depth):
        kind = i % 4
        j = i // 4
        if kind == 0:
            lam_init = 0.8 - 0.6 * math.exp(-0.3 * i)
            h = _diff_attention(h, attn_norm_w[i], da_w_in[j], da_lambda[j], da_subln_w[j], da_w_out[j],
                                lam_init, batch, seq)
        elif kind == 1:
            h = _nsa_attention(h, attn_norm_w[i], nsa_w_in[j], nsa_cmp_pe[j], nsa_cmp_w[j], nsa_w_out[j],
                               batch, seq)
        elif kind == 2:
            h = _mla_attention(h, attn_norm_w[i], mla_w_in[j], mla_q_norm_w[j], mla_w_uq[j],
                               mla_kv_norm_w[j], mla_w_ukv[j], mla_w_out[j], batch, seq)
        else:
            h = _sb_attention(h, attn_norm_w[i], sb_w_in[j], sb_w_out[j], batch, seq)
        if i % 2 == 0:
            h = _swiglu(h, ffn_norm_w[i], ffn_w_gu[i // 2], ffn_w_down[i // 2])
        else:
            h = _moe(h, ffn_norm_w[i], moe_w_router[i // 2], moe_b_router[i // 2],
                     moe_w_gu, moe_w_down, i // 2)
    return _rmsnorm(h, final_norm_w.astype(F32)).reshape(batch, seq, d)
```

```python
import functools
import math

import numpy as np
import jax
import jax.numpy as jnp
from jax import lax
from jax.experimental import pallas as pl
from jax.experimental.pallas import tpu as pltpu

ROPE_THETA = 10000.0
NORM_EPS = 1e-6

DA_QK_DIM = 64
DA_V_DIM = 2 * DA_QK_DIM

NSA_HEADS = 16
NSA_GROUPS = 4
NSA_CMP_BLOCK = 32
NSA_CMP_STRIDE = 16
NSA_SLC_BLOCK = 64
NSA_SLC_TOPK = 16
NSA_WINDOW = 512

MLA_HEADS = 16
MLA_Q_RANK = 512
MLA_KV_RANK = 512
MLA_NOPE_DIM = 128
MLA_ROPE_DIM = 64
MLA_V_DIM = 128

SB_HEADS = 16

N_EXPERTS = 8
TOP_K = 2
MOE_ROW_BLOCK = 512

LANES = 128
HALF_LANES = LANES // 2
V7X_VMEM_BYTES = 64 * 1024 * 1024
VMEM_LIMIT = V7X_VMEM_BYTES * 7 // 8
NEG = -1e30
LOG2E = math.log2(math.e)
ONES_ROWS = 16
DMA_UNROLL = 8
ATTN_TILE = 1024
ATTN_CHAINS = 8
SB_HEADS_PER_STEP = 4

F32 = jnp.float32
BF16 = jnp.bfloat16


def _cparams(sem):
    return pltpu.CompilerParams(dimension_semantics=sem, vmem_limit_bytes=VMEM_LIMIT)


def _dot(a, b):
    return jnp.dot(a, b, preferred_element_type=F32)


def _dot_nt(a, b):
    return lax.dot_general(a, b, (((1,), (1,)), ((), ())), preferred_element_type=F32)


def _split_bf16(x):
    hi = x.astype(BF16)
    lo = (x - hi.astype(F32)).astype(BF16)
    return hi, lo


def _linear_kernel(*refs, has_norm, prologue, n_w, has_res, rope, act, transpose_out):
    it = iter(refs)
    x_ref = next(it)
    nw_ref = next(it) if has_norm else None
    w_refs = [next(it) for _ in range(n_w)]
    cos_ref = next(it) if rope else None
    sin_ref = next(it) if rope else None
    res_ref = next(it) if has_res else None
    o_ref = next(it)
    xs_ref = next(it) if prologue else None
    j = pl.program_id(1)

    if prologue:
        @pl.when(j == 0)
        def _():
            xf = x_ref[...].astype(F32)
            if has_norm:
                ms = jnp.mean(xf * xf, axis=-1, keepdims=True)
                xf = xf * lax.rsqrt(ms + NORM_EPS) * nw_ref[...]
            xs_ref[...] = xf.astype(BF16)
        xb = xs_ref[...]
    else:
        xb = x_ref[...]

    y = _dot(xb, w_refs[0][...])
    if act == "swiglu":
        u = _dot(xb, w_refs[1][...])
        y = y * jax.nn.sigmoid(y) * u
    elif act == "sigmoid":
        y = jax.nn.sigmoid(y)
    if has_res:
        y = y + res_ref[...]

    if rope:
        lo, hi = rope
        groups = y.shape[1] // LANES
        g0 = j * groups
        tile_has_rope = (g0 < hi) & (g0 + groups > lo)

        @pl.when(tile_has_rope)
        def _():
            cos = cos_ref[...]
            sin = sin_ref[...]
            for c in range(groups):
                yc = y[:, c * LANES:(c + 1) * LANES]
                roped = yc * cos + pltpu.roll(yc, HALF_LANES, 1) * sin
                use = (g0 + c >= lo) & (g0 + c < hi)
                o_ref[:, c * LANES:(c + 1) * LANES] = jnp.where(use, roped, yc).astype(o_ref.dtype)

        @pl.when(jnp.logical_not(tile_has_rope))
        def _():
            o_ref[...] = y.astype(o_ref.dtype)
    elif transpose_out:
        o_ref[...] = y.T.astype(o_ref.dtype)
    else:
        o_ref[...] = y.astype(o_ref.dtype)


def _linear(x, w, *, n_out, k=None, x_col_block=0, w_tile_offsets=(0,), norm_w=None, residual=None,
            rope=None, act=None, out_dtype=None, tm=1024, tn=512, transpose_out=False, name="linear"):
    m = x.shape[0]
    k = x.shape[1] if k is None else k
    tm = min(tm, m)
    tn = min(tn, n_out)
    assert m % tm == 0 and n_out % tn == 0 and w.shape[0] == k
    out_dtype = BF16 if out_dtype is None else out_dtype
    has_norm = norm_w is not None
    prologue = has_norm or x.dtype != BF16
    grid = (m // tm, n_out // tn)

    in_specs = [pl.BlockSpec((tm, k), lambda i, j: (i, x_col_block))]
    args = [x]
    if has_norm:
        in_specs.append(pl.BlockSpec((1, k), lambda i, j: (0, 0)))
        args.append(norm_w.reshape(1, k).astype(F32))
    for off in w_tile_offsets:
        in_specs.append(pl.BlockSpec((k, tn), lambda i, j, off=off: (0, j + off)))
        args.append(w)
    rope_range = None
    if rope is not None:
        cos, sin, lo, hi, seq = rope
        assert seq % tm == 0
        nrep = seq // tm
        for tab in (cos, sin):
            in_specs.append(pl.BlockSpec((tm, LANES), lambda i, j: (i % nrep, 0)))
            args.append(tab)
        rope_range = (lo, hi)
    if residual is not None:
        in_specs.append(pl.BlockSpec((tm, tn), lambda i, j: (i, j)))
        args.append(residual)
    scratch = [pltpu.VMEM((tm, k), BF16)] if prologue else []

    kern = functools.partial(_linear_kernel, has_norm=has_norm, prologue=prologue,
                             n_w=len(w_tile_offsets), has_res=residual is not None,
                             rope=rope_range, act=act, transpose_out=transpose_out)
    if transpose_out:
        assert rope is None
        out_specs = pl.BlockSpec((tn, tm), lambda i, j: (j, i))
        out_shape = jax.ShapeDtypeStruct((n_out, m), out_dtype)
    else:
        out_specs = pl.BlockSpec((tm, tn), lambda i, j: (i, j))
        out_shape = jax.ShapeDtypeStruct((m, n_out), out_dtype)
    return pl.pallas_call(
        kern,
        grid=grid,
        in_specs=in_specs,
        out_specs=out_specs,
        out_shape=out_shape,
        scratch_shapes=scratch,
        compiler_params=_cparams(("parallel", "arbitrary")),
        name=name,
    )(*args)


def _rmsnorm_kernel(x_ref, w_ref, o_ref):
    xf = x_ref[...]
    ms = jnp.mean(xf * xf, axis=-1, keepdims=True)
    o_ref[...] = xf * lax.rsqrt(ms + NORM_EPS) * w_ref[...]


def _rmsnorm(x, w, tm=512):
    m, d = x.shape
    tm = min(tm, m)
    return pl.pallas_call(
        _rmsnorm_kernel,
        grid=(m // tm,),
        in_specs=[pl.BlockSpec((tm, d), lambda i: (i, 0)), pl.BlockSpec((1, d), lambda i: (0, 0))],
        out_specs=pl.BlockSpec((tm, d), lambda i: (i, 0)),
        out_shape=jax.ShapeDtypeStruct((m, d), F32),
        compiler_params=_cparams(("parallel",)),
    )(x, w.reshape(1, d))


def _rope_tables(pos, half):
    inv_freq = ROPE_THETA ** (-np.arange(half, dtype=np.float64) / half)
    ang = np.asarray(pos, np.float64)[:, None] * inv_freq[None, :]
    reps = LANES // half
    cos = np.tile(np.cos(ang), (1, reps))
    sin = np.tile(np.sin(ang), (1, reps))
    sign = np.where(np.arange(LANES) < HALF_LANES, -1.0, 1.0)
    return jnp.asarray(cos, F32), jnp.asarray(sin * sign, F32)


def _tri_tables(nq, reverse=False):
    qi, ki = [], []
    for q in range(nq):
        for kk in (range(q, -1, -1) if reverse else range(q + 1)):
            qi.append(q)
            ki.append(kk)
    return jnp.asarray(qi, jnp.int32), jnp.asarray(ki, jnp.int32)


def _chain_init(m_sc, acc_sc):
    m_sc[...] = jnp.full_like(m_sc, NEG)
    acc_sc[...] = jnp.zeros_like(acc_sc)


def _with_ones(v_t):
    return jnp.concatenate([v_t, jnp.ones((ONES_ROWS, v_t.shape[1]), v_t.dtype)], axis=0)


def _chain_update(c, s, mask, v_t1, m_sc, acc_sc):
    if mask is not None:
        s = jnp.where(mask, s, NEG)
    m_prev = m_sc[c]
    m_new = jnp.maximum(m_prev, jnp.max(s, axis=0, keepdims=True))
    alpha = jnp.exp2(m_prev - m_new)
    p = jnp.exp2(s - m_new)
    acc_sc[c] = alpha * acc_sc[c] + _dot(v_t1, p.astype(BF16))
    m_sc[c] = m_new


def _chain_result(c, acc_sc):
    acc = acc_sc[c]
    return acc[:LANES] * (1.0 / acc[LANES:LANES + 1])


def _chain_scratch(n_chains, t):
    return [pltpu.VMEM((n_chains, 1, t), F32), pltpu.VMEM((n_chains, LANES + ONES_ROWS, t), F32)]


def _tile_causal(t):
    return lax.broadcasted_iota(jnp.int32, (t, t), 0) <= lax.broadcasted_iota(jnp.int32, (t, t), 1)


def _head_rows(ref, h):
    return ref[h * LANES:(h + 1) * LANES, :]


def _heads(ref, h):
    return ref[:, h * LANES:(h + 1) * LANES]


def _da_kernel(qt_ref, kt_ref, q_ref, k_ref, vt_ref, lam_ref, sub_ref, o_ref, m_sc, acc_sc,
               *, lam_init, t, hb):
    step = pl.program_id(2)
    qi = qt_ref[step]
    ki = kt_ref[step]

    @pl.when(ki == 0)
    def _():
        _chain_init(m_sc, acc_sc)

    lane = lax.broadcasted_iota(jnp.int32, (1, LANES), 1)
    first_map = (lane % HALF_LANES) < (HALF_LANES // 2)

    def update(mask):
        for h in range(hb):
            q = _heads(q_ref, h)
            k = _heads(k_ref, h)
            v_t1 = _with_ones(_head_rows(vt_ref, h))
            zero = jnp.zeros_like(q)
            for j, qm in enumerate((jnp.where(first_map, q, zero), jnp.where(first_map, zero, q))):
                _chain_update(2 * h + j, _dot_nt(k, qm), mask, v_t1, m_sc, acc_sc)

    @pl.when(ki < qi)
    def _():
        update(None)

    @pl.when(ki == qi)
    def _():
        update(_tile_causal(t))
        lam = lam_ref[...]
        lam_full = (jnp.exp(jnp.sum(lam[0:1] * lam[1:2], axis=-1, keepdims=True))
                    - jnp.exp(jnp.sum(lam[2:3] * lam[3:4], axis=-1, keepdims=True)) + lam_init)
        for h in range(hb):
            o = _chain_result(2 * h, acc_sc) - lam_full * _chain_result(2 * h + 1, acc_sc)
            ms = jnp.mean(o * o, axis=0, keepdims=True)
            o = o * lax.rsqrt(ms + NORM_EPS) * sub_ref[...]
            o_ref[:, h * LANES:(h + 1) * LANES] = (o * (1.0 - lam_init)).T.astype(o_ref.dtype)


def _da_column_order(heads):
    dk = DA_QK_DIM
    hd = dk // 2
    per = 4 * dk + DA_V_DIM
    q_idx, k_idx, v_idx = [], [], []
    for h in range(heads):
        base = h * per
        for out, off in ((q_idx, 0), (k_idx, 2 * dk)):
            a = base + off + np.arange(dk)
            b = base + off + dk + np.arange(dk)
            out += [a[:hd], b[:hd], a[hd:], b[hd:]]
        v_idx.append(base + 4 * dk + np.arange(DA_V_DIM))
    return np.concatenate(q_idx + k_idx + v_idx)


def _diff_attention(hres, norm_w, w_in, lam, subln_w, w_out, lam_init, batch, seq):
    m, d = hres.shape
    heads = w_in.shape[1] // (4 * DA_QK_DIM + DA_V_DIM)
    hw = heads * LANES
    col_scale = np.ones(3 * hw, np.float32)
    col_scale[:hw] = DA_QK_DIM ** -0.5 * LOG2E
    w_perm = (w_in[:, _da_column_order(heads)] * col_scale[None, :]).astype(BF16)
    cos, sin = _rope_tables(np.arange(seq), DA_QK_DIM // 2)
    tn = 512
    proj = _linear(hres, w_perm[:, :2 * hw], n_out=2 * hw, norm_w=norm_w, tn=tn,
                   rope=(cos, sin, 0, 2 * hw // LANES, seq), name="da_qk_proj")
    v_t = _linear(hres, w_perm[:, 2 * hw:], n_out=hw, norm_w=norm_w, tn=tn, transpose_out=True,
                  name="da_vt_proj")

    t = min(ATTN_TILE, seq)
    nq = seq // t
    qt, kt = _tri_tables(nq)
    hb = min(ATTN_CHAINS // 2, heads)
    hg = heads // hb
    bw = hb * LANES
    kern = functools.partial(_da_kernel, lam_init=lam_init, t=t, hb=hb)
    attn = pl.pallas_call(
        kern,
        grid_spec=pltpu.PrefetchScalarGridSpec(
            num_scalar_prefetch=2,
            grid=(batch, hg, qt.shape[0]),
            in_specs=[
                pl.BlockSpec((t, bw), lambda b, h, s, qt, kt: (b * nq + qt[s], h)),
                pl.BlockSpec((t, bw), lambda b, h, s, qt, kt: (b * nq + kt[s], hg + h)),
                pl.BlockSpec((bw, t), lambda b, h, s, qt, kt: (h, b * nq + kt[s])),
                pl.BlockSpec(lam.shape, lambda b, h, s, qt, kt: (0, 0)),
                pl.BlockSpec((DA_V_DIM, 1), lambda b, h, s, qt, kt: (0, 0)),
            ],
            out_specs=pl.BlockSpec((t, bw), lambda b, h, s, qt, kt: (b * nq + qt[s], h)),
            scratch_shapes=_chain_scratch(2 * hb, t),
        ),
        out_shape=jax.ShapeDtypeStruct((m, hw), BF16),
        compiler_params=_cparams(("parallel", "parallel", "arbitrary")),
        name="da_attention",
    )(qt, kt, proj, proj, v_t, lam.astype(F32), subln_w.reshape(DA_V_DIM, 1).astype(F32))
    return _linear(attn, w_out.astype(BF16), n_out=d, residual=hres, out_dtype=F32, name="da_out_proj")


def _mla_kernel(qt_ref, kt_ref, qn_ref, qr_ref, kn_ref, kr_ref, vt_ref, o_ref, m_sc, acc_sc, *, t, hb):
    step = pl.program_id(2)
    qi = qt_ref[step]
    ki = kt_ref[step]

    @pl.when(ki == 0)
    def _():
        _chain_init(m_sc, acc_sc)

    kr = kr_ref[...].astype(BF16)

    def update(mask):
        for h in range(hb):
            q = jnp.concatenate([_heads(qn_ref, h), _heads(qr_ref, h)], axis=1)
            k = jnp.concatenate([_heads(kn_ref, h), kr], axis=1)
            _chain_update(h, _dot_nt(k, q), mask, _with_ones(_head_rows(vt_ref, h)), m_sc, acc_sc)

    @pl.when(ki < qi)
    def _():
        update(None)

    @pl.when(ki == qi)
    def _():
        update(_tile_causal(t))
        for h in range(hb):
            o_ref[:, h * LANES:(h + 1) * LANES] = _chain_result(h, acc_sc).T.astype(o_ref.dtype)


def _mla_q_columns(heads):
    per = MLA_NOPE_DIM + MLA_ROPE_DIM
    hd = MLA_ROPE_DIM // 2
    idx, keep = [], []
    for h in range(heads):
        idx.append(h * per + np.arange(MLA_NOPE_DIM))
        keep.append(np.ones(MLA_NOPE_DIM))
    for h in range(heads):
        s = h % 2
        g_idx = np.zeros(LANES, np.int64)
        g_keep = np.zeros(LANES)
        r0 = h * per + MLA_NOPE_DIM
        g_idx[s * hd:(s + 1) * hd] = r0 + np.arange(hd)
        g_idx[HALF_LANES + s * hd:HALF_LANES + (s + 1) * hd] = r0 + hd + np.arange(hd)
        g_keep[s * hd:(s + 1) * hd] = 1.0
        g_keep[HALF_LANES + s * hd:HALF_LANES + (s + 1) * hd] = 1.0
        idx.append(g_idx)
        keep.append(g_keep)
    return np.concatenate(idx), np.concatenate(keep)


def _mla_attention(hres, norm_w, w_in, q_norm_w, w_uq, kv_norm_w, w_ukv, w_out, batch, seq):
    m, d = hres.shape
    heads = MLA_HEADS
    hw = heads * LANES
    hd = MLA_ROPE_DIM // 2
    assert MLA_Q_RANK == MLA_KV_RANK and MLA_NOPE_DIM == LANES and MLA_V_DIM == LANES
    rank = MLA_Q_RANK
    r0 = 2 * rank
    kr_cols = np.concatenate([r0 + np.arange(hd), r0 + np.arange(hd),
                              r0 + hd + np.arange(hd), r0 + hd + np.arange(hd)])
    w_in_x = jnp.concatenate([w_in[:, :2 * rank], w_in[:, kr_cols]], axis=1).astype(BF16)
    cos, sin = _rope_tables(np.arange(seq), hd)
    n_in = 2 * rank + LANES
    tn_in = 3 * LANES if n_in % (3 * LANES) == 0 else LANES
    proj = _linear(hres, w_in_x, n_out=n_in, norm_w=norm_w, tn=tn_in, out_dtype=F32,
                   rope=(cos, sin, 2 * rank // LANES, n_in // LANES, seq), name="mla_in_proj")

    q_idx, q_keep = _mla_q_columns(heads)
    c2 = (MLA_NOPE_DIM + MLA_ROPE_DIM) ** -0.5 * LOG2E
    w_uq_x = (w_uq[:, q_idx] * jnp.asarray(q_keep * c2, F32)[None, :]).astype(BF16)
    tn = 512
    qcat = _linear(proj, w_uq_x, n_out=2 * hw, k=rank, x_col_block=0, norm_w=q_norm_w, tn=tn,
                   rope=(cos, sin, hw // LANES, 2 * hw // LANES, seq), name="mla_q_proj")
    k_nope = _linear(proj, w_ukv[:, :, :MLA_NOPE_DIM].reshape(rank, hw).astype(BF16), n_out=hw, k=rank,
                     x_col_block=1, norm_w=kv_norm_w, tn=tn, name="mla_k_proj")
    v_t = _linear(proj, w_ukv[:, :, MLA_NOPE_DIM:].reshape(rank, hw).astype(BF16), n_out=hw, k=rank,
                  x_col_block=1, norm_w=kv_norm_w, tn=tn, transpose_out=True, name="mla_vt_proj")

    t = min(ATTN_TILE, seq)
    nq = seq // t
    qt, kt = _tri_tables(nq)
    kr_block = 2 * rank // LANES
    hb = min(ATTN_CHAINS, heads)
    hg = heads // hb
    bw = hb * LANES
    kern = functools.partial(_mla_kernel, t=t, hb=hb)
    attn = pl.pallas_call(
        kern,
        grid_spec=pltpu.PrefetchScalarGridSpec(
            num_scalar_prefetch=2,
            grid=(batch, hg, qt.shape[0]),
            in_specs=[
                pl.BlockSpec((t, bw), lambda b, h, s, qt, kt: (b * nq + qt[s], h)),
                pl.BlockSpec((t, bw), lambda b, h, s, qt, kt: (b * nq + qt[s], hg + h)),
                pl.BlockSpec((t, bw), lambda b, h, s, qt, kt: (b * nq + kt[s], h)),
                pl.BlockSpec((t, LANES), lambda b, h, s, qt, kt: (b * nq + kt[s], kr_block)),
                pl.BlockSpec((bw, t), lambda b, h, s, qt, kt: (h, b * nq + kt[s])),
            ],
            out_specs=pl.BlockSpec((t, bw), lambda b, h, s, qt, kt: (b * nq + qt[s], h)),
            scratch_shapes=_chain_scratch(hb, t),
        ),
        out_shape=jax.ShapeDtypeStruct((m, hw), BF16),
        compiler_params=_cparams(("parallel", "parallel", "arbitrary")),
        name="mla_attention",
    )(qt, kt, qcat, qcat, k_nope, proj, v_t)
    return _linear(attn, w_out.astype(BF16), n_out=d, residual=hres, out_dtype=F32, name="mla_out_proj")


def _sb_kernel(qt_ref, kt_ref, q_ref, k_ref, v_ref, tri_ref, o_ref, carry_sc, acc_sc, *, t, hb):
    step = pl.program_id(2)
    qi = qt_ref[step]
    ki = kt_ref[step]

    @pl.when(ki == qi)
    def _():
        carry_sc[...] = jnp.zeros_like(carry_sc)
        acc_sc[...] = jnp.zeros_like(acc_sc)

    tri = tri_ref[...]
    chunk = tri.shape[0]

    def update(masked):
        if masked:
            mask = lax.broadcasted_iota(jnp.int32, (t, t), 1) < lax.broadcasted_iota(jnp.int32, (t, t), 0)
        for h in range(hb):
            z = _dot_nt(_heads(q_ref, h), _heads(k_ref, h))
            log_beta = jnp.minimum(z, 0.0) - jnp.log2(1.0 + jnp.exp2(-jnp.abs(z)))
            log_rest = log_beta - z
            if masked:
                log_rest = jnp.where(mask, log_rest, 0.0)
            carry = carry_sc[h]
            parts = []
            for c in range(t // chunk - 1, -1, -1):
                cols = slice(c * chunk, (c + 1) * chunk)
                hi, lo = _split_bf16(log_rest[:, cols])
                tail = _dot(hi, tri) + _dot(lo, tri) + carry
                parts.append(jnp.exp2(log_beta[:, cols] + tail))
                carry = carry + jnp.sum(log_rest[:, cols], axis=-1, keepdims=True)
            a = jnp.concatenate(parts[::-1], axis=1)
            if masked:
                a = jnp.where(mask, a, 0.0)
            acc_sc[h] += _dot(a.astype(BF16), _heads(v_ref, h))
            carry_sc[h] = carry

    @pl.when(ki == qi)
    def _():
        update(True)

    @pl.when(ki < qi)
    def _():
        update(False)

    @pl.when(ki == 0)
    def _():
        for h in range(hb):
            o_ref[:, h * LANES:(h + 1) * LANES] = acc_sc[h].astype(o_ref.dtype)


def _sb_attention(hres, norm_w, w_in, w_out, batch, seq):
    m, d = hres.shape
    heads = SB_HEADS
    hw = heads * LANES
    assert w_in.shape[1] == 3 * hw
    col_scale = np.ones(3 * hw, np.float32)
    col_scale[:hw] = LANES ** -0.5 * LOG2E
    qkv = _linear(hres, (w_in * col_scale[None, :]).astype(BF16), n_out=3 * hw, norm_w=norm_w,
                  name="sb_qkv_proj")
    t = min(ATTN_TILE, seq)
    nq = seq // t
    qt, kt = _tri_tables(nq, reverse=True)
    chunk = min(2 * LANES, t)
    tri = jnp.asarray(np.arange(chunk)[:, None] > np.arange(chunk)[None, :], BF16)
    hb = min(SB_HEADS_PER_STEP, heads)
    hg = heads // hb
    bw = hb * LANES
    kern = functools.partial(_sb_kernel, t=t, hb=hb)
    attn = pl.pallas_call(
        kern,
        grid_spec=pltpu.PrefetchScalarGridSpec(
            num_scalar_prefetch=2,
            grid=(batch, hg, qt.shape[0]),
            in_specs=[
                pl.BlockSpec((t, bw), lambda b, h, s, qt, kt: (b * nq + qt[s], h)),
                pl.BlockSpec((t, bw), lambda b, h, s, qt, kt: (b * nq + kt[s], hg + h)),
                pl.BlockSpec((t, bw), lambda b, h, s, qt, kt: (b * nq + kt[s], 2 * hg + h)),
                pl.BlockSpec((chunk, chunk), lambda b, h, s, qt, kt: (0, 0)),
            ],
            out_specs=pl.BlockSpec((t, bw), lambda b, h, s, qt, kt: (b * nq + qt[s], h)),
            scratch_shapes=[pltpu.VMEM((hb, t, 1), F32), pltpu.VMEM((hb, t, LANES), F32)],
        ),
        out_shape=jax.ShapeDtypeStruct((m, hw), BF16),
        compiler_params=_cparams(("parallel", "parallel", "arbitrary")),
        name="sb_attention",
    )(qt, kt, qkv, qkv, qkv, tri)
    return _linear(attn, w_out.astype(BF16), n_out=d, residual=hres, out_dtype=F32, name="sb_out_proj")


def _nsa_compress_kernel(x_ref, w_ref, pe_ref, cos_ref, sin_ref, o_ref, *, rope, transpose_out):
    x = x_ref[0]
    half = w_ref.shape[0] // 2
    y_lo = _dot(x, w_ref[:half, :])
    y_hi = _dot(x, w_ref[half:, :])
    n = y_hi.shape[0]
    y = y_lo + pltpu.roll(y_hi, n - 1, 0) + _dot(pe_ref[...], w_ref[...])[0:1]
    if rope:
        y = y * cos_ref[...] + pltpu.roll(y, HALF_LANES, 1) * sin_ref[...]
    o_ref[0] = (y.T if transpose_out else y).astype(o_ref.dtype)


def _nsa_compress(x_chunks, w, pe, cos, sin, rope, transpose_out):
    bg, n_chunk, kdim = x_chunks.shape
    dh = w.shape[1]
    out_block = (1, dh, n_chunk) if transpose_out else (1, n_chunk, dh)
    return pl.pallas_call(
        functools.partial(_nsa_compress_kernel, rope=rope, transpose_out=transpose_out),
        grid=(bg,),
        in_specs=[pl.BlockSpec((1, n_chunk, kdim), lambda i: (i, 0, 0)),
                  pl.BlockSpec(w.shape, lambda i: (0, 0)),
                  pl.BlockSpec(pe.shape, lambda i: (0, 0)),
                  pl.BlockSpec(cos.shape, lambda i: (0, 0)),
                  pl.BlockSpec(sin.shape, lambda i: (0, 0))],
        out_specs=pl.BlockSpec(out_block, lambda i: (i, 0, 0)),
        out_shape=jax.ShapeDtypeStruct((bg,) + out_block[1:], BF16),
        compiler_params=_cparams(("parallel",)),
        name="nsa_compress",
    )(x_chunks, w, pe, cos, sin)


def _head_gate(gates, branch, head):
    lane = lax.broadcasted_iota(jnp.int32, gates.shape, 1)
    return jnp.sum(jnp.where(lane == branch * NSA_HEADS + head, gates, 0.0), axis=-1, keepdims=True)


def _nsa_cmp_kernel(q_ref, kc_ref, vct_ref, ovt_ref, g_ref, o_ref, sel_ref, *, tq, rep, n_sel):
    qi = pl.program_id(2)
    grp = pl.program_id(1)
    qs = jnp.concatenate([q_ref[:, r * LANES:(r + 1) * LANES] for r in range(rep)], axis=0)
    kc = kc_ref[0]
    n_cmp = kc.shape[0]
    rows = rep * tq

    st = _dot_nt(kc, qs)
    qpos_t = qi * tq + lax.broadcasted_iota(jnp.int32, (n_cmp, rep, tq), 2).reshape(n_cmp, rows)
    cend_t = lax.broadcasted_iota(jnp.int32, (n_cmp, rows), 0) * NSA_CMP_STRIDE + (NSA_CMP_BLOCK - 1)
    mask_t = cend_t <= qpos_t
    st = jnp.where(mask_t, st, NEG)
    pt = jnp.where(mask_t, jnp.exp2(st - jnp.max(st, axis=0, keepdims=True)), 0.0)
    den_t = jnp.sum(pt, axis=0, keepdims=True)
    pt = pt / jnp.where(den_t > 0.0, den_t, 1.0)
    o_t = _dot(vct_ref[0], pt.astype(BF16))
    gates = g_ref[...]
    for r in range(rep):
        gate = _head_gate(gates, 0, grp * rep + r)
        o_ref[:, r * LANES:(r + 1) * LANES] = gate * o_t[:, r * tq:(r + 1) * tq].T

    psum = pt[:, 0:tq]
    for r in range(1, rep):
        psum = psum + pt[:, r * tq:(r + 1) * tq]
    hi, lo = _split_bf16(psum)
    ovt = ovt_ref[...]
    imp = _dot(ovt, hi) + _dot(ovt, lo)

    n_pad = imp.shape[0]
    blk = lax.broadcasted_iota(jnp.int32, (n_pad, tq), 0)
    cur = (qi * tq + lax.broadcasted_iota(jnp.int32, (n_pad, tq), 1)) // NSA_SLC_BLOCK
    forced = (blk == 0) | (blk == cur) | (blk == cur - 1)
    val = jnp.where(forced, jnp.inf, jnp.where(blk <= cur, imp, -jnp.inf))
    sel = jnp.zeros((n_pad, tq), F32)
    for _ in range(n_sel):
        top = jnp.max(val, axis=0, keepdims=True)
        first = jnp.min(jnp.where(val == top, blk, n_pad), axis=0, keepdims=True)
        pick = blk == first
        sel = jnp.where(pick, 1.0, sel)
        val = jnp.where(pick, -jnp.inf, val)
    sel_ref[...] = sel.astype(sel_ref.dtype)


def _nsa_win_kernel(q_ref, k_ref, vt_ref, g_ref, prev_ref, o_ref, m_sc, acc_sc, *, t, rep, nwin):
    qi = pl.program_id(2)
    kk = pl.program_id(3)
    grp = pl.program_id(1)
    kb = qi - (nwin - 1) + kk

    @pl.when(kk == 0)
    def _():
        _chain_init(m_sc, acc_sc)

    def update(mask):
        k = k_ref[...]
        v_t1 = _with_ones(vt_ref[...])
        for r in range(rep):
            _chain_update(r, _dot_nt(k, _heads(q_ref, r)), mask, v_t1, m_sc, acc_sc)

    @pl.when((kk == 0) & (kb >= 0))
    def _():
        update(jnp.logical_not(_tile_causal(t)))

    @pl.when((kk > 0) & (kk < nwin - 1) & (kb >= 0))
    def _():
        update(None)

    @pl.when(kk == nwin - 1)
    def _():
        update(_tile_causal(t))
        gates = g_ref[...]
        for r in range(rep):
            gate = _head_gate(gates, 2, grp * rep + r)
            o_ref[:, r * LANES:(r + 1) * LANES] = _heads(prev_ref, r) + gate * _chain_result(r, acc_sc).T


def _nsa_slc_kernel(qt_ref, kt_ref, q_ref, k_ref, vt_ref, sel_ref, g_ref, prev_ref, o_ref,
                    m_sc, acc_sc, *, t, rep, gb):
    step = pl.program_id(2)
    grp0 = pl.program_id(1) * gb
    qi = qt_ref[step]
    ki = kt_ref[step]

    @pl.when(ki == 0)
    def _():
        _chain_init(m_sc, acc_sc)

    n_pad = sel_ref.shape[0] // gb
    blk = lax.broadcasted_iota(jnp.int32, (t, n_pad), 1)
    kblk = (ki * t + lax.broadcasted_iota(jnp.int32, (t, n_pad), 0)) // NSA_SLC_BLOCK
    expand = jnp.where(blk == kblk, 1.0, 0.0).astype(BF16)

    def update(diagonal):
        for g in range(gb):
            mask = _dot(expand, sel_ref[g * n_pad:(g + 1) * n_pad, :]) > 0.5
            if diagonal:
                mask = mask & _tile_causal(t)
            k = _heads(k_ref, g)
            v_t1 = _with_ones(_head_rows(vt_ref, g))
            for r in range(rep):
                c = g * rep + r
                _chain_update(c, _dot_nt(k, _heads(q_ref, c)), mask, v_t1, m_sc, acc_sc)

    @pl.when(ki < qi)
    def _():
        update(False)

    @pl.when(ki == qi)
    def _():
        update(True)
        gates = g_ref[...]
        for c in range(gb * rep):
            gate = _head_gate(gates, 1, grp0 * rep + c)
            o = _chain_result(c, acc_sc).T
            o_ref[:, c * LANES:(c + 1) * LANES] = (_heads(prev_ref, c) + gate * o).astype(o_ref.dtype)


def _nsa_attention(hres, norm_w, w_in, cmp_pe, cmp_w, w_out, batch, seq):
    m, d = hres.shape
    heads, groups = NSA_HEADS, NSA_GROUPS
    rep = heads // groups
    dh = d // heads
    assert dh == LANES
    kvw = groups * dh
    n_main = heads * dh + 6 * kvw
    gw = rep * LANES

    hq = heads * dh
    cos, sin = _rope_tables(np.arange(seq), dh // 2)
    kv_cols = lambda i: np.arange(hq + i * kvw, hq + (i + 1) * kvw)
    main_cols = np.concatenate([np.arange(hq), kv_cols(0), kv_cols(1), kv_cols(2), kv_cols(4)])
    col_scale = np.ones(main_cols.size, np.float32)
    col_scale[:hq] = dh ** -0.5 * LOG2E
    proj = _linear(hres, (w_in[:, main_cols] * col_scale[None, :]).astype(BF16), n_out=main_cols.size,
                   norm_w=norm_w, rope=(cos, sin, 0, hq // LANES, seq), name="nsa_qk_proj")
    v_t = _linear(hres, w_in[:, np.concatenate([kv_cols(3), kv_cols(5)])].astype(BF16), n_out=2 * kvw,
                  norm_w=norm_w, transpose_out=True, name="nsa_vt_proj")
    w_gate = jnp.pad(w_in[:, n_main:], ((0, 0), (0, LANES - 3 * heads))).astype(BF16)
    gates = _linear(hres, w_gate, n_out=LANES, norm_w=norm_w, act="sigmoid", out_dtype=F32,
                    name="nsa_gate_proj")
    k_slc_blk = (hq + 2 * kvw) // LANES
    k_win_blk = (hq + 3 * kvw) // LANES

    n_chunk = seq // NSA_CMP_STRIDE
    assert NSA_CMP_BLOCK == 2 * NSA_CMP_STRIDE
    cmp_end = np.arange(n_chunk) * NSA_CMP_STRIDE + NSA_CMP_BLOCK - 1
    ccos, csin = _rope_tables(cmp_end, dh // 2)

    def chunks(i):
        lo = hq + i * kvw
        x = proj[:, lo:lo + kvw].reshape(batch, n_chunk, NSA_CMP_STRIDE, groups, dh)
        return x.transpose(0, 3, 1, 2, 4).reshape(batch * groups, n_chunk, NSA_CMP_STRIDE * dh)

    def cmp_weights(i):
        return (cmp_w[i].reshape(NSA_CMP_BLOCK * dh, dh).astype(BF16),
                jnp.broadcast_to(cmp_pe[i].reshape(1, NSA_CMP_BLOCK * dh), (8, NSA_CMP_BLOCK * dh)).astype(BF16))

    k_cmp = _nsa_compress(chunks(0), *cmp_weights(0), ccos, csin, True, False)
    v_cmp_t = _nsa_compress(chunks(1), *cmp_weights(1), ccos, csin, False, True)

    n_slc = seq // NSA_SLC_BLOCK
    n_sel = min(NSA_SLC_TOPK, n_slc)
    n_pad = max(LANES, n_slc)
    n_cmp = n_chunk - NSA_CMP_BLOCK // NSA_CMP_STRIDE + 1
    cs = np.arange(n_chunk)[None, :] * NSA_CMP_STRIDE
    js = np.arange(n_pad)[:, None] * NSA_SLC_BLOCK
    ovt = ((cs < js + NSA_SLC_BLOCK) & (cs + NSA_CMP_BLOCK > js)
           & (np.arange(n_chunk)[None, :] < n_cmp) & (np.arange(n_pad)[:, None] < n_slc))
    ovt = jnp.asarray(ovt, BF16)

    tq = min(256, seq)
    nqc = seq // tq
    o_c, sel = pl.pallas_call(
        functools.partial(_nsa_cmp_kernel, tq=tq, rep=rep, n_sel=n_sel),
        grid=(batch, groups, nqc),
        in_specs=[
            pl.BlockSpec((tq, gw), lambda b, g, i: (b * nqc + i, g)),
            pl.BlockSpec((1, n_chunk, dh), lambda b, g, i: (b * groups + g, 0, 0)),
            pl.BlockSpec((1, dh, n_chunk), lambda b, g, i: (b * groups + g, 0, 0)),
            pl.BlockSpec(ovt.shape, lambda b, g, i: (0, 0)),
            pl.BlockSpec((tq, LANES), lambda b, g, i: (b * nqc + i, 0)),
        ],
        out_specs=[
            pl.BlockSpec((tq, gw), lambda b, g, i: (b * nqc + i, g)),
            pl.BlockSpec((n_pad, tq), lambda b, g, i: (b * groups + g, i)),
        ],
        out_shape=[jax.ShapeDtypeStruct((m, heads * dh), F32),
                   jax.ShapeDtypeStruct((batch * groups * n_pad, seq), BF16)],
        compiler_params=_cparams(("parallel", "parallel", "arbitrary")),
        name="nsa_compressed_select",
    )(proj, k_cmp, v_cmp_t, ovt, gates)

    t = min(512, seq)
    nq = seq // t
    assert NSA_WINDOW % t == 0
    nwin = NSA_WINDOW // t + 1
    key_tile = lambda i, kk: jnp.maximum(i - (nwin - 1) + kk, 0)
    o_cw = pl.pallas_call(
        functools.partial(_nsa_win_kernel, t=t, rep=rep, nwin=nwin),
        grid=(batch, groups, nq, nwin),
        in_specs=[
            pl.BlockSpec((t, gw), lambda b, g, i, kk: (b * nq + i, g)),
            pl.BlockSpec((t, LANES), lambda b, g, i, kk: (b * nq + key_tile(i, kk), k_win_blk + g)),
            pl.BlockSpec((LANES, t), lambda b, g, i, kk: (groups + g, b * nq + key_tile(i, kk))),
            pl.BlockSpec((t, LANES), lambda b, g, i, kk: (b * nq + i, 0)),
            pl.BlockSpec((t, gw), lambda b, g, i, kk: (b * nq + i, g)),
        ],
        out_specs=pl.BlockSpec((t, gw), lambda b, g, i, kk: (b * nq + i, g)),
        out_shape=jax.ShapeDtypeStruct((m, heads * dh), F32),
        scratch_shapes=_chain_scratch(rep, t),
        compiler_params=_cparams(("parallel", "parallel", "parallel", "arbitrary")),
        name="nsa_window",
    )(proj, proj, v_t, gates, o_c)

    ts = min(ATTN_TILE, seq)
    nqs = seq // ts
    qt, kt = _tri_tables(nqs)
    gb = min(groups, max(1, ATTN_CHAINS // rep))
    assert groups % gb == 0 and k_slc_blk % gb == 0
    gsteps = groups // gb
    attn = pl.pallas_call(
        functools.partial(_nsa_slc_kernel, t=ts, rep=rep, gb=gb),
        grid_spec=pltpu.PrefetchScalarGridSpec(
            num_scalar_prefetch=2,
            grid=(batch, gsteps, qt.shape[0]),
            in_specs=[
                pl.BlockSpec((ts, gb * gw), lambda b, g, s, qt, kt: (b * nqs + qt[s], g)),
                pl.BlockSpec((ts, gb * LANES), lambda b, g, s, qt, kt: (b * nqs + kt[s], k_slc_blk // gb + g)),
                pl.BlockSpec((gb * LANES, ts), lambda b, g, s, qt, kt: (g, b * nqs + kt[s])),
                pl.BlockSpec((gb * n_pad, ts), lambda b, g, s, qt, kt: (b * gsteps + g, qt[s])),
                pl.BlockSpec((ts, LANES), lambda b, g, s, qt, kt: (b * nqs + qt[s], 0)),
                pl.BlockSpec((ts, gb * gw), lambda b, g, s, qt, kt: (b * nqs + qt[s], g)),
            ],
            out_specs=pl.BlockSpec((ts, gb * gw), lambda b, g, s, qt, kt: (b * nqs + qt[s], g)),
            scratch_shapes=_chain_scratch(gb * rep, ts),
        ),
        out_shape=jax.ShapeDtypeStruct((m, heads * dh), BF16),
        compiler_params=_cparams(("parallel", "parallel", "arbitrary")),
        name="nsa_selected",
    )(qt, kt, proj, proj, v_t, sel, gates, o_cw)
    return _linear(attn, w_out.astype(BF16), n_out=d, residual=hres, out_dtype=F32, name="nsa_out_proj")


def _swiglu(hres, norm_w, w_gu, w_down):
    d = hres.shape[1]
    ff = w_down.shape[0]
    tn = 512
    hidden = _linear(hres, w_gu.astype(BF16), n_out=ff, w_tile_offsets=(0, ff // tn), norm_w=norm_w,
                     act="swiglu", tn=tn, name="ffn_gate_up")
    return _linear(hidden, w_down.astype(BF16), n_out=d, residual=hres, out_dtype=F32, name="ffn_down")


def _router_kernel(x_ref, nw_ref, w_ref, b_ref, f_ref, idx_ref, gate_ref, *, n_experts):
    xf = x_ref[...]
    ms = jnp.mean(xf * xf, axis=-1, keepdims=True)
    f = xf * lax.rsqrt(ms + NORM_EPS) * nw_ref[...]
    f_ref[...] = f
    fh, fl = _split_bf16(f)
    w = w_ref[...]
    wh, wl = _split_bf16(w)
    logits = _dot(fh, wh) + _dot(fl, wh) + _dot(fh, wl) + b_ref[...]
    lane = lax.broadcasted_iota(jnp.int32, logits.shape, 1)
    logits = jnp.where(lane < n_experts, logits, -jnp.inf)
    v1 = jnp.max(logits, axis=-1, keepdims=True)
    i1 = jnp.min(jnp.where(logits == v1, lane, LANES), axis=-1, keepdims=True)
    rest = jnp.where(lane == i1, -jnp.inf, logits)
    v2 = jnp.max(rest, axis=-1, keepdims=True)
    i2 = jnp.min(jnp.where(rest == v2, lane, LANES), axis=-1, keepdims=True)
    e2 = jnp.exp(v2 - v1)
    g1 = 1.0 / (1.0 + e2)
    g2 = e2 / (1.0 + e2)
    idx_ref[...] = jnp.where(lane == 0, i1, jnp.where(lane == 1, i2, 0))
    gate_ref[...] = jnp.where(lane == 0, g1, jnp.where(lane == 1, g2, 0.0))


def _gather_kernel(src_ref, nvalid_ref, x_hbm, o_ref, buf, sem):
    blk = pl.program_id(0)
    rows = buf.shape[0]
    nvalid = nvalid_ref[blk]

    def row_copy(r):
        return pltpu.make_async_copy(x_hbm.at[pl.ds(src_ref[blk * rows + r], 1), :],
                                     buf.at[pl.ds(r, 1), :], sem)

    def start(pair, c):
        row_copy(2 * pair).start(priority=0)
        row_copy(2 * pair + 1).start(priority=1)
        return c

    def wait(r, c):
        row_copy(r).wait()
        return c

    @pl.when(nvalid > 0)
    def _():
        lax.fori_loop(0, rows // 2, start, 0, unroll=DMA_UNROLL // 2)
        lax.fori_loop(0, rows, wait, 0, unroll=DMA_UNROLL)
        rid = lax.broadcasted_iota(jnp.int32, (rows, 1), 0)
        o_ref[...] = jnp.where(rid < nvalid, buf[...], 0.0).astype(o_ref.dtype)

    @pl.when(nvalid == 0)
    def _():
        o_ref[...] = jnp.zeros_like(o_ref)


def _weight_tile_is_new(be_ref, blk):
    return (blk == 0) | (be_ref[blk] != be_ref[jnp.maximum(blk - 1, 0)])


def _expert_up_kernel(be_ref, nv_ref, x_ref, wg_ref, wu_ref, o_ref, wg_sc, wu_sc):
    blk = pl.program_id(1)

    @pl.when(_weight_tile_is_new(be_ref, blk))
    def _():
        wg_sc[...] = wg_ref[0, 0].astype(BF16)
        wu_sc[...] = wu_ref[0, 0].astype(BF16)

    @pl.when(nv_ref[blk] > 0)
    def _():
        x = x_ref[...]
        g = _dot(x, wg_sc[...])
        u = _dot(x, wu_sc[...])
        o_ref[...] = (g * jax.nn.sigmoid(g) * u).astype(o_ref.dtype)

    @pl.when(nv_ref[blk] == 0)
    def _():
        o_ref[...] = jnp.zeros_like(o_ref)


def _expert_down_kernel(be_ref, nv_ref, h_ref, w_ref, o_ref, w_sc):
    blk = pl.program_id(1)

    @pl.when(_weight_tile_is_new(be_ref, blk))
    def _():
        w_sc[...] = w_ref[0, 0].astype(BF16)

    @pl.when(nv_ref[blk] > 0)
    def _():
        o_ref[...] = _dot(h_ref[...], w_sc[...])

    @pl.when(nv_ref[blk] == 0)
    def _():
        o_ref[...] = jnp.zeros_like(o_ref)


def _combine_kernel(dst_ref, y_hbm, res_ref, gate_ref, o_ref, buf0, buf1, sem):
    i = pl.program_id(0)
    rows = buf0.shape[0]

    def row_copy(r, slot, buf):
        return pltpu.make_async_copy(y_hbm.at[pl.ds(dst_ref[(i * rows + r) * TOP_K + slot], 1), :],
                                     buf.at[pl.ds(r, 1), :], sem.at[slot])

    def start(r, c):
        row_copy(r, 0, buf0).start(priority=0)
        row_copy(r, 1, buf1).start(priority=1)
        return c

    def wait(r, c):
        row_copy(r, 0, buf0).wait()
        row_copy(r, 1, buf1).wait()
        return c

    lax.fori_loop(0, rows, start, 0, unroll=DMA_UNROLL)
    lax.fori_loop(0, rows, wait, 0, unroll=DMA_UNROLL)
    gate = gate_ref[...]
    o_ref[...] = res_ref[...] + (buf0[...] * gate[:, 0:1] + buf1[...] * gate[:, 1:2])


def _moe(hres, norm_w, w_router, b_router, w_gu, w_down, layer):
    m, d = hres.shape
    n_exp = w_router.shape[1]
    ff = w_down.shape[2]
    rb = MOE_ROW_BLOCK
    assert TOP_K == 2 and m % rb == 0
    tm = min(512, m)

    w_r = jnp.pad(w_router, ((0, 0), (0, LANES - n_exp))).astype(F32)
    b_r = jnp.pad(b_router, (0, LANES - n_exp)).reshape(1, LANES).astype(F32)
    f, idx, gate = pl.pallas_call(
        functools.partial(_router_kernel, n_experts=n_exp),
        grid=(m // tm,),
        in_specs=[pl.BlockSpec((tm, d), lambda i: (i, 0)),
                  pl.BlockSpec((1, d), lambda i: (0, 0)),
                  pl.BlockSpec((d, LANES), lambda i: (0, 0)),
                  pl.BlockSpec((1, LANES), lambda i: (0, 0))],
        out_specs=[pl.BlockSpec((tm, d), lambda i: (i, 0)),
                   pl.BlockSpec((tm, LANES), lambda i: (i, 0)),
                   pl.BlockSpec((tm, LANES), lambda i: (i, 0))],
        out_shape=[jax.ShapeDtypeStruct((m, d), F32),
                   jax.ShapeDtypeStruct((m, LANES), jnp.int32),
                   jax.ShapeDtypeStruct((m, LANES), F32)],
        compiler_params=_cparams(("parallel",)),
        name="moe_router",
    )(hres, norm_w.reshape(1, d).astype(F32), w_r, b_r)

    nk = m * TOP_K
    e_flat = idx[:, :TOP_K].reshape(nk)
    onehot = (e_flat[:, None] == jnp.arange(n_exp, dtype=jnp.int32)[None, :]).astype(jnp.int32)
    csum = jnp.cumsum(onehot, axis=0)
    rank = jnp.sum((csum - onehot) * onehot, axis=1)
    counts = csum[-1]
    padded = (counts + rb - 1) // rb * rb
    pend = jnp.cumsum(padded)
    pstart = pend - padded
    dest = (pstart[e_flat] + rank).astype(jnp.int32)
    n_rows = (-(-nk // rb)) * rb + n_exp * rb
    n_blocks = n_rows // rb
    src_tok = jnp.zeros((n_rows,), jnp.int32).at[dest].set(jnp.arange(nk, dtype=jnp.int32) // TOP_K)
    blk_start = jnp.arange(n_blocks, dtype=jnp.int32) * rb
    blk_expert = jnp.minimum(jnp.searchsorted(pend, blk_start, side="right"), n_exp - 1).astype(jnp.int32)
    nvalid = jnp.clip(pstart[blk_expert] + counts[blk_expert] - blk_start, 0, rb).astype(jnp.int32)

    xbuf = pl.pallas_call(
        _gather_kernel,
        grid_spec=pltpu.PrefetchScalarGridSpec(
            num_scalar_prefetch=2,
            grid=(n_blocks,),
            in_specs=[pl.BlockSpec(memory_space=pl.ANY)],
            out_specs=pl.BlockSpec((rb, d), lambda i, src, nv: (i, 0)),
            scratch_shapes=[pltpu.VMEM((rb, d), F32), pltpu.SemaphoreType.DMA(())],
        ),
        out_shape=jax.ShapeDtypeStruct((n_rows, d), BF16),
        compiler_params=_cparams(("arbitrary",)),
        name="moe_gather",
    )(src_tok, nvalid, f)

    tn = 512
    hidden = pl.pallas_call(
        _expert_up_kernel,
        grid_spec=pltpu.PrefetchScalarGridSpec(
            num_scalar_prefetch=2,
            grid=(ff // tn, n_blocks),
            in_specs=[pl.BlockSpec((rb, d), lambda j, i, be, nv: (i, 0)),
                      pl.BlockSpec((1, 1, d, tn), lambda j, i, be, nv: (layer, be[i], 0, j)),
                      pl.BlockSpec((1, 1, d, tn), lambda j, i, be, nv: (layer, be[i], 0, j + ff // tn))],
            out_specs=pl.BlockSpec((rb, tn), lambda j, i, be, nv: (i, j)),
            scratch_shapes=[pltpu.VMEM((d, tn), BF16), pltpu.VMEM((d, tn), BF16)],
        ),
        out_shape=jax.ShapeDtypeStruct((n_rows, ff), BF16),
        compiler_params=_cparams(("arbitrary", "arbitrary")),
        name="moe_expert_up",
    )(blk_expert, nvalid, xbuf, w_gu, w_gu)

    tnd = min(512, d)
    ybuf = pl.pallas_call(
        _expert_down_kernel,
        grid_spec=pltpu.PrefetchScalarGridSpec(
            num_scalar_prefetch=2,
            grid=(d // tnd, n_blocks),
            in_specs=[pl.BlockSpec((rb, ff), lambda j, i, be, nv: (i, 0)),
                      pl.BlockSpec((1, 1, ff, tnd), lambda j, i, be, nv: (layer, be[i], 0, j))],
            out_specs=pl.BlockSpec((rb, tnd), lambda j, i, be, nv: (i, j)),
            scratch_shapes=[pltpu.VMEM((ff, tnd), BF16)],
        ),
        out_shape=jax.ShapeDtypeStruct((n_rows, d), F32),
        compiler_params=_cparams(("arbitrary", "arbitrary")),
        name="moe_expert_down",
    )(blk_expert, nvalid, hidden, w_down)

    return pl.pallas_call(
        _combine_kernel,
        grid_spec=pltpu.PrefetchScalarGridSpec(
            num_scalar_prefetch=1,
            grid=(m // tm,),
            in_specs=[pl.BlockSpec(memory_space=pl.ANY),
                      pl.BlockSpec((tm, d), lambda i, dst: (i, 0)),
                      pl.BlockSpec((tm, LANES), lambda i, dst: (i, 0))],
            out_specs=pl.BlockSpec((tm, d), lambda i, dst: (i, 0)),
            scratch_shapes=[pltpu.VMEM((tm, d), F32), pltpu.VMEM((tm, d), F32),
                            pltpu.SemaphoreType.DMA((TOP_K,))],
        ),
        out_shape=jax.ShapeDtypeStruct((m, d), F32),
        compiler_params=_cparams(("arbitrary",)),
        name="moe_combine",
    )(dest, ybuf, hres, gate)


def kernel(x, attn_norm_w, ffn_norm_w, final_norm_w, da_w_in, da_lambda, da_subln_w, da_w_out, nsa_w_in, nsa_cmp_pe, nsa_cmp_w, nsa_w_out, mla_w_in, mla_q_norm_w, mla_w_uq, mla_kv_norm_w, mla_w_ukv, mla_w_out, sb_w_in, sb_w_out, ffn_w_gu, ffn_w_down, moe_w_router, moe_b_router, moe_w_gu, moe_w_down):
    batch, seq, d = x.shape
    depth = attn_norm_w.shape[0]
    h = x.reshape(batch * seq, d)
    for i in range(depth):
        kind = i % 4
        j = i // 4
        if kind == 0:
            lam_init = 0.8 - 0.6 * math.exp(-0.3 * i)
            h = _diff_attention(h, attn_norm_w[i], da_w_in[j], da_lambda[j], da_subln_w[j], da_w_out[j],
                                lam_init, batch, seq)
        elif kind == 1:
            h = _nsa_attention(h, attn_norm_w[i], nsa_w_in[j], nsa_cmp_pe[j], nsa_cmp_w[j], nsa_w_out[j],
                               batch, seq)
        elif kind == 2:
            h = _mla_attention(h, attn_norm_w[i], mla_w_in[j], mla_q_norm_w[j], mla_w_uq[j],
                               mla_kv_norm_w[j], mla_w_ukv[j], mla_w_out[j], batch, seq)
        else:
            h = _sb_attention(h, attn_norm_w[i], sb_w_in[j], sb_w_out[j], batch, seq)
        if i % 2 == 0:
            h = _swiglu(h, ffn_norm_w[i], ffn_w_gu[i // 2], ffn_w_down[i // 2])
        else:
            h = _moe(h, ffn_norm_w[i], moe_w_router[i // 2], moe_b_router[i // 2],
                     moe_w_gu, moe_w_down, i // 2)
    return _rmsnorm(h, final_norm_w.astype(F32)).reshape(batch, seq, d)
```
